```python
import math
import jax, jax.numpy as jnp
from jax import lax
import numpy as np

D_MODEL = 2048
BATCH = 16
SEQ = 2048
DEPTH = 1

CHUNK = 64
N_META = 16
ATTN_HEADS = 8
ATTN_HEAD_DIM = 64
ATTN_VDIM = 2 * ATTN_HEAD_DIM
ATTN_WIDTH = ATTN_HEADS * 2 * ATTN_HEAD_DIM
Q_BLOCK = 128
SSM_WIDTH = D_MODEL // 4
SSM_GROUP = 16
SSM_GROUPS = SSM_WIDTH // SSM_GROUP
SSM_STATE = 64
DT_MIN = 1e-3
DT_MAX = 1e-1
N_EXPERT_GROUPS = 4
EXPERTS_PER_GROUP = 8
N_EXPERTS = N_EXPERT_GROUPS * EXPERTS_PER_GROUP
TOP_K = 2
D_FF_EXPERT = D_MODEL // 4
ROW_BLOCK = 128
IN_COLS = SSM_WIDTH + 3 * ATTN_WIDTH + 2 * D_MODEL
LN_EPS = 1e-5
DEEPNORM_ALPHA = (2.0 * DEPTH) ** 0.25
DEEPNORM_BETA = (8.0 * DEPTH) ** -0.25

kernel_name = "hybrid_s5_diffattn_hmoe_deepnorm"


def layer_norm(x, g, b):
    xf = x.astype(jnp.float32)
    mu = jnp.mean(xf, axis=-1, keepdims=True)
    var = jnp.mean(jnp.square(xf - mu), axis=-1, keepdims=True)
    return ((xf - mu) * lax.rsqrt(var + LN_EPS) * g + b).astype(x.dtype)


def chunk_index(pos):
    return jnp.where(pos < N_META, 0, 1 + (pos - N_META) // CHUNK)


def _ssm_combine(earlier, later):
    a1r, a1i, b1r, b1i = earlier
    a2r, a2i, b2r, b2i = later
    ar = a2r * a1r - a2i * a1i
    ai = a2r * a1i + a2i * a1r
    br = a2r * b1r - a2i * b1i + b2r
    bi = a2r * b1i + a2i * b1r + b2i
    return ar, ai, br, bi


def s5_branch(u, a_re, a_im, log_dt, b_re, b_im, c_re, c_im, d_skip, w_glu):
    bsz, L, _ = u.shape
    f32 = jnp.float32
    uf = u.astype(f32).reshape(bsz, L, SSM_GROUPS, SSM_GROUP)
    lam_re = jnp.minimum(a_re.astype(f32), -1e-4)
    lam_im = a_im.astype(f32)
    dt = jnp.exp(log_dt.astype(f32))[:, None]
    mag = jnp.exp(lam_re * dt)
    ab_re = mag * jnp.cos(lam_im * dt)
    ab_im = mag * jnp.sin(lam_im * dt)
    den = lam_re * lam_re + lam_im * lam_im
    nr = ab_re - 1.0
    ni = ab_im
    z_re = (nr * lam_re + ni * lam_im) / den
    z_im = (ni * lam_re - nr * lam_im) / den
    br32 = b_re.astype(f32)
    bi32 = b_im.astype(f32)
    bb_re = z_re[..., None] * br32 - z_im[..., None] * bi32
    bb_im = z_re[..., None] * bi32 + z_im[..., None] * br32
    bu_re = jnp.einsum('blgc,gpc->blgp', uf, bb_re)
    bu_im = jnp.einsum('blgc,gpc->blgp', uf, bb_im)
    a_re_t = jnp.broadcast_to(ab_re, (1, L, SSM_GROUPS, SSM_STATE))
    a_im_t = jnp.broadcast_to(ab_im, (1, L, SSM_GROUPS, SSM_STATE))
    _, _, s_re, s_im = lax.associative_scan(_ssm_combine, (a_re_t, a_im_t, bu_re, bu_im), axis=1)
    y = (jnp.einsum('blgp,gcp->blgc', s_re, c_re.astype(f32))
         - jnp.einsum('blgp,gcp->blgc', s_im, c_im.astype(f32)))
    y = y + d_skip.astype(f32) * uf
    y = jax.nn.gelu(y.reshape(bsz, L, SSM_WIDTH))
    y = y * jax.nn.sigmoid(y @ w_glu.astype(f32))
    return y.astype(u.dtype)


def diff_attention(q, k, v, lam, lambda_init, subln_g):
    bsz, L = q.shape[0], q.shape[1]
    nqb = -(-L // Q_BLOCK)
    lq = nqb * Q_BLOCK
    qp = jnp.pad(q, ((0, 0), (0, lq - L), (0, 0), (0, 0), (0, 0)))
    qb = qp.reshape(bsz, nqb, Q_BLOCK, ATTN_HEADS, 2, ATTN_HEAD_DIM).transpose(1, 0, 2, 3, 4, 5)
    qpos = jnp.arange(lq).reshape(nqb, Q_BLOCK)
    kchunk = chunk_index(jnp.arange(L))
    scale = ATTN_HEAD_DIM ** -0.5

    def block(args):
        qblk, pos = args
        s = jnp.einsum('bqhmd,bkhmd->bhmqk', qblk, k).astype(jnp.float32) * scale
        allowed = chunk_index(pos)[:, None] >= kchunk[None, :]
        s = jnp.where(allowed, s, -1e30)
        p = jax.nn.softmax(s, axis=-1)
        a = p[:, :, 0] - lam * p[:, :, 1]
        return jnp.einsum('bhqk,bkhe->bqhe', a.astype(v.dtype), v)

    o = lax.map(block, (qb, qpos))
    o = o.transpose(1, 0, 2, 3, 4).reshape(bsz, lq, ATTN_HEADS, ATTN_VDIM)[:, :L]
    of = o.astype(jnp.float32)
    of = of * lax.rsqrt(jnp.mean(of * of, axis=-1, keepdims=True) + LN_EPS) * subln_g * (1.0 - lambda_init)
    return of.reshape(bsz, L, ATTN_WIDTH).astype(q.dtype)


def hier_moe(h, w_rg, b_rg, w_re, b_re, w_gate, w_up, w_down):
    bsz, L, d = h.shape
    x = h.reshape(-1, d)
    T = bsz * L
    g_prob = jax.nn.softmax((x @ w_rg + b_rg).astype(jnp.float32), axis=-1)
    g_val, g_idx = lax.top_k(g_prob, 1)
    e_logits = (x @ w_re + b_re).astype(jnp.float32).reshape(T, N_EXPERT_GROUPS, EXPERTS_PER_GROUP)
    e_sel = jnp.take_along_axis(e_logits, g_idx[:, :, None], axis=1)[:, 0]
    e_val, e_idx = lax.top_k(jax.nn.softmax(e_sel, axis=-1), TOP_K)
    e_val = e_val / jnp.sum(e_val, axis=-1, keepdims=True)
    weights = g_val * e_val
    expert = g_idx * EXPERTS_PER_GROUP + e_idx

    A = T * TOP_K
    flat_e = expert.reshape(-1).astype(jnp.int32)
    flat_tok = jnp.repeat(jnp.arange(T, dtype=jnp.int32), TOP_K)
    flat_w = weights.reshape(-1)
    order = jnp.argsort(flat_e)
    se = flat_e[order]
    stok = flat_tok[order]
    sw = flat_w[order]
    counts = jnp.bincount(flat_e, length=N_EXPERTS)
    starts = jnp.cumsum(counts) - counts
    padded = (counts + ROW_BLOCK - 1) // ROW_BLOCK * ROW_BLOCK
    pad_ends = jnp.cumsum(padded)
    pad_starts = pad_ends - padded
    dest = pad_starts[se] + (jnp.arange(A) - starts[se])
    n_blocks = -(-(A + N_EXPERTS * (ROW_BLOCK - 1)) // ROW_BLOCK)
    R = n_blocks * ROW_BLOCK
    row_tok = jnp.full((R,), T, jnp.int32).at[dest].set(stok)
    row_w = jnp.zeros((R,), jnp.float32).at[dest].set(sw)
    block_exp = jnp.minimum(jnp.searchsorted(pad_ends, jnp.arange(n_blocks) * ROW_BLOCK, side='right'),
                            N_EXPERTS - 1)
    x_pad = jnp.concatenate([x, jnp.zeros((1, d), x.dtype)], axis=0)

    def run_block(args):
        toks, e = args
        xb = x_pad[toks]
        hb = jax.nn.silu(xb @ w_gate[e]) * (xb @ w_up[e])
        return hb @ w_down[e]

    y_rows = lax.map(run_block, (row_tok.reshape(n_blocks, ROW_BLOCK), block_exp))
    y_rows = y_rows.reshape(R, d) * row_w[:, None].astype(y_rows.dtype)
    y = jax.ops.segment_sum(y_rows, row_tok, num_segments=T + 1)[:T]
    return y.reshape(bsz, L, d)


def setup_inputs(seed: int = 0) -> dict:
    key = jax.random.key(seed)
    ks = iter(jax.random.split(key, 40))
    nrm = lambda shape, std: jax.random.normal(next(ks), shape, jnp.float32) * std
    Dp = DEPTH
    n_idx = jnp.arange(SSM_STATE, dtype=jnp.float32)
    a_im0 = jnp.broadcast_to(math.pi * n_idx, (Dp, SSM_GROUPS, SSM_STATE))
    return {
        "x": nrm((BATCH, SEQ, D_MODEL), 1.0),
        "meta_tokens": nrm((N_META, D_MODEL), 1.0),
        "ln_in_g": 1.0 + nrm((D_MODEL,), 0.02),
        "ln_in_b": nrm((D_MODEL,), 0.02),
        "w_in": nrm((Dp, D_MODEL, IN_COLS), D_MODEL ** -0.5),
        "ssm_a_re": -0.5 + nrm((Dp, SSM_GROUPS, SSM_STATE), 0.01),
        "ssm_a_im": a_im0 + nrm((Dp, SSM_GROUPS, SSM_STATE), 0.01),
        "ssm_log_dt": jax.random.uniform(next(ks), (Dp, SSM_GROUPS), jnp.float32,
                                         math.log(DT_MIN), math.log(DT_MAX)),
        "ssm_b_re": nrm((Dp, SSM_GROUPS, SSM_STATE, SSM_GROUP), SSM_GROUP ** -0.5),
        "ssm_b_im": nrm((Dp, SSM_GROUPS, SSM_STATE, SSM_GROUP), SSM_GROUP ** -0.5),
        "ssm_c_re": nrm((Dp, SSM_GROUPS, SSM_GROUP, SSM_STATE), SSM_STATE ** -0.5),
        "ssm_c_im": nrm((Dp, SSM_GROUPS, SSM_GROUP, SSM_STATE), SSM_STATE ** -0.5),
        "ssm_d": nrm((Dp, SSM_GROUPS, SSM_GROUP), 1.0),
        "ssm_w_glu": nrm((Dp, SSM_WIDTH, SSM_WIDTH), SSM_WIDTH ** -0.5),
        "attn_lambda_q1": nrm((Dp, ATTN_HEAD_DIM), 0.1),
        "attn_lambda_k1": nrm((Dp, ATTN_HEAD_DIM), 0.1),
        "attn_lambda_q2": nrm((Dp, ATTN_HEAD_DIM), 0.1),
        "attn_lambda_k2": nrm((Dp, ATTN_HEAD_DIM), 0.1),
        "attn_subln_g": 1.0 + nrm((Dp, ATTN_VDIM), 0.02),
        "w_br_ssm": nrm((Dp, SSM_WIDTH, D_MODEL), SSM_WIDTH ** -0.5 * DEEPNORM_BETA),
        "w_br_attn": nrm((Dp, ATTN_WIDTH, D_MODEL), ATTN_WIDTH ** -0.5 * DEEPNORM_BETA),
        "w_o": nrm((Dp, D_MODEL, D_MODEL), D_MODEL ** -0.5 * DEEPNORM_BETA),
        "ln1_g": 1.0 + nrm((Dp, D_MODEL), 0.02),
        "ln1_b": nrm((Dp, D_MODEL), 0.02),
        "router_g_w": nrm((Dp, D_MODEL, N_EXPERT_GROUPS), D_MODEL ** -0.5),
        "router_g_b": nrm((Dp, N_EXPERT_GROUPS), 0.01),
        "router_e_w": nrm((Dp, D_MODEL, N_EXPERTS), D_MODEL ** -0.5),
        "router_e_b": nrm((Dp, N_EXPERTS), 0.01),
        "exp_w_gate": nrm((Dp, N_EXPERTS, D_MODEL, D_FF_EXPERT), D_MODEL ** -0.5),
        "exp_w_up": nrm((Dp, N_EXPERTS, D_MODEL, D_FF_EXPERT), D_MODEL ** -0.5),
        "exp_w_down": nrm((Dp, N_EXPERTS, D_FF_EXPERT, D_MODEL), D_FF_EXPERT ** -0.5 * DEEPNORM_BETA),
        "ln2_g": 1.0 + nrm((Dp, D_MODEL), 0.02),
        "ln2_b": nrm((Dp, D_MODEL), 0.02),
    }


def reference(x, meta_tokens, ln_in_g, ln_in_b, w_in, ssm_a_re, ssm_a_im, ssm_log_dt,
              ssm_b_re, ssm_b_im, ssm_c_re, ssm_c_im, ssm_d, ssm_w_glu,
              attn_lambda_q1, attn_lambda_k1, attn_lambda_q2, attn_lambda_k2, attn_subln_g,
              w_br_ssm, w_br_attn, w_o, ln1_g, ln1_b,
              router_g_w, router_g_b, router_e_w, router_e_b,
              exp_w_gate, exp_w_up, exp_w_down, ln2_g, ln2_b):
    bsz = x.shape[0]
    meta = jnp.broadcast_to(meta_tokens[None].astype(x.dtype), (bsz, N_META, D_MODEL))
    h = jnp.concatenate([meta, x], axis=1)
    L = h.shape[1]
    h = layer_norm(h, ln_in_g, ln_in_b)
    splits = np.cumsum([SSM_WIDTH, ATTN_WIDTH, ATTN_WIDTH, ATTN_WIDTH, D_MODEL]).tolist()
    for l in range(DEPTH):
        lambda_init = 0.8 - 0.6 * math.exp(-0.3 * l)
        proj = h @ w_in[l]
        u_ssm, q, k, v, g_ssm, g_attn = jnp.split(proj, splits, axis=-1)
        y_ssm = s5_branch(u_ssm, ssm_a_re[l], ssm_a_im[l], ssm_log_dt[l], ssm_b_re[l], ssm_b_im[l],
                          ssm_c_re[l], ssm_c_im[l], ssm_d[l], ssm_w_glu[l])
        q = q.reshape(bsz, L, ATTN_HEADS, 2, ATTN_HEAD_DIM)
        k = k.reshape(bsz, L, ATTN_HEADS, 2, ATTN_HEAD_DIM)
        v = v.reshape(bsz, L, ATTN_HEADS, ATTN_VDIM)
        lam = (jnp.exp(jnp.sum(attn_lambda_q1[l].astype(jnp.float32) * attn_lambda_k1[l].astype(jnp.float32)))
               - jnp.exp(jnp.sum(attn_lambda_q2[l].astype(jnp.float32) * attn_lambda_k2[l].astype(jnp.float32)))
               + lambda_init)
        y_attn = diff_attention(q, k, v, lam, lambda_init, attn_subln_g[l])
        merged = (jax.nn.sigmoid(g_ssm) * (y_ssm @ w_br_ssm[l])
                  + jax.nn.sigmoid(g_attn) * (y_attn @ w_br_attn[l]))
        mix = merged @ w_o[l]
        h = layer_norm(DEEPNORM_ALPHA * h + mix, ln1_g[l], ln1_b[l])
        ffn = hier_moe(h, router_g_w[l], router_g_b[l], router_e_w[l], router_e_b[l],
                       exp_w_gate[l], exp_w_up[l], exp_w_down[l])
        h = layer_norm(DEEPNORM_ALPHA * h + ffn, ln2_g[l], ln2_b[l])
    return h[:, N_META:]
```

```python
import functools
import math

import jax
import jax.numpy as jnp
from jax import lax
from jax.experimental import pallas as pl
from jax.experimental.pallas import tpu as pltpu

D_MODEL = 2048
N_META = 16
CHUNK = 64
ATTN_HEADS = 8
ATTN_HEAD_DIM = 64
ATTN_VDIM = 2 * ATTN_HEAD_DIM
ATTN_WIDTH = ATTN_HEADS * ATTN_VDIM
SSM_WIDTH = D_MODEL // 4
SSM_GROUP = 16
SSM_GROUPS = SSM_WIDTH // SSM_GROUP
SSM_STATE = 64
SSM_COLS = SSM_GROUPS * SSM_STATE
N_EXPERT_GROUPS = 4
EXPERTS_PER_GROUP = 8
N_EXPERTS = N_EXPERT_GROUPS * EXPERTS_PER_GROUP
TOP_K = 2
D_FF_EXPERT = D_MODEL // 4
QKV_COLS = 3 * ATTN_WIDTH
GATE_COLS = 2 * D_MODEL
IN_COLS = SSM_WIDTH + QKV_COLS + GATE_COLS
LN_EPS = 1e-5
DEPTH = 1
DEEPNORM_ALPHA = (2.0 * DEPTH) ** 0.25
LANES = 128
NEG_INF = -1e30

PROJ_TN = 512
PROJ_TM = 1024
SSM_TC = 32
SSM_CB = 512
ATTN_TQ = 256
MERGE_TM = 256
ROW_BLOCK = 256
DISPATCH_TM = 256
COMBINE_TM = 256
VMEM_LIMIT = 56 * 1024 * 1024


def _layer_norm(x, g, b):
    mu = jnp.mean(x, axis=-1, keepdims=True)
    xc = x - mu
    var = jnp.mean(xc * xc, axis=-1, keepdims=True)
    return xc * lax.rsqrt(var + LN_EPS) * g + b


def _ln_proj_kernel(x_ref, g_ref, b_ref, w_ref, u_ref, qkv_ref, gate_ref, xn_ref):
    j = pl.program_id(1)

    @pl.when(j == 0)
    def _():
        xn_ref[...] = _layer_norm(x_ref[...], g_ref[...], b_ref[...]).astype(jnp.bfloat16)

    acc = jnp.dot(xn_ref[...], w_ref[...], preferred_element_type=jnp.float32)
    n_qkv = QKV_COLS // PROJ_TN

    @pl.when(j == 0)
    def _():
        u_ref[...] = acc

    @pl.when(jnp.logical_and(j >= 1, j <= n_qkv))
    def _():
        qkv_ref[...] = acc.astype(jnp.bfloat16)

    @pl.when(j > n_qkv)
    def _():
        gate_ref[...] = acc.astype(jnp.bfloat16)


def _ln_proj(x2, g, b, w_bf16, tm):
    t_rows = x2.shape[0]
    n_qkv = QKV_COLS // PROJ_TN
    n_gate = GATE_COLS // PROJ_TN
    n_col = IN_COLS // PROJ_TN
    assert SSM_WIDTH == PROJ_TN and t_rows % tm == 0
    return pl.pallas_call(
        _ln_proj_kernel,
        grid=(t_rows // tm, n_col),
        in_specs=[
            pl.BlockSpec((tm, D_MODEL), lambda i, j: (i, 0)),
            pl.BlockSpec((1, D_MODEL), lambda i, j: (0, 0)),
            pl.BlockSpec((1, D_MODEL), lambda i, j: (0, 0)),
            pl.BlockSpec((D_MODEL, PROJ_TN), lambda i, j: (0, j)),
        ],
        out_specs=[
            pl.BlockSpec((tm, PROJ_TN), lambda i, j: (i, 0)),
            pl.BlockSpec((tm, PROJ_TN), lambda i, j: (i, jnp.clip(j - 1, 0, n_qkv - 1))),
            pl.BlockSpec((tm, PROJ_TN), lambda i, j: (i, jnp.clip(j - 1 - n_qkv, 0, n_gate - 1))),
        ],
        out_shape=[
            jax.ShapeDtypeStruct((t_rows, SSM_WIDTH), jnp.float32),
            jax.ShapeDtypeStruct((t_rows, QKV_COLS), jnp.bfloat16),
            jax.ShapeDtypeStruct((t_rows, GATE_COLS), jnp.bfloat16),
        ],
        scratch_shapes=[pltpu.VMEM((tm, D_MODEL), jnp.bfloat16)],
        compiler_params=pltpu.CompilerParams(
            dimension_semantics=("parallel", "arbitrary"), vmem_limit_bytes=VMEM_LIMIT),
        name="ln_proj",
    )(x2, g, b, w_bf16)


def _ssm_kernel(um_ref, u_ref, bb_ref, ab_ref, cm_ref, d_ref, wglu_ref, y_ref, state_ref, bu_ref, *, nb):
    def expand_and_scan(u_bf16, n_steps):
        rows = n_steps * nb
        bu_ref[pl.ds(0, rows), :] = jnp.dot(u_bf16, bb_ref[...], preferred_element_type=jnp.float32)
        for cb in range(SSM_COLS // SSM_CB):
            re_cols = pl.ds(cb * SSM_CB, SSM_CB)
            im_cols = pl.ds(SSM_COLS + cb * SSM_CB, SSM_CB)
            a_re = ab_ref[0:1, cb * SSM_CB:(cb + 1) * SSM_CB]
            a_im = ab_ref[1:2, cb * SSM_CB:(cb + 1) * SSM_CB]

            def step(t, carry):
                s_re, s_im = carry
                r0 = pl.multiple_of(t * nb, nb)
                n_re = a_re * s_re - a_im * s_im + bu_ref[pl.ds(r0, nb), re_cols]
                n_im = a_re * s_im + a_im * s_re + bu_ref[pl.ds(r0, nb), im_cols]
                bu_ref[pl.ds(r0, nb), re_cols] = n_re
                bu_ref[pl.ds(r0, nb), im_cols] = n_im
                return n_re, n_im

            s_re, s_im = lax.fori_loop(0, n_steps, step, (state_ref[:, re_cols], state_ref[:, im_cols]))
            state_ref[:, re_cols] = s_re
            state_ref[:, im_cols] = s_im

    @pl.when(pl.program_id(0) == 0)
    def _():
        state_ref[...] = jnp.zeros_like(state_ref)
        expand_and_scan(um_ref[...].astype(jnp.bfloat16), N_META)

    u = u_ref[...]
    expand_and_scan(u.astype(jnp.bfloat16), SSM_TC)
    y = jnp.dot(bu_ref[...].astype(jnp.bfloat16), cm_ref[...], preferred_element_type=jnp.float32)
    y = jax.nn.gelu(y + d_ref[...] * u)
    gate = jnp.dot(y.astype(jnp.bfloat16), wglu_ref[...], preferred_element_type=jnp.float32)
    y_ref[...] = (y * jax.nn.sigmoid(gate)).astype(y_ref.dtype)


def _ssm(u_meta_rows, u_tm, bb, ab, cm, d, wglu, nb):
    rows = u_tm.shape[0]
    blk = SSM_TC * nb
    assert rows % blk == 0 and nb % 8 == 0 and N_META <= SSM_TC
    const = lambda i: (0, 0)
    return pl.pallas_call(
        functools.partial(_ssm_kernel, nb=nb),
        grid=(rows // blk,),
        in_specs=[
            pl.BlockSpec((N_META * nb, SSM_WIDTH), const),
            pl.BlockSpec((blk, SSM_WIDTH), lambda i: (i, 0)),
            pl.BlockSpec((SSM_WIDTH, 2 * SSM_COLS), const),
            pl.BlockSpec((2, SSM_COLS), const),
            pl.BlockSpec((2 * SSM_COLS, SSM_WIDTH), const),
            pl.BlockSpec((1, SSM_WIDTH), const),
            pl.BlockSpec((SSM_WIDTH, SSM_WIDTH), const),
        ],
        out_specs=pl.BlockSpec((blk, SSM_WIDTH), lambda i: (i, 0)),
        out_shape=jax.ShapeDtypeStruct((rows, SSM_WIDTH), jnp.bfloat16),
        scratch_shapes=[
            pltpu.VMEM((nb, 2 * SSM_COLS), jnp.float32),
            pltpu.VMEM((blk, 2 * SSM_COLS), jnp.float32),
        ],
        compiler_params=pltpu.CompilerParams(
            dimension_semantics=("arbitrary",), vmem_limit_bytes=VMEM_LIMIT),
        name="ssm",
    )(u_meta_rows, u_tm, bb, ab, cm, d, wglu)


def _ssm_params(a_re, a_im, log_dt, b_re, b_im, c_re, c_im):
    f32 = jnp.float32
    lam_re = jnp.minimum(a_re.astype(f32), -1e-4)
    lam_im = a_im.astype(f32)
    dt = jnp.exp(log_dt.astype(f32))[:, None]
    mag = jnp.exp(lam_re * dt)
    ab_re = mag * jnp.cos(lam_im * dt)
    ab_im = mag * jnp.sin(lam_im * dt)
    den = lam_re * lam_re + lam_im * lam_im
    nr = ab_re - 1.0
    ni = ab_im
    z_re = (nr * lam_re + ni * lam_im) / den
    z_im = (ni * lam_re - nr * lam_im) / den
    br32 = b_re.astype(f32)
    bi32 = b_im.astype(f32)
    bb_re = z_re[..., None] * br32 - z_im[..., None] * bi32
    bb_im = z_re[..., None] * bi32 + z_im[..., None] * br32
    eye = jnp.eye(SSM_GROUPS, dtype=f32)
    exp_re = jnp.einsum('gpc,gh->gchp', bb_re, eye).reshape(SSM_WIDTH, SSM_COLS)
    exp_im = jnp.einsum('gpc,gh->gchp', bb_im, eye).reshape(SSM_WIDTH, SSM_COLS)
    bb = jnp.concatenate([exp_re, exp_im], axis=1)
    ro_re = jnp.einsum('gcp,gh->gphc', c_re.astype(f32), eye).reshape(SSM_COLS, SSM_WIDTH)
    ro_im = jnp.einsum('gcp,gh->gphc', c_im.astype(f32), eye).reshape(SSM_COLS, SSM_WIDTH)
    cm = jnp.concatenate([ro_re, -ro_im], axis=0)
    ab = jnp.stack([ab_re.reshape(SSM_COLS), ab_im.reshape(SSM_COLS)], axis=0)
    return bb.astype(jnp.bfloat16), ab, cm.astype(jnp.bfloat16)


def _attn_kernel(lam_ref, q_ref, k_ref, v_ref, km_ref, vm_ref, g_ref, o_ref, m_ref, l_ref, acc_ref, *, scale):
    i = pl.program_id(2)
    tq = ATTN_TQ
    q = q_ref[...] * scale
    lane = lax.broadcasted_iota(jnp.int32, q.shape, 1)
    zero = jnp.zeros_like(q)
    qq = jnp.concatenate([jnp.where(lane < ATTN_HEAD_DIM, q, zero),
                          jnp.where(lane >= ATTN_HEAD_DIM, q, zero)], axis=0)

    def update(s, vb):
        m_old = m_ref[...]
        m_new = jnp.maximum(m_old, jnp.max(s, axis=-1, keepdims=True))
        alpha = jnp.exp(m_old - m_new)
        p = jnp.exp(s - m_new)
        l_ref[...] = alpha * l_ref[...] + jnp.sum(p, axis=-1, keepdims=True)
        acc_ref[...] = alpha * acc_ref[...] + jnp.dot(p.astype(jnp.bfloat16), vb,
                                                      preferred_element_type=jnp.float32)
        m_ref[...] = m_new

    def scores(kb):
        return lax.dot_general(qq, kb, (((1,), (1,)), ((), ())), preferred_element_type=jnp.float32)

    m_ref[...] = jnp.full_like(m_ref, NEG_INF)
    l_ref[...] = jnp.zeros_like(l_ref)
    acc_ref[...] = jnp.zeros_like(acc_ref)
    update(scores(km_ref[...]), vm_ref[...])

    def body(j, carry):
        r0 = pl.multiple_of(j * tq, tq)
        update(scores(k_ref[pl.ds(r0, tq), :]), v_ref[pl.ds(r0, tq), :])
        return carry

    lax.fori_loop(0, i, body, 0)

    r0 = pl.multiple_of(i * tq, tq)
    s = scores(k_ref[pl.ds(r0, tq), :])
    row = lax.broadcasted_iota(jnp.int32, s.shape, 0) % tq
    col = lax.broadcasted_iota(jnp.int32, s.shape, 1)
    s = jnp.where(row // CHUNK >= col // CHUNK, s, NEG_INF)
    update(s, v_ref[pl.ds(r0, tq), :])

    o_all = acc_ref[...] / l_ref[...]
    o = o_all[:tq] - lam_ref[0] * o_all[tq:]
    o = o * lax.rsqrt(jnp.mean(o * o, axis=-1, keepdims=True) + LN_EPS) * g_ref[...]
    o_ref[...] = o.astype(o_ref.dtype)


def _attention(lam, qkv, qkv_meta, g_scaled, nb, seq):
    tq = ATTN_TQ
    assert seq % tq == 0
    nh = ATTN_HEADS
    return pl.pallas_call(
        functools.partial(_attn_kernel, scale=ATTN_HEAD_DIM ** -0.5),
        grid=(nb, nh, seq // tq),
        in_specs=[
            pl.BlockSpec(memory_space=pltpu.SMEM),
            pl.BlockSpec((None, tq, ATTN_VDIM), lambda b, h, i: (b, i, h)),
            pl.BlockSpec((None, seq, ATTN_VDIM), lambda b, h, i: (b, 0, nh + h)),
            pl.BlockSpec((None, seq, ATTN_VDIM), lambda b, h, i: (b, 0, 2 * nh + h)),
            pl.BlockSpec((N_META, ATTN_VDIM), lambda b, h, i: (0, nh + h)),
            pl.BlockSpec((N_META, ATTN_VDIM), lambda b, h, i: (0, 2 * nh + h)),
            pl.BlockSpec((1, ATTN_VDIM), lambda b, h, i: (0, 0)),
        ],
        out_specs=pl.BlockSpec((None, tq, ATTN_VDIM), lambda b, h, i: (b, i, h)),
        out_shape=jax.ShapeDtypeStruct((nb, seq, ATTN_WIDTH), jnp.bfloat16),
        scratch_shapes=[
            pltpu.VMEM((2 * tq, 1), jnp.float32),
            pltpu.VMEM((2 * tq, 1), jnp.float32),
            pltpu.VMEM((2 * tq, ATTN_VDIM), jnp.float32),
        ],
        compiler_params=pltpu.CompilerParams(
            dimension_semantics=("parallel", "parallel", "arbitrary"), vmem_limit_bytes=VMEM_LIMIT),
        name="attention",
    )(lam, qkv, qkv, qkv, qkv_meta, qkv_meta, g_scaled)


def _merge_kernel(x_ref, gi_ref, bi_ref, gate_ref, ys_ref, ya_ref, wbs_ref, wba_ref, wo_ref,
                  g1_ref, b1_ref, wrh_ref, wrl_ref, br_ref, h1_ref, route_ref, cnt_ref):
    tm = x_ref.shape[0]

    @pl.when(pl.program_id(0) == 0)
    def _():
        cnt_ref[...] = jnp.zeros_like(cnt_ref)

    h = _layer_norm(x_ref[...], gi_ref[...], bi_ref[...])
    gates = gate_ref[...].astype(jnp.float32)
    ps = jnp.dot(ys_ref[...], wbs_ref[...], preferred_element_type=jnp.float32)
    pa = jnp.dot(ya_ref[...], wba_ref[...], preferred_element_type=jnp.float32)
    merged = jax.nn.sigmoid(gates[:, :D_MODEL]) * ps + jax.nn.sigmoid(gates[:, D_MODEL:]) * pa
    mix = jnp.dot(merged.astype(jnp.bfloat16), wo_ref[...], preferred_element_type=jnp.float32)
    h1 = _layer_norm(DEEPNORM_ALPHA * h + mix, g1_ref[...], b1_ref[...])
    h1_ref[...] = h1

    hi = h1.astype(jnp.bfloat16)
    lo = (h1 - hi.astype(jnp.float32)).astype(jnp.bfloat16)
    logits = (jnp.dot(hi, wrh_ref[...], preferred_element_type=jnp.float32)
              + jnp.dot(lo, wrh_ref[...], preferred_element_type=jnp.float32)
              + jnp.dot(hi, wrl_ref[...], preferred_element_type=jnp.float32)) + br_ref[...]
    lane_i = lax.broadcasted_iota(jnp.int32, logits.shape, 1)
    lane = lane_i.astype(jnp.float32)

    def first_argmax(vals):
        mx = jnp.max(vals, axis=-1, keepdims=True)
        idx = jnp.min(jnp.where(vals == mx, lane, float(LANES)), axis=-1, keepdims=True)
        return mx, idx

    is_group = jnp.logical_and(lane_i >= N_EXPERTS, lane_i < N_EXPERTS + N_EXPERT_GROUPS)
    gl = jnp.where(is_group, logits, NEG_INF)
    gmax, glane = first_argmax(gl)
    g_val = 1.0 / jnp.sum(jnp.where(is_group, jnp.exp(gl - gmax), 0.0), axis=-1, keepdims=True)
    g_idx = glane - float(N_EXPERTS)
    lane_group = (lane_i // EXPERTS_PER_GROUP).astype(jnp.float32)
    in_group = jnp.logical_and(lane_i < N_EXPERTS, lane_group == g_idx)
    el = jnp.where(in_group, logits, NEG_INF)
    m1, i1 = first_argmax(el)
    el2 = jnp.where(lane == i1, NEG_INF, el)
    m2, i2 = first_argmax(el2)
    e2 = jnp.exp(m2 - m1)
    w1 = g_val / (1.0 + e2)
    w2 = g_val * e2 / (1.0 + e2)

    onehot = jnp.logical_or(lane == i1, lane == i2)
    oh = jnp.where(onehot, 1.0, 0.0)
    r_i = lax.broadcasted_iota(jnp.int32, (tm, tm), 0)
    c_i = lax.broadcasted_iota(jnp.int32, (tm, tm), 1)
    tri = jnp.where(c_i < r_i, 1.0, 0.0).astype(jnp.bfloat16)
    before = jnp.dot(tri, oh.astype(jnp.bfloat16), preferred_element_type=jnp.float32) + cnt_ref[...]
    rank1 = jnp.sum(jnp.where(lane == i1, before, 0.0), axis=-1, keepdims=True)
    rank2 = jnp.sum(jnp.where(lane == i2, before, 0.0), axis=-1, keepdims=True)
    cnt_ref[...] = cnt_ref[...] + jnp.sum(oh, axis=0, keepdims=True)

    route = jnp.where(lane_i == 0, i1, 0.0)
    route = jnp.where(lane_i == 1, i2, route)
    route = jnp.where(lane_i == 2, w1, route)
    route = jnp.where(lane_i == 3, w2, route)
    route = jnp.where(lane_i == 4, rank1, route)
    route = jnp.where(lane_i == 5, rank2, route)
    route_ref[...] = route


def _merge(x2, gi, bi, gates, ys, ya, wbs, wba, wo, g1, b1, wrh, wrl, br):
    t_rows = x2.shape[0]
    tm = MERGE_TM
    assert t_rows % tm == 0
    row = lambda i: (i, 0)
    const = lambda i: (0, 0)
    return pl.pallas_call(
        _merge_kernel,
        grid=(t_rows // tm,),
        in_specs=[
            pl.BlockSpec((tm, D_MODEL), row),
            pl.BlockSpec((1, D_MODEL), const),
            pl.BlockSpec((1, D_MODEL), const),
            pl.BlockSpec((tm, GATE_COLS), row),
            pl.BlockSpec((tm, SSM_WIDTH), row),
            pl.BlockSpec((tm, ATTN_WIDTH), row),
            pl.BlockSpec((SSM_WIDTH, D_MODEL), const),
            pl.BlockSpec((ATTN_WIDTH, D_MODEL), const),
            pl.BlockSpec((D_MODEL, D_MODEL), const),
            pl.BlockSpec((1, D_MODEL), const),
            pl.BlockSpec((1, D_MODEL), const),
            pl.BlockSpec((D_MODEL, LANES), const),
            pl.BlockSpec((D_MODEL, LANES), const),
            pl.BlockSpec((1, LANES), const),
        ],
        out_specs=[
            pl.BlockSpec((tm, D_MODEL), row),
            pl.BlockSpec((tm, LANES), row),
            pl.BlockSpec((1, LANES), const),
        ],
        out_shape=[
            jax.ShapeDtypeStruct((t_rows, D_MODEL), jnp.float32),
            jax.ShapeDtypeStruct((t_rows, LANES), jnp.float32),
            jax.ShapeDtypeStruct((1, LANES), jnp.float32),
        ],
        compiler_params=pltpu.CompilerParams(
            dimension_semantics=("arbitrary",), vmem_limit_bytes=VMEM_LIMIT),
        name="merge",
    )(x2, gi, bi, gates, ys, ya, wbs, wba, wo, g1, b1, wrh, wrl, br)


def _dispatch_kernel(dest_ref, h1_hbm, xs_in_hbm, xs_hbm, sem):
    del xs_in_hbm
    tm = DISPATCH_TM
    base = pl.program_id(0) * tm

    def body(t, carry):
        for k in range(TOP_K):
            d = dest_ref[0, TOP_K * t + k]
            pltpu.make_async_copy(h1_hbm.at[pl.ds(base + t, 1)], xs_hbm.at[pl.ds(d, 1)], sem).start()
        return carry

    lax.fori_loop(0, tm, body, 0)
    for _ in range(TOP_K):
        pltpu.make_async_copy(h1_hbm.at[pl.ds(0, tm)], xs_hbm.at[pl.ds(0, tm)], sem).wait()


def _dispatch(dest3, h1, xs_zero):
    t_rows = h1.shape[0]
    tm = DISPATCH_TM
    assert t_rows % tm == 0
    return pl.pallas_call(
        _dispatch_kernel,
        grid=(t_rows // tm,),
        in_specs=[
            pl.BlockSpec((None, 1, TOP_K * tm), lambda i: (i, 0, 0), memory_space=pltpu.SMEM),
            pl.BlockSpec(memory_space=pl.ANY),
            pl.BlockSpec(memory_space=pl.ANY),
        ],
        out_specs=pl.BlockSpec(memory_space=pl.ANY),
        out_shape=jax.ShapeDtypeStruct(xs_zero.shape, xs_zero.dtype),
        scratch_shapes=[pltpu.SemaphoreType.DMA(())],
        input_output_aliases={2: 0},
        compiler_params=pltpu.CompilerParams(dimension_semantics=("arbitrary",)),
        name="dispatch",
    )(dest3, h1, xs_zero)


def _experts_kernel(be_ref, nu_ref, xs_ref, wg_ref, wu_ref, wd_ref, y_ref):
    del be_ref

    @pl.when(pl.program_id(0) < nu_ref[0])
    def _():
        xb = xs_ref[...].astype(jnp.bfloat16)
        g = jnp.dot(xb, wg_ref[...], preferred_element_type=jnp.float32)
        u = jnp.dot(xb, wu_ref[...], preferred_element_type=jnp.float32)
        hb = (jax.nn.silu(g) * u).astype(jnp.bfloat16)
        y_ref[...] = jnp.dot(hb, wd_ref[...], preferred_element_type=jnp.float32)

    @pl.when(pl.program_id(0) >= nu_ref[0])
    def _():
        y_ref[...] = jnp.zeros_like(y_ref)


def _experts(block_exp, n_used, xs, wg, wu, wd):
    r_rows = xs.shape[0]
    nblk = r_rows // ROW_BLOCK
    rows = lambda i, be, nu: (jnp.minimum(i, nu[0] - 1), 0)
    out_rows = lambda i, be, nu: (i, 0)
    wsel = lambda i, be, nu: (be[i], 0, 0)
    grid_spec = pltpu.PrefetchScalarGridSpec(
        num_scalar_prefetch=2,
        grid=(nblk,),
        in_specs=[
            pl.BlockSpec((ROW_BLOCK, D_MODEL), rows),
            pl.BlockSpec((None, D_MODEL, D_FF_EXPERT), wsel),
            pl.BlockSpec((None, D_MODEL, D_FF_EXPERT), wsel),
            pl.BlockSpec((None, D_FF_EXPERT, D_MODEL), wsel),
        ],
        out_specs=pl.BlockSpec((ROW_BLOCK, D_MODEL), out_rows),
    )
    return pl.pallas_call(
        _experts_kernel,
        grid_spec=grid_spec,
        out_shape=jax.ShapeDtypeStruct((r_rows, D_MODEL), jnp.float32),
        compiler_params=pltpu.CompilerParams(
            dimension_semantics=("arbitrary",), vmem_limit_bytes=VMEM_LIMIT),
        name="experts",
    )(block_exp, n_used, xs, wg, wu, wd)


def _combine_kernel(dest_ref, h1_ref, route_ref, g_ref, b_ref, y_hbm, o_ref, buf_ref, sem):
    tm = COMBINE_TM

    def body(t, carry):
        for k in range(TOP_K):
            d = dest_ref[0, TOP_K * t + k]
            pltpu.make_async_copy(y_hbm.at[pl.ds(d, 1)], buf_ref.at[k, pl.ds(t, 1)], sem).start()
        return carry

    lax.fori_loop(0, tm, body, 0)
    for k in range(TOP_K):
        pltpu.make_async_copy(y_hbm.at[pl.ds(0, tm)], buf_ref.at[k], sem).wait()

    route = route_ref[...]
    ffn = route[:, 2:3] * buf_ref[0] + route[:, 3:4] * buf_ref[1]
    o_ref[...] = _layer_norm(DEEPNORM_ALPHA * h1_ref[...] + ffn, g_ref[...], b_ref[...])


def _combine(dest3, h1, route, g2, b2, y_rows):
    t_rows = h1.shape[0]
    tm = COMBINE_TM
    assert t_rows % tm == 0
    row = lambda i: (i, 0)
    const = lambda i: (0, 0)
    return pl.pallas_call(
        _combine_kernel,
        grid=(t_rows // tm,),
        in_specs=[
            pl.BlockSpec((None, 1, TOP_K * tm), lambda i: (i, 0, 0), memory_space=pltpu.SMEM),
            pl.BlockSpec((tm, D_MODEL), row),
            pl.BlockSpec((tm, LANES), row),
            pl.BlockSpec((1, D_MODEL), const),
            pl.BlockSpec((1, D_MODEL), const),
            pl.BlockSpec(memory_space=pl.ANY),
        ],
        out_specs=pl.BlockSpec((tm, D_MODEL), row),
        out_shape=jax.ShapeDtypeStruct((t_rows, D_MODEL), jnp.float32),
        scratch_shapes=[
            pltpu.VMEM((TOP_K, tm, D_MODEL), jnp.float32),
            pltpu.SemaphoreType.DMA(()),
        ],
        compiler_params=pltpu.CompilerParams(
            dimension_semantics=("arbitrary",), vmem_limit_bytes=VMEM_LIMIT),
        name="combine",
    )(dest3, h1, route, g2, b2, y_rows)


def kernel(x, meta_tokens, ln_in_g, ln_in_b, w_in, ssm_a_re, ssm_a_im, ssm_log_dt, ssm_b_re, ssm_b_im,
           ssm_c_re, ssm_c_im, ssm_d, ssm_w_glu, attn_lambda_q1, attn_lambda_k1, attn_lambda_q2,
           attn_lambda_k2, attn_subln_g, w_br_ssm, w_br_attn, w_o, ln1_g, ln1_b, router_g_w, router_g_b,
           router_e_w, router_e_b, exp_w_gate, exp_w_up, exp_w_down, ln2_g, ln2_b):
    f32, bf16 = jnp.float32, jnp.bfloat16
    nb, seq, d = x.shape
    assert d == D_MODEL and w_in.shape[0] == DEPTH == 1
    t_rows = nb * seq
    l = 0
    lambda_init = 0.8 - 0.6 * math.exp(-0.3 * l)
    row2 = lambda v: v.reshape(1, -1).astype(f32)

    x2 = x.reshape(t_rows, d)
    w_in_b = w_in[l].astype(bf16)
    gi, bi = row2(ln_in_g), row2(ln_in_b)
    u, qkv, gates = _ln_proj(x2, gi, bi, w_in_b, min(PROJ_TM, seq))
    u_meta, qkv_meta, _ = _ln_proj(meta_tokens.astype(f32), gi, bi, w_in_b, N_META)

    bb, ab, cm = _ssm_params(ssm_a_re[l], ssm_a_im[l], ssm_log_dt[l], ssm_b_re[l], ssm_b_im[l],
                             ssm_c_re[l], ssm_c_im[l])
    u_tm = u.reshape(nb, seq, SSM_WIDTH).transpose(1, 0, 2).reshape(seq * nb, SSM_WIDTH)
    u_meta_rows = jnp.repeat(u_meta, nb, axis=0)
    y_ssm_tm = _ssm(u_meta_rows, u_tm, bb, ab, cm, row2(ssm_d[l]), ssm_w_glu[l].astype(bf16), nb)
    y_ssm = y_ssm_tm.reshape(seq, nb, SSM_WIDTH).transpose(1, 0, 2).reshape(t_rows, SSM_WIDTH)

    lam = (jnp.exp(jnp.sum(attn_lambda_q1[l].astype(f32) * attn_lambda_k1[l].astype(f32)))
           - jnp.exp(jnp.sum(attn_lambda_q2[l].astype(f32) * attn_lambda_k2[l].astype(f32)))
           + lambda_init).reshape(1)
    g_scaled = row2(attn_subln_g[l]) * (1.0 - lambda_init)
    y_attn = _attention(lam, qkv.reshape(nb, seq, QKV_COLS), qkv_meta, g_scaled, nb, seq)
    y_attn = y_attn.reshape(t_rows, ATTN_WIDTH)

    w_r = jnp.concatenate([router_e_w[l].astype(f32), router_g_w[l].astype(f32)], axis=1)
    w_r = jnp.pad(w_r, ((0, 0), (0, LANES - w_r.shape[1])))
    w_r_hi = w_r.astype(bf16)
    w_r_lo = (w_r - w_r_hi.astype(f32)).astype(bf16)
    b_r = jnp.concatenate([router_e_b[l].astype(f32), router_g_b[l].astype(f32)])
    b_r = jnp.pad(b_r, (0, LANES - b_r.shape[0])).reshape(1, LANES)
    h1, route, cnt = _merge(x2, gi, bi, gates, y_ssm, y_attn, w_br_ssm[l].astype(bf16),
                            w_br_attn[l].astype(bf16), w_o[l].astype(bf16), row2(ln1_g[l]), row2(ln1_b[l]),
                            w_r_hi, w_r_lo, b_r)

    counts = cnt[0, :N_EXPERTS].astype(jnp.int32)
    padded = (counts + ROW_BLOCK - 1) // ROW_BLOCK * ROW_BLOCK
    pad_ends = jnp.cumsum(padded)
    pad_starts = pad_ends - padded
    expert = route[:, 0:TOP_K].astype(jnp.int32)
    rank = route[:, 4:4 + TOP_K].astype(jnp.int32)
    dest = pad_starts[expert] + rank
    n_blocks = -(-(t_rows * TOP_K + N_EXPERTS * (ROW_BLOCK - 1)) // ROW_BLOCK)
    block_exp = jnp.minimum(jnp.searchsorted(pad_ends, jnp.arange(n_blocks) * ROW_BLOCK, side='right'),
                            N_EXPERTS - 1).astype(jnp.int32)
    n_used = (pad_ends[-1:] // ROW_BLOCK).astype(jnp.int32)

    xs = _dispatch(dest.reshape(t_rows // DISPATCH_TM, 1, TOP_K * DISPATCH_TM), h1,
                   jnp.zeros((n_blocks * ROW_BLOCK, d), f32))
    y_rows = _experts(block_exp, n_used, xs, exp_w_gate[l].astype(bf16), exp_w_up[l].astype(bf16),
                      exp_w_down[l].astype(bf16))
    out = _combine(dest.reshape(t_rows // COMBINE_TM, 1, TOP_K * COMBINE_TM), h1, route,
                   row2(ln2_g[l]), row2(ln2_b[l]), y_rows)
    return out.reshape(nb, seq, d).astype(x.dtype)
```

```python
import functools
import math

import jax
import jax.numpy as jnp
from jax import lax
from jax.experimental import pallas as pl
from jax.experimental.pallas import tpu as pltpu

D_MODEL = 2048
N_META = 16
CHUNK = 64
ATTN_HEADS = 8
ATTN_HEAD_DIM = 64
ATTN_VDIM = 2 * ATTN_HEAD_DIM
ATTN_WIDTH = ATTN_HEADS * ATTN_VDIM
SSM_WIDTH = D_MODEL // 4
SSM_GROUP = 16
SSM_GROUPS = SSM_WIDTH // SSM_GROUP
SSM_STATE = 64
SSM_COLS = SSM_GROUPS * SSM_STATE
N_EXPERT_GROUPS = 4
EXPERTS_PER_GROUP = 8
N_EXPERTS = N_EXPERT_GROUPS * EXPERTS_PER_GROUP
TOP_K = 2
D_FF_EXPERT = D_MODEL // 4
QKV_COLS = 3 * ATTN_WIDTH
GATE_COLS = 2 * D_MODEL
IN_COLS = SSM_WIDTH + QKV_COLS + GATE_COLS
LN_EPS = 1e-5
DEPTH = 1
DEEPNORM_ALPHA = (2.0 * DEPTH) ** 0.25
LANES = 128
NEG_INF = -1e30

PROJ_TN = 1024
PROJ_TM = 1024
SSM_TC = 32
SSM_CB = 512
ATTN_TQ = 256
MERGE_TM = 256
ROW_BLOCK = 256
DISPATCH_TM = 256
COMBINE_TM = 256
VMEM_LIMIT = 56 * 1024 * 1024


def _layer_norm(x, g, b):
    mu = jnp.mean(x, axis=-1, keepdims=True)
    xc = x - mu
    var = jnp.mean(xc * xc, axis=-1, keepdims=True)
    return xc * lax.rsqrt(var + LN_EPS) * g + b


def _ln_proj_kernel(x_ref, g_ref, b_ref, wu_ref, w_ref, u_ref, qkv_ref, gate_ref, xn_ref):
    j = pl.program_id(1)
    n_qkv = QKV_COLS // PROJ_TN

    @pl.when(j == 0)
    def _():
        xn = _layer_norm(x_ref[...], g_ref[...], b_ref[...]).astype(jnp.bfloat16)
        xn_ref[...] = xn
        u_ref[...] = jnp.dot(xn, wu_ref[...], preferred_element_type=jnp.float32)

    acc = jnp.dot(xn_ref[...], w_ref[...], preferred_element_type=jnp.float32)

    @pl.when(j < n_qkv)
    def _():
        qkv_ref[...] = acc.astype(jnp.bfloat16)

    @pl.when(j >= n_qkv)
    def _():
        gate_ref[...] = acc.astype(jnp.bfloat16)


def _ln_proj(x2, g, b, w_u, w_rest, tm):
    t_rows = x2.shape[0]
    n_qkv = QKV_COLS // PROJ_TN
    n_col = (QKV_COLS + GATE_COLS) // PROJ_TN
    assert QKV_COLS % PROJ_TN == 0 and GATE_COLS % PROJ_TN == 0 and t_rows % tm == 0
    return pl.pallas_call(
        _ln_proj_kernel,
        grid=(t_rows // tm, n_col),
        in_specs=[
            pl.BlockSpec((tm, D_MODEL), lambda i, j: (i, 0)),
            pl.BlockSpec((1, D_MODEL), lambda i, j: (0, 0)),
            pl.BlockSpec((1, D_MODEL), lambda i, j: (0, 0)),
            pl.BlockSpec((D_MODEL, SSM_WIDTH), lambda i, j: (0, 0)),
            pl.BlockSpec((D_MODEL, PROJ_TN), lambda i, j: (0, j)),
        ],
        out_specs=[
            pl.BlockSpec((tm, SSM_WIDTH), lambda i, j: (i, 0)),
            pl.BlockSpec((tm, PROJ_TN), lambda i, j: (i, jnp.minimum(j, n_qkv - 1))),
            pl.BlockSpec((tm, PROJ_TN), lambda i, j: (i, jnp.maximum(j - n_qkv, 0))),
        ],
        out_shape=[
            jax.ShapeDtypeStruct((t_rows, SSM_WIDTH), jnp.float32),
            jax.ShapeDtypeStruct((t_rows, QKV_COLS), jnp.bfloat16),
            jax.ShapeDtypeStruct((t_rows, GATE_COLS), jnp.bfloat16),
        ],
        scratch_shapes=[pltpu.VMEM((tm, D_MODEL), jnp.bfloat16)],
        compiler_params=pltpu.CompilerParams(
            dimension_semantics=("parallel", "arbitrary"), vmem_limit_bytes=VMEM_LIMIT),
        name="ln_proj",
    )(x2, g, b, w_u, w_rest)


def _ssm_kernel(um_ref, u_ref, bb_ref, ab_ref, cm_ref, d_ref, wglu_ref, y_ref, state_ref, bu_ref, *, nb):
    def expand_and_scan(u_bf16, n_steps):
        rows = n_steps * nb
        bu_ref[pl.ds(0, rows), :] = jnp.dot(u_bf16, bb_ref[...], preferred_element_type=jnp.float32)
        for cb in range(SSM_COLS // SSM_CB):
            re_cols = pl.ds(cb * SSM_CB, SSM_CB)
            im_cols = pl.ds(SSM_COLS + cb * SSM_CB, SSM_CB)
            a_re = ab_ref[0:1, cb * SSM_CB:(cb + 1) * SSM_CB]
            a_im = ab_ref[1:2, cb * SSM_CB:(cb + 1) * SSM_CB]

            def step(t, carry):
                s_re, s_im = carry
                r0 = pl.multiple_of(t * nb, nb)
                n_re = a_re * s_re - a_im * s_im + bu_ref[pl.ds(r0, nb), re_cols]
                n_im = a_re * s_im + a_im * s_re + bu_ref[pl.ds(r0, nb), im_cols]
                bu_ref[pl.ds(r0, nb), re_cols] = n_re
                bu_ref[pl.ds(r0, nb), im_cols] = n_im
                return n_re, n_im

            s_re, s_im = lax.fori_loop(0, n_steps, step, (state_ref[:, re_cols], state_ref[:, im_cols]))
            state_ref[:, re_cols] = s_re
            state_ref[:, im_cols] = s_im

    @pl.when(pl.program_id(0) == 0)
    def _():
        state_ref[...] = jnp.zeros_like(state_ref)
        expand_and_scan(um_ref[...].astype(jnp.bfloat16), N_META)

    u = u_ref[...]
    expand_and_scan(u.astype(jnp.bfloat16), SSM_TC)
    y = jnp.dot(bu_ref[...].astype(jnp.bfloat16), cm_ref[...], preferred_element_type=jnp.float32)
    y = jax.nn.gelu(y + d_ref[...] * u)
    gate = jnp.dot(y.astype(jnp.bfloat16), wglu_ref[...], preferred_element_type=jnp.float32)
    y_ref[...] = (y * jax.nn.sigmoid(gate)).astype(y_ref.dtype)


def _ssm(u_meta_rows, u_tm, bb, ab, cm, d, wglu, nb):
    rows = u_tm.shape[0]
    blk = SSM_TC * nb
    assert rows % blk == 0 and nb % 8 == 0 and N_META <= SSM_TC
    const = lambda i: (0, 0)
    return pl.pallas_call(
        functools.partial(_ssm_kernel, nb=nb),
        grid=(rows // blk,),
        in_specs=[
            pl.BlockSpec((N_META * nb, SSM_WIDTH), const),
            pl.BlockSpec((blk, SSM_WIDTH), lambda i: (i, 0)),
            pl.BlockSpec((SSM_WIDTH, 2 * SSM_COLS), const),
            pl.BlockSpec((2, SSM_COLS), const),
            pl.BlockSpec((2 * SSM_COLS, SSM_WIDTH), const),
            pl.BlockSpec((1, SSM_WIDTH), const),
            pl.BlockSpec((SSM_WIDTH, SSM_WIDTH), const),
        ],
        out_specs=pl.BlockSpec((blk, SSM_WIDTH), lambda i: (i, 0)),
        out_shape=jax.ShapeDtypeStruct((rows, SSM_WIDTH), jnp.bfloat16),
        scratch_shapes=[
            pltpu.VMEM((nb, 2 * SSM_COLS), jnp.float32),
            pltpu.VMEM((blk, 2 * SSM_COLS), jnp.float32),
        ],
        compiler_params=pltpu.CompilerParams(
            dimension_semantics=("arbitrary",), vmem_limit_bytes=VMEM_LIMIT),
        name="ssm",
    )(u_meta_rows, u_tm, bb, ab, cm, d, wglu)


def _ssm_params(a_re, a_im, log_dt, b_re, b_im, c_re, c_im):
    f32 = jnp.float32
    lam_re = jnp.minimum(a_re.astype(f32), -1e-4)
    lam_im = a_im.astype(f32)
    dt = jnp.exp(log_dt.astype(f32))[:, None]
    mag = jnp.exp(lam_re * dt)
    ab_re = mag * jnp.cos(lam_im * dt)
    ab_im = mag * jnp.sin(lam_im * dt)
    den = lam_re * lam_re + lam_im * lam_im
    nr = ab_re - 1.0
    ni = ab_im
    z_re = (nr * lam_re + ni * lam_im) / den
    z_im = (ni * lam_re - nr * lam_im) / den
    br32 = b_re.astype(f32)
    bi32 = b_im.astype(f32)
    bb_re = z_re[..., None] * br32 - z_im[..., None] * bi32
    bb_im = z_re[..., None] * bi32 + z_im[..., None] * br32
    eye = jnp.eye(SSM_GROUPS, dtype=f32)
    exp_re = jnp.einsum('gpc,gh->gchp', bb_re, eye).reshape(SSM_WIDTH, SSM_COLS)
    exp_im = jnp.einsum('gpc,gh->gchp', bb_im, eye).reshape(SSM_WIDTH, SSM_COLS)
    bb = jnp.concatenate([exp_re, exp_im], axis=1)
    ro_re = jnp.einsum('gcp,gh->gphc', c_re.astype(f32), eye).reshape(SSM_COLS, SSM_WIDTH)
    ro_im = jnp.einsum('gcp,gh->gphc', c_im.astype(f32), eye).reshape(SSM_COLS, SSM_WIDTH)
    cm = jnp.concatenate([ro_re, -ro_im], axis=0)
    ab = jnp.stack([ab_re.reshape(SSM_COLS), ab_im.reshape(SSM_COLS)], axis=0)
    return bb.astype(jnp.bfloat16), ab, cm.astype(jnp.bfloat16)


def _attn_kernel(lam_ref, q_ref, k_ref, v_ref, km_ref, vm_ref, g_ref, o_ref, vt_ref, *, scale, seq):
    tq = ATTN_TQ
    f32, bf16 = jnp.float32, jnp.bfloat16
    vt_ref[...] = v_ref[...].astype(f32).T.astype(bf16)
    vm_t = vm_ref[...].astype(f32).T.astype(bf16)
    k_meta = km_ref[...]
    lane = lax.broadcasted_iota(jnp.int32, (tq, ATTN_VDIM), 1)
    key = lax.broadcasted_iota(jnp.int32, (tq, 2 * tq), 0)
    qry = lax.broadcasted_iota(jnp.int32, (tq, 2 * tq), 1) % tq
    diag_visible = qry // CHUNK >= key // CHUNK
    col_max = lambda a: jnp.max(a, axis=0, keepdims=True)
    col_sum = lambda a: jnp.sum(a, axis=0, keepdims=True)

    for i in range(seq // tq):
        rows = slice(i * tq, (i + 1) * tq)
        q = q_ref[rows, :] * scale
        zero = jnp.zeros_like(q)
        qq = jnp.concatenate([jnp.where(lane < ATTN_HEAD_DIM, q, zero),
                              jnp.where(lane >= ATTN_HEAD_DIM, q, zero)], axis=0)
        scores_t = lambda kb: lax.dot_general(kb, qq, (((1,), (1,)), ((), ())), preferred_element_type=f32)

        s_meta = scores_t(k_meta)
        s_diag = jnp.where(diag_visible, scores_t(k_ref[rows, :]), NEG_INF)
        m = jnp.maximum(col_max(s_meta), col_max(s_diag))
        if i > 0:
            s_prev = scores_t(k_ref[0:i * tq, :])
            m = jnp.maximum(m, col_max(s_prev))
        p_meta = jnp.exp(s_meta - m)
        p_diag = jnp.exp(s_diag - m)
        l = col_sum(p_meta) + col_sum(p_diag)
        acc = (jnp.dot(vm_t, p_meta.astype(bf16), preferred_element_type=f32)
               + jnp.dot(vt_ref[:, rows], p_diag.astype(bf16), preferred_element_type=f32))
        if i > 0:
            p_prev = jnp.exp(s_prev - m)
            l = l + col_sum(p_prev)
            acc = acc + jnp.dot(vt_ref[:, 0:i * tq], p_prev.astype(bf16), preferred_element_type=f32)

        o_all = acc / l
        o_t = o_all[:, :tq] - lam_ref[0] * o_all[:, tq:]
        o_t = o_t * lax.rsqrt(jnp.mean(o_t * o_t, axis=0, keepdims=True) + LN_EPS)
        o_ref[rows, :] = (o_t.T * g_ref[...]).astype(o_ref.dtype)


def _attention(lam, qkv, qkv_meta, g_scaled, nb, seq):
    assert seq % ATTN_TQ == 0
    nh = ATTN_HEADS
    head = lambda part: pl.BlockSpec((None, seq, ATTN_VDIM), lambda b, h: (b, 0, part * nh + h))
    return pl.pallas_call(
        functools.partial(_attn_kernel, scale=ATTN_HEAD_DIM ** -0.5, seq=seq),
        grid=(nb, nh),
        in_specs=[
            pl.BlockSpec(memory_space=pltpu.SMEM),
            head(0), head(1), head(2),
            pl.BlockSpec((N_META, ATTN_VDIM), lambda b, h: (0, nh + h)),
            pl.BlockSpec((N_META, ATTN_VDIM), lambda b, h: (0, 2 * nh + h)),
            pl.BlockSpec((1, ATTN_VDIM), lambda b, h: (0, 0)),
        ],
        out_specs=pl.BlockSpec((None, seq, ATTN_VDIM), lambda b, h: (b, 0, h)),
        out_shape=jax.ShapeDtypeStruct((nb, seq, ATTN_WIDTH), jnp.bfloat16),
        scratch_shapes=[pltpu.VMEM((ATTN_VDIM, seq), jnp.bfloat16)],
        compiler_params=pltpu.CompilerParams(
            dimension_semantics=("parallel", "parallel"), vmem_limit_bytes=VMEM_LIMIT),
        name="attention",
    )(lam, qkv, qkv, qkv, qkv_meta, qkv_meta, g_scaled)


def _merge_kernel(x_ref, gi_ref, bi_ref, gate_ref, ys_ref, ya_ref, wbs_ref, wba_ref, wo_ref,
                  g1_ref, b1_ref, wrh_ref, wrl_ref, br_ref, h1_ref, route_ref, cnt_ref):
    tm = x_ref.shape[0]

    @pl.when(pl.program_id(0) == 0)
    def _():
        cnt_ref[...] = jnp.zeros_like(cnt_ref)

    h = _layer_norm(x_ref[...], gi_ref[...], bi_ref[...])
    gates = gate_ref[...].astype(jnp.float32)
    ps = jnp.dot(ys_ref[...], wbs_ref[...], preferred_element_type=jnp.float32)
    pa = jnp.dot(ya_ref[...], wba_ref[...], preferred_element_type=jnp.float32)
    merged = jax.nn.sigmoid(gates[:, :D_MODEL]) * ps + jax.nn.sigmoid(gates[:, D_MODEL:]) * pa
    mix = jnp.dot(merged.astype(jnp.bfloat16), wo_ref[...], preferred_element_type=jnp.float32)
    h1 = _layer_norm(DEEPNORM_ALPHA * h + mix, g1_ref[...], b1_ref[...])
    h1_ref[...] = h1

    hi = h1.astype(jnp.bfloat16)
    lo = (h1 - hi.astype(jnp.float32)).astype(jnp.bfloat16)
    logits = (jnp.dot(hi, wrh_ref[...], preferred_element_type=jnp.float32)
              + jnp.dot(lo, wrh_ref[...], preferred_element_type=jnp.float32)
              + jnp.dot(hi, wrl_ref[...], preferred_element_type=jnp.float32)) + br_ref[...]
    lane_i = lax.broadcasted_iota(jnp.int32, logits.shape, 1)
    lane = lane_i.astype(jnp.float32)

    def first_argmax(vals):
        mx = jnp.max(vals, axis=-1, keepdims=True)
        idx = jnp.min(jnp.where(vals == mx, lane, float(LANES)), axis=-1, keepdims=True)
        return mx, idx

    is_group = jnp.logical_and(lane_i >= N_EXPERTS, lane_i < N_EXPERTS + N_EXPERT_GROUPS)
    gl = jnp.where(is_group, logits, NEG_INF)
    gmax, glane = first_argmax(gl)
    g_val = 1.0 / jnp.sum(jnp.where(is_group, jnp.exp(gl - gmax), 0.0), axis=-1, keepdims=True)
    g_idx = glane - float(N_EXPERTS)
    lane_group = (lane_i // EXPERTS_PER_GROUP).astype(jnp.float32)
    in_group = jnp.logical_and(lane_i < N_EXPERTS, lane_group == g_idx)
    el = jnp.where(in_group, logits, NEG_INF)
    m1, i1 = first_argmax(el)
    el2 = jnp.where(lane == i1, NEG_INF, el)
    m2, i2 = first_argmax(el2)
    e2 = jnp.exp(m2 - m1)
    w1 = g_val / (1.0 + e2)
    w2 = g_val * e2 / (1.0 + e2)

    onehot = jnp.logical_or(lane == i1, lane == i2)
    oh = jnp.where(onehot, 1.0, 0.0)
    r_i = lax.broadcasted_iota(jnp.int32, (tm, tm), 0)
    c_i = lax.broadcasted_iota(jnp.int32, (tm, tm), 1)
    tri = jnp.where(c_i < r_i, 1.0, 0.0).astype(jnp.bfloat16)
    before = jnp.dot(tri, oh.astype(jnp.bfloat16), preferred_element_type=jnp.float32) + cnt_ref[...]
    rank1 = jnp.sum(jnp.where(lane == i1, before, 0.0), axis=-1, keepdims=True)
    rank2 = jnp.sum(jnp.where(lane == i2, before, 0.0), axis=-1, keepdims=True)
    cnt_ref[...] = cnt_ref[...] + jnp.sum(oh, axis=0, keepdims=True)

    route = jnp.where(lane_i == 0, i1, 0.0)
    route = jnp.where(lane_i == 1, i2, route)
    route = jnp.where(lane_i == 2, w1, route)
    route = jnp.where(lane_i == 3, w2, route)
    route = jnp.where(lane_i == 4, rank1, route)
    route = jnp.where(lane_i == 5, rank2, route)
    route_ref[...] = route


def _merge(x2, gi, bi, gates, ys, ya, wbs, wba, wo, g1, b1, wrh, wrl, br):
    t_rows = x2.shape[0]
    tm = MERGE_TM
    assert t_rows % tm == 0
    row = lambda i: (i, 0)
    const = lambda i: (0, 0)
    return pl.pallas_call(
        _merge_kernel,
        grid=(t_rows // tm,),
        in_specs=[
            pl.BlockSpec((tm, D_MODEL), row),
            pl.BlockSpec((1, D_MODEL), const),
            pl.BlockSpec((1, D_MODEL), const),
            pl.BlockSpec((tm, GATE_COLS), row),
            pl.BlockSpec((tm, SSM_WIDTH), row),
            pl.BlockSpec((tm, ATTN_WIDTH), row),
            pl.BlockSpec((SSM_WIDTH, D_MODEL), const),
            pl.BlockSpec((ATTN_WIDTH, D_MODEL), const),
            pl.BlockSpec((D_MODEL, D_MODEL), const),
            pl.BlockSpec((1, D_MODEL), const),
            pl.BlockSpec((1, D_MODEL), const),
            pl.BlockSpec((D_MODEL, LANES), const),
            pl.BlockSpec((D_MODEL, LANES), const),
            pl.BlockSpec((1, LANES), const),
        ],
        out_specs=[
            pl.BlockSpec((tm, D_MODEL), row),
            pl.BlockSpec((tm, LANES), row),
            pl.BlockSpec((1, LANES), const),
        ],
        out_shape=[
            jax.ShapeDtypeStruct((t_rows, D_MODEL), jnp.float32),
            jax.ShapeDtypeStruct((t_rows, LANES), jnp.float32),
            jax.ShapeDtypeStruct((1, LANES), jnp.float32),
        ],
        compiler_params=pltpu.CompilerParams(
            dimension_semantics=("arbitrary",), vmem_limit_bytes=VMEM_LIMIT),
        name="merge",
    )(x2, gi, bi, gates, ys, ya, wbs, wba, wo, g1, b1, wrh, wrl, br)


def _dispatch_kernel(dest_ref, h1_ref, xs_in_hbm, xs_hbm, sem):
    del xs_in_hbm
    tm = DISPATCH_TM

    def body(t, carry):
        for k in range(TOP_K):
            d = dest_ref[0, TOP_K * t + k]
            pltpu.make_async_copy(h1_ref.at[pl.ds(t, 1)], xs_hbm.at[pl.ds(d, 1)], sem).start()
        return carry

    lax.fori_loop(0, tm, body, 0)
    for _ in range(TOP_K):
        pltpu.make_async_copy(h1_ref, xs_hbm.at[pl.ds(0, tm)], sem).wait()


def _dispatch(dest3, h1, xs_zero):
    t_rows = h1.shape[0]
    tm = DISPATCH_TM
    assert t_rows % tm == 0
    return pl.pallas_call(
        _dispatch_kernel,
        grid=(t_rows // tm,),
        in_specs=[
            pl.BlockSpec((None, 1, TOP_K * tm), lambda i: (i, 0, 0), memory_space=pltpu.SMEM),
            pl.BlockSpec((tm, D_MODEL), lambda i: (i, 0)),
            pl.BlockSpec(memory_space=pl.ANY),
        ],
        out_specs=pl.BlockSpec(memory_space=pl.ANY),
        out_shape=jax.ShapeDtypeStruct(xs_zero.shape, xs_zero.dtype),
        scratch_shapes=[pltpu.SemaphoreType.DMA(())],
        input_output_aliases={2: 0},
        compiler_params=pltpu.CompilerParams(
            dimension_semantics=("arbitrary",), vmem_limit_bytes=VMEM_LIMIT),
        name="dispatch",
    )(dest3, h1, xs_zero)


def _experts_kernel(be_ref, nu_ref, xs_ref, wg_ref, wu_ref, wd_ref, y_ref):
    del be_ref

    @pl.when(pl.program_id(0) < nu_ref[0])
    def _():
        xb = xs_ref[...].astype(jnp.bfloat16)
        g = jnp.dot(xb, wg_ref[...], preferred_element_type=jnp.float32)
        u = jnp.dot(xb, wu_ref[...], preferred_element_type=jnp.float32)
        hb = (jax.nn.silu(g) * u).astype(jnp.bfloat16)
        y_ref[...] = jnp.dot(hb, wd_ref[...], preferred_element_type=jnp.float32)

    @pl.when(pl.program_id(0) >= nu_ref[0])
    def _():
        y_ref[...] = jnp.zeros_like(y_ref)


def _experts(block_exp, n_used, xs, wg, wu, wd):
    r_rows = xs.shape[0]
    nblk = r_rows // ROW_BLOCK
    rows = lambda i, be, nu: (jnp.minimum(i, nu[0] - 1), 0)
    out_rows = lambda i, be, nu: (i, 0)
    wsel = lambda i, be, nu: (be[i], 0, 0)
    grid_spec = pltpu.PrefetchScalarGridSpec(
        num_scalar_prefetch=2,
        grid=(nblk,),
        in_specs=[
            pl.BlockSpec((ROW_BLOCK, D_MODEL), rows),
            pl.BlockSpec((None, D_MODEL, D_FF_EXPERT), wsel),
            pl.BlockSpec((None, D_MODEL, D_FF_EXPERT), wsel),
            pl.BlockSpec((None, D_FF_EXPERT, D_MODEL), wsel),
        ],
        out_specs=pl.BlockSpec((ROW_BLOCK, D_MODEL), out_rows),
    )
    return pl.pallas_call(
        _experts_kernel,
        grid_spec=grid_spec,
        out_shape=jax.ShapeDtypeStruct((r_rows, D_MODEL), jnp.float32),
        compiler_params=pltpu.CompilerParams(
            dimension_semantics=("arbitrary",), vmem_limit_bytes=VMEM_LIMIT),
        name="experts",
    )(block_exp, n_used, xs, wg, wu, wd)


def _combine_kernel(dest_ref, h1_ref, route_ref, g_ref, b_ref, y_hbm, o_ref, buf_ref, sem):
    tm = COMBINE_TM

    def body(t, carry):
        for k in range(TOP_K):
            d = dest_ref[0, TOP_K * t + k]
            pltpu.make_async_copy(y_hbm.at[pl.ds(d, 1)], buf_ref.at[k, pl.ds(t, 1)], sem).start()
        return carry

    lax.fori_loop(0, tm, body, 0)
    for k in range(TOP_K):
        pltpu.make_async_copy(y_hbm.at[pl.ds(0, tm)], buf_ref.at[k], sem).wait()

    route = route_ref[...]
    ffn = route[:, 2:3] * buf_ref[0] + route[:, 3:4] * buf_ref[1]
    o_ref[...] = _layer_norm(DEEPNORM_ALPHA * h1_ref[...] + ffn, g_ref[...], b_ref[...])


def _combine(dest3, h1, route, g2, b2, y_rows):
    t_rows = h1.shape[0]
    tm = COMBINE_TM
    assert t_rows % tm == 0
    row = lambda i: (i, 0)
    const = lambda i: (0, 0)
    return pl.pallas_call(
        _combine_kernel,
        grid=(t_rows // tm,),
        in_specs=[
            pl.BlockSpec((None, 1, TOP_K * tm), lambda i: (i, 0, 0), memory_space=pltpu.SMEM),
            pl.BlockSpec((tm, D_MODEL), row),
            pl.BlockSpec((tm, LANES), row),
            pl.BlockSpec((1, D_MODEL), const),
            pl.BlockSpec((1, D_MODEL), const),
            pl.BlockSpec(memory_space=pl.ANY),
        ],
        out_specs=pl.BlockSpec((tm, D_MODEL), row),
        out_shape=jax.ShapeDtypeStruct((t_rows, D_MODEL), jnp.float32),
        scratch_shapes=[
            pltpu.VMEM((TOP_K, tm, D_MODEL), jnp.float32),
            pltpu.SemaphoreType.DMA(()),
        ],
        compiler_params=pltpu.CompilerParams(
            dimension_semantics=("arbitrary",), vmem_limit_bytes=VMEM_LIMIT),
        name="combine",
    )(dest3, h1, route, g2, b2, y_rows)


def kernel(x, meta_tokens, ln_in_g, ln_in_b, w_in, ssm_a_re, ssm_a_im, ssm_log_dt, ssm_b_re, ssm_b_im,
           ssm_c_re, ssm_c_im, ssm_d, ssm_w_glu, attn_lambda_q1, attn_lambda_k1, attn_lambda_q2,
           attn_lambda_k2, attn_subln_g, w_br_ssm, w_br_attn, w_o, ln1_g, ln1_b, router_g_w, router_g_b,
           router_e_w, router_e_b, exp_w_gate, exp_w_up, exp_w_down, ln2_g, ln2_b):
    f32, bf16 = jnp.float32, jnp.bfloat16
    nb, seq, d = x.shape
    assert d == D_MODEL and w_in.shape[0] == DEPTH == 1
    t_rows = nb * seq
    l = 0
    lambda_init = 0.8 - 0.6 * math.exp(-0.3 * l)
    row2 = lambda v: v.reshape(1, -1).astype(f32)

    x2 = x.reshape(t_rows, d)
    w_u = w_in[l, :, :SSM_WIDTH].astype(bf16)
    w_rest = w_in[l, :, SSM_WIDTH:].astype(bf16)
    gi, bi = row2(ln_in_g), row2(ln_in_b)
    u, qkv, gates = _ln_proj(x2, gi, bi, w_u, w_rest, min(PROJ_TM, seq))
    u_meta, qkv_meta, _ = _ln_proj(meta_tokens.astype(f32), gi, bi, w_u, w_rest, N_META)

    bb, ab, cm = _ssm_params(ssm_a_re[l], ssm_a_im[l], ssm_log_dt[l], ssm_b_re[l], ssm_b_im[l],
                             ssm_c_re[l], ssm_c_im[l])
    u_tm = u.reshape(nb, seq, SSM_WIDTH).transpose(1, 0, 2).reshape(seq * nb, SSM_WIDTH)
    u_meta_rows = jnp.repeat(u_meta, nb, axis=0)
    y_ssm_tm = _ssm(u_meta_rows, u_tm, bb, ab, cm, row2(ssm_d[l]), ssm_w_glu[l].astype(bf16), nb)
    y_ssm = y_ssm_tm.reshape(seq, nb, SSM_WIDTH).transpose(1, 0, 2).reshape(t_rows, SSM_WIDTH)

    lam = (jnp.exp(jnp.sum(attn_lambda_q1[l].astype(f32) * attn_lambda_k1[l].astype(f32)))
           - jnp.exp(jnp.sum(attn_lambda_q2[l].astype(f32) * attn_lambda_k2[l].astype(f32)))
           + lambda_init).reshape(1)
    g_scaled = row2(attn_subln_g[l]) * (1.0 - lambda_init)
    y_attn = _attention(lam, qkv.reshape(nb, seq, QKV_COLS), qkv_meta, g_scaled, nb, seq)
    y_attn = y_attn.reshape(t_rows, ATTN_WIDTH)

    w_r = jnp.concatenate([router_e_w[l].astype(f32), router_g_w[l].astype(f32)], axis=1)
    w_r = jnp.pad(w_r, ((0, 0), (0, LANES - w_r.shape[1])))
    w_r_hi = w_r.astype(bf16)
    w_r_lo = (w_r - w_r_hi.astype(f32)).astype(bf16)
    b_r = jnp.concatenate([router_e_b[l].astype(f32), router_g_b[l].astype(f32)])
    b_r = jnp.pad(b_r, (0, LANES - b_r.shape[0])).reshape(1, LANES)
    h1, route, cnt = _merge(x2, gi, bi, gates, y_ssm, y_attn, w_br_ssm[l].astype(bf16),
                            w_br_attn[l].astype(bf16), w_o[l].astype(bf16), row2(ln1_g[l]), row2(ln1_b[l]),
                            w_r_hi, w_r_lo, b_r)

    counts = cnt[0, :N_EXPERTS].astype(jnp.int32)
    padded = (counts + ROW_BLOCK - 1) // ROW_BLOCK * ROW_BLOCK
    pad_ends = jnp.cumsum(padded)
    pad_starts = pad_ends - padded
    expert = route[:, 0:TOP_K].astype(jnp.int32)
    rank = route[:, 4:4 + TOP_K].astype(jnp.int32)
    dest = pad_starts[expert] + rank
    n_blocks = -(-(t_rows * TOP_K + N_EXPERTS * (ROW_BLOCK - 1)) // ROW_BLOCK)
    block_start = jnp.arange(n_blocks, dtype=jnp.int32) * ROW_BLOCK
    block_exp = jnp.minimum(jnp.sum(pad_ends[None, :] <= block_start[:, None], axis=1),
                            N_EXPERTS - 1).astype(jnp.int32)
    n_used = (pad_ends[-1:] // ROW_BLOCK).astype(jnp.int32)

    xs = _dispatch(dest.reshape(t_rows // DISPATCH_TM, 1, TOP_K * DISPATCH_TM), h1,
                   jnp.zeros((n_blocks * ROW_BLOCK, d), f32))
    y_rows = _experts(block_exp, n_used, xs, exp_w_gate[l].astype(bf16), exp_w_up[l].astype(bf16),
                      exp_w_down[l].astype(bf16))
    out = _combine(dest.reshape(t_rows // COMBINE_TM, 1, TOP_K * COMBINE_TM), h1, route,
                   row2(ln2_g[l]), row2(ln2_b[l]), y_rows)
    return out.reshape(nb, seq, d).astype(x.dtype)
```

```python
import functools
import math

import jax
import jax.numpy as jnp
from jax import lax
from jax.experimental import pallas as pl
from jax.experimental.pallas import tpu as pltpu

D_MODEL = 2048
N_META = 16
CHUNK = 64
ATTN_HEADS = 8
ATTN_HEAD_DIM = 64
ATTN_VDIM = 2 * ATTN_HEAD_DIM
ATTN_WIDTH = ATTN_HEADS * ATTN_VDIM
SSM_WIDTH = D_MODEL // 4
SSM_GROUP = 16
SSM_GROUPS = SSM_WIDTH // SSM_GROUP
SSM_STATE = 64
SSM_COLS = SSM_GROUPS * SSM_STATE
N_EXPERT_GROUPS = 4
EXPERTS_PER_GROUP = 8
N_EXPERTS = N_EXPERT_GROUPS * EXPERTS_PER_GROUP
TOP_K = 2
D_FF_EXPERT = D_MODEL // 4
QKV_COLS = 3 * ATTN_WIDTH
GATE_COLS = 2 * D_MODEL
IN_COLS = SSM_WIDTH + QKV_COLS + GATE_COLS
LN_EPS = 1e-5
DEPTH = 1
DEEPNORM_ALPHA = (2.0 * DEPTH) ** 0.25
LANES = 128
NEG_INF = -1e30

PROJ_TN = 1024
PROJ_TM = 1024
SSM_TC = 32
SSM_CB = 512
ATTN_TQ = 256
MERGE_TM = 256
ROW_BLOCK = 256
COMBINE_TM = 256
VMEM_LIMIT = 56 * 1024 * 1024


def _layer_norm(x, g, b):
    mu = jnp.mean(x, axis=-1, keepdims=True)
    xc = x - mu
    var = jnp.mean(xc * xc, axis=-1, keepdims=True)
    return xc * lax.rsqrt(var + LN_EPS) * g + b


def _ln_proj_kernel(x_ref, g_ref, b_ref, wu_ref, w_ref, u_ref, qkv_ref, gate_ref, xn_ref):
    j = pl.program_id(1)
    n_qkv = QKV_COLS // PROJ_TN

    @pl.when(j == 0)
    def _():
        xn = _layer_norm(x_ref[...], g_ref[...], b_ref[...]).astype(jnp.bfloat16)
        xn_ref[...] = xn
        u_ref[...] = jnp.dot(xn, wu_ref[...], preferred_element_type=jnp.float32)

    acc = jnp.dot(xn_ref[...], w_ref[...], preferred_element_type=jnp.float32)

    @pl.when(j < n_qkv)
    def _():
        qkv_ref[...] = acc.astype(jnp.bfloat16)

    @pl.when(j >= n_qkv)
    def _():
        gate_ref[...] = acc.astype(jnp.bfloat16)


def _ln_proj(x2, g, b, w_u, w_rest, tm):
    t_rows = x2.shape[0]
    n_qkv = QKV_COLS // PROJ_TN
    n_col = (QKV_COLS + GATE_COLS) // PROJ_TN
    assert QKV_COLS % PROJ_TN == 0 and GATE_COLS % PROJ_TN == 0 and t_rows % tm == 0
    return pl.pallas_call(
        _ln_proj_kernel,
        grid=(t_rows // tm, n_col),
        in_specs=[
            pl.BlockSpec((tm, D_MODEL), lambda i, j: (i, 0)),
            pl.BlockSpec((1, D_MODEL), lambda i, j: (0, 0)),
            pl.BlockSpec((1, D_MODEL), lambda i, j: (0, 0)),
            pl.BlockSpec((D_MODEL, SSM_WIDTH), lambda i, j: (0, 0)),
            pl.BlockSpec((D_MODEL, PROJ_TN), lambda i, j: (0, j)),
        ],
        out_specs=[
            pl.BlockSpec((tm, SSM_WIDTH), lambda i, j: (i, 0)),
            pl.BlockSpec((tm, PROJ_TN), lambda i, j: (i, jnp.minimum(j, n_qkv - 1))),
            pl.BlockSpec((tm, PROJ_TN), lambda i, j: (i, jnp.maximum(j - n_qkv, 0))),
        ],
        out_shape=[
            jax.ShapeDtypeStruct((t_rows, SSM_WIDTH), jnp.float32),
            jax.ShapeDtypeStruct((t_rows, QKV_COLS), jnp.bfloat16),
            jax.ShapeDtypeStruct((t_rows, GATE_COLS), jnp.bfloat16),
        ],
        scratch_shapes=[pltpu.VMEM((tm, D_MODEL), jnp.bfloat16)],
        compiler_params=pltpu.CompilerParams(
            dimension_semantics=("parallel", "arbitrary"), vmem_limit_bytes=VMEM_LIMIT),
        name="ln_proj",
    )(x2, g, b, w_u, w_rest)


def _ssm_kernel(um_ref, u_ref, bb_ref, ab_ref, cm_ref, d_ref, wglu_ref, y_ref, state_ref, bu_ref, *, nb):
    def expand_and_scan(u_bf16, n_steps):
        rows = n_steps * nb
        bu_ref[pl.ds(0, rows), :] = jnp.dot(u_bf16, bb_ref[...], preferred_element_type=jnp.float32)
        for cb in range(SSM_COLS // SSM_CB):
            re_cols = pl.ds(cb * SSM_CB, SSM_CB)
            im_cols = pl.ds(SSM_COLS + cb * SSM_CB, SSM_CB)
            a_re = ab_ref[0:1, cb * SSM_CB:(cb + 1) * SSM_CB]
            a_im = ab_ref[1:2, cb * SSM_CB:(cb + 1) * SSM_CB]

            def step(t, carry):
                s_re, s_im = carry
                r0 = pl.multiple_of(t * nb, nb)
                n_re = a_re * s_re - a_im * s_im + bu_ref[pl.ds(r0, nb), re_cols]
                n_im = a_re * s_im + a_im * s_re + bu_ref[pl.ds(r0, nb), im_cols]
                bu_ref[pl.ds(r0, nb), re_cols] = n_re
                bu_ref[pl.ds(r0, nb), im_cols] = n_im
                return n_re, n_im

            s_re, s_im = lax.fori_loop(0, n_steps, step, (state_ref[:, re_cols], state_ref[:, im_cols]))
            state_ref[:, re_cols] = s_re
            state_ref[:, im_cols] = s_im

    @pl.when(pl.program_id(0) == 0)
    def _():
        state_ref[...] = jnp.zeros_like(state_ref)
        expand_and_scan(um_ref[...].astype(jnp.bfloat16), N_META)

    u = u_ref[...]
    expand_and_scan(u.astype(jnp.bfloat16), SSM_TC)
    y = jnp.dot(bu_ref[...].astype(jnp.bfloat16), cm_ref[...], preferred_element_type=jnp.float32)
    y = jax.nn.gelu(y + d_ref[...] * u)
    gate = jnp.dot(y.astype(jnp.bfloat16), wglu_ref[...], preferred_element_type=jnp.float32)
    y_ref[...] = (y * jax.nn.sigmoid(gate)).astype(y_ref.dtype)


def _ssm(u_meta_rows, u_tm, bb, ab, cm, d, wglu, nb):
    rows = u_tm.shape[0]
    blk = SSM_TC * nb
    assert rows % blk == 0 and nb % 8 == 0 and N_META <= SSM_TC
    const = lambda i: (0, 0)
    return pl.pallas_call(
        functools.partial(_ssm_kernel, nb=nb),
        grid=(rows // blk,),
        in_specs=[
            pl.BlockSpec((N_META * nb, SSM_WIDTH), const),
            pl.BlockSpec((blk, SSM_WIDTH), lambda i: (i, 0)),
            pl.BlockSpec((SSM_WIDTH, 2 * SSM_COLS), const),
            pl.BlockSpec((2, SSM_COLS), const),
            pl.BlockSpec((2 * SSM_COLS, SSM_WIDTH), const),
            pl.BlockSpec((1, SSM_WIDTH), const),
            pl.BlockSpec((SSM_WIDTH, SSM_WIDTH), const),
        ],
        out_specs=pl.BlockSpec((blk, SSM_WIDTH), lambda i: (i, 0)),
        out_shape=jax.ShapeDtypeStruct((rows, SSM_WIDTH), jnp.bfloat16),
        scratch_shapes=[
            pltpu.VMEM((nb, 2 * SSM_COLS), jnp.float32),
            pltpu.VMEM((blk, 2 * SSM_COLS), jnp.float32),
        ],
        compiler_params=pltpu.CompilerParams(
            dimension_semantics=("arbitrary",), vmem_limit_bytes=VMEM_LIMIT),
        name="ssm",
    )(u_meta_rows, u_tm, bb, ab, cm, d, wglu)


def _ssm_params(a_re, a_im, log_dt, b_re, b_im, c_re, c_im):
    f32 = jnp.float32
    lam_re = jnp.minimum(a_re.astype(f32), -1e-4)
    lam_im = a_im.astype(f32)
    dt = jnp.exp(log_dt.astype(f32))[:, None]
    mag = jnp.exp(lam_re * dt)
    ab_re = mag * jnp.cos(lam_im * dt)
    ab_im = mag * jnp.sin(lam_im * dt)
    den = lam_re * lam_re + lam_im * lam_im
    nr = ab_re - 1.0
    ni = ab_im
    z_re = (nr * lam_re + ni * lam_im) / den
    z_im = (ni * lam_re - nr * lam_im) / den
    br32 = b_re.astype(f32)
    bi32 = b_im.astype(f32)
    bb_re = z_re[..., None] * br32 - z_im[..., None] * bi32
    bb_im = z_re[..., None] * bi32 + z_im[..., None] * br32
    eye = jnp.eye(SSM_GROUPS, dtype=f32)
    exp_re = jnp.einsum('gpc,gh->gchp', bb_re, eye).reshape(SSM_WIDTH, SSM_COLS)
    exp_im = jnp.einsum('gpc,gh->gchp', bb_im, eye).reshape(SSM_WIDTH, SSM_COLS)
    bb = jnp.concatenate([exp_re, exp_im], axis=1)
    ro_re = jnp.einsum('gcp,gh->gphc', c_re.astype(f32), eye).reshape(SSM_COLS, SSM_WIDTH)
    ro_im = jnp.einsum('gcp,gh->gphc', c_im.astype(f32), eye).reshape(SSM_COLS, SSM_WIDTH)
    cm = jnp.concatenate([ro_re, -ro_im], axis=0)
    ab = jnp.stack([ab_re.reshape(SSM_COLS), ab_im.reshape(SSM_COLS)], axis=0)
    return bb.astype(jnp.bfloat16), ab, cm.astype(jnp.bfloat16)


def _attn_kernel(lam_ref, q_ref, k_ref, v_ref, km_ref, vm_ref, g_ref, o_ref, vt_ref, *, scale, seq):
    tq = ATTN_TQ
    f32, bf16 = jnp.float32, jnp.bfloat16
    vt_ref[...] = v_ref[...].astype(f32).T.astype(bf16)
    vm_t = vm_ref[...].astype(f32).T.astype(bf16)
    k_meta = km_ref[...]
    lane = lax.broadcasted_iota(jnp.int32, (tq, ATTN_VDIM), 1)
    key = lax.broadcasted_iota(jnp.int32, (tq, 2 * tq), 0)
    qry = lax.broadcasted_iota(jnp.int32, (tq, 2 * tq), 1) % tq
    diag_visible = qry // CHUNK >= key // CHUNK
    col_max = lambda a: jnp.max(a, axis=0, keepdims=True)
    col_sum = lambda a: jnp.sum(a, axis=0, keepdims=True)

    for i in range(seq // tq):
        rows = slice(i * tq, (i + 1) * tq)
        q = q_ref[rows, :] * scale
        zero = jnp.zeros_like(q)
        qq = jnp.concatenate([jnp.where(lane < ATTN_HEAD_DIM, q, zero),
                              jnp.where(lane >= ATTN_HEAD_DIM, q, zero)], axis=0)
        scores_t = lambda kb: lax.dot_general(kb, qq, (((1,), (1,)), ((), ())), preferred_element_type=f32)

        s_meta = scores_t(k_meta)
        s_diag = jnp.where(diag_visible, scores_t(k_ref[rows, :]), NEG_INF)
        m = jnp.maximum(col_max(s_meta), col_max(s_diag))
        if i > 0:
            s_prev = scores_t(k_ref[0:i * tq, :])
            m = jnp.maximum(m, col_max(s_prev))
        p_meta = jnp.exp(s_meta - m)
        p_diag = jnp.exp(s_diag - m)
        l = col_sum(p_meta) + col_sum(p_diag)
        acc = (jnp.dot(vm_t, p_meta.astype(bf16), preferred_element_type=f32)
               + jnp.dot(vt_ref[:, rows], p_diag.astype(bf16), preferred_element_type=f32))
        if i > 0:
            p_prev = jnp.exp(s_prev - m)
            l = l + col_sum(p_prev)
            acc = acc + jnp.dot(vt_ref[:, 0:i * tq], p_prev.astype(bf16), preferred_element_type=f32)

        o_all = acc / l
        o_t = o_all[:, :tq] - lam_ref[0] * o_all[:, tq:]
        o_t = o_t * lax.rsqrt(jnp.mean(o_t * o_t, axis=0, keepdims=True) + LN_EPS)
        o_ref[rows, :] = (o_t.T * g_ref[...]).astype(o_ref.dtype)


def _attention(lam, qkv, qkv_meta, g_scaled, nb, seq):
    assert seq % ATTN_TQ == 0
    nh = ATTN_HEADS
    head = lambda part: pl.BlockSpec((None, seq, ATTN_VDIM), lambda b, h: (b, 0, part * nh + h))
    return pl.pallas_call(
        functools.partial(_attn_kernel, scale=ATTN_HEAD_DIM ** -0.5, seq=seq),
        grid=(nb, nh),
        in_specs=[
            pl.BlockSpec(memory_space=pltpu.SMEM),
            head(0), head(1), head(2),
            pl.BlockSpec((N_META, ATTN_VDIM), lambda b, h: (0, nh + h)),
            pl.BlockSpec((N_META, ATTN_VDIM), lambda b, h: (0, 2 * nh + h)),
            pl.BlockSpec((1, ATTN_VDIM), lambda b, h: (0, 0)),
        ],
        out_specs=pl.BlockSpec((None, seq, ATTN_VDIM), lambda b, h: (b, 0, h)),
        out_shape=jax.ShapeDtypeStruct((nb, seq, ATTN_WIDTH), jnp.bfloat16),
        scratch_shapes=[pltpu.VMEM((ATTN_VDIM, seq), jnp.bfloat16)],
        compiler_params=pltpu.CompilerParams(
            dimension_semantics=("parallel", "parallel"), vmem_limit_bytes=VMEM_LIMIT),
        name="attention",
    )(lam, qkv, qkv, qkv, qkv_meta, qkv_meta, g_scaled)


def _merge_kernel(x_ref, gi_ref, bi_ref, gate_ref, ys_ref, ya_ref, wbs_ref, wba_ref, wo_ref,
                  g1_ref, b1_ref, wrh_ref, wrl_ref, br_ref, h1_ref, route_ref, cnt_ref):
    tm = x_ref.shape[0]

    @pl.when(pl.program_id(0) == 0)
    def _():
        cnt_ref[...] = jnp.zeros_like(cnt_ref)

    h = _layer_norm(x_ref[...], gi_ref[...], bi_ref[...])
    gates = gate_ref[...].astype(jnp.float32)
    ps = jnp.dot(ys_ref[...], wbs_ref[...], preferred_element_type=jnp.float32)
    pa = jnp.dot(ya_ref[...], wba_ref[...], preferred_element_type=jnp.float32)
    merged = jax.nn.sigmoid(gates[:, :D_MODEL]) * ps + jax.nn.sigmoid(gates[:, D_MODEL:]) * pa
    mix = jnp.dot(merged.astype(jnp.bfloat16), wo_ref[...], preferred_element_type=jnp.float32)
    h1 = _layer_norm(DEEPNORM_ALPHA * h + mix, g1_ref[...], b1_ref[...])
    h1_ref[...] = h1

    hi = h1.astype(jnp.bfloat16)
    lo = (h1 - hi.astype(jnp.float32)).astype(jnp.bfloat16)
    logits = (jnp.dot(hi, wrh_ref[...], preferred_element_type=jnp.float32)
              + jnp.dot(lo, wrh_ref[...], preferred_element_type=jnp.float32)
              + jnp.dot(hi, wrl_ref[...], preferred_element_type=jnp.float32)) + br_ref[...]
    lane_i = lax.broadcasted_iota(jnp.int32, logits.shape, 1)
    lane = lane_i.astype(jnp.float32)

    def first_argmax(vals):
        mx = jnp.max(vals, axis=-1, keepdims=True)
        idx = jnp.min(jnp.where(vals == mx, lane, float(LANES)), axis=-1, keepdims=True)
        return mx, idx

    is_group = jnp.logical_and(lane_i >= N_EXPERTS, lane_i < N_EXPERTS + N_EXPERT_GROUPS)
    gl = jnp.where(is_group, logits, NEG_INF)
    gmax, glane = first_argmax(gl)
    g_val = 1.0 / jnp.sum(jnp.where(is_group, jnp.exp(gl - gmax), 0.0), axis=-1, keepdims=True)
    g_idx = glane - float(N_EXPERTS)
    lane_group = (lane_i // EXPERTS_PER_GROUP).astype(jnp.float32)
    in_group = jnp.logical_and(lane_i < N_EXPERTS, lane_group == g_idx)
    el = jnp.where(in_group, logits, NEG_INF)
    m1, i1 = first_argmax(el)
    el2 = jnp.where(lane == i1, NEG_INF, el)
    m2, i2 = first_argmax(el2)
    e2 = jnp.exp(m2 - m1)
    w1 = g_val / (1.0 + e2)
    w2 = g_val * e2 / (1.0 + e2)

    onehot = jnp.logical_or(lane == i1, lane == i2)
    oh = jnp.where(onehot, 1.0, 0.0)
    r_i = lax.broadcasted_iota(jnp.int32, (tm, tm), 0)
    c_i = lax.broadcasted_iota(jnp.int32, (tm, tm), 1)
    tri = jnp.where(c_i < r_i, 1.0, 0.0).astype(jnp.bfloat16)
    before = jnp.dot(tri, oh.astype(jnp.bfloat16), preferred_element_type=jnp.float32) + cnt_ref[...]
    rank1 = jnp.sum(jnp.where(lane == i1, before, 0.0), axis=-1, keepdims=True)
    rank2 = jnp.sum(jnp.where(lane == i2, before, 0.0), axis=-1, keepdims=True)
    cnt_ref[...] = cnt_ref[...] + jnp.sum(oh, axis=0, keepdims=True)

    route = jnp.where(lane_i == 0, i1, 0.0)
    route = jnp.where(lane_i == 1, i2, route)
    route = jnp.where(lane_i == 2, w1, route)
    route = jnp.where(lane_i == 3, w2, route)
    route = jnp.where(lane_i == 4, rank1, route)
    route = jnp.where(lane_i == 5, rank2, route)
    route_ref[...] = route


def _merge(x2, gi, bi, gates, ys, ya, wbs, wba, wo, g1, b1, wrh, wrl, br):
    t_rows = x2.shape[0]
    tm = MERGE_TM
    assert t_rows % tm == 0
    row = lambda i: (i, 0)
    const = lambda i: (0, 0)
    return pl.pallas_call(
        _merge_kernel,
        grid=(t_rows // tm,),
        in_specs=[
            pl.BlockSpec((tm, D_MODEL), row),
            pl.BlockSpec((1, D_MODEL), const),
            pl.BlockSpec((1, D_MODEL), const),
            pl.BlockSpec((tm, GATE_COLS), row),
            pl.BlockSpec((tm, SSM_WIDTH), row),
            pl.BlockSpec((tm, ATTN_WIDTH), row),
            pl.BlockSpec((SSM_WIDTH, D_MODEL), const),
            pl.BlockSpec((ATTN_WIDTH, D_MODEL), const),
            pl.BlockSpec((D_MODEL, D_MODEL), const),
            pl.BlockSpec((1, D_MODEL), const),
            pl.BlockSpec((1, D_MODEL), const),
            pl.BlockSpec((D_MODEL, LANES), const),
            pl.BlockSpec((D_MODEL, LANES), const),
            pl.BlockSpec((1, LANES), const),
        ],
        out_specs=[
            pl.BlockSpec((tm, D_MODEL), row),
            pl.BlockSpec((tm, LANES), row),
            pl.BlockSpec((1, LANES), const),
        ],
        out_shape=[
            jax.ShapeDtypeStruct((t_rows, D_MODEL), jnp.float32),
            jax.ShapeDtypeStruct((t_rows, LANES), jnp.float32),
            jax.ShapeDtypeStruct((1, LANES), jnp.float32),
        ],
        compiler_params=pltpu.CompilerParams(
            dimension_semantics=("arbitrary",), vmem_limit_bytes=VMEM_LIMIT),
        name="merge",
    )(x2, gi, bi, gates, ys, ya, wbs, wba, wo, g1, b1, wrh, wrl, br)


def _moe_kernel(be_ref, nu_ref, src_cur_ref, src_next_ref, dst_prev_ref, h1_hbm, wg_ref, wu_ref, wd_ref,
                ysc_hbm, xbuf, ybuf, wgb, wub, wdb, gsem, ssem):
    i = pl.program_id(0)
    n_used = nu_ref[0]
    slot = i % 2
    other = 1 - slot
    rb = ROW_BLOCK

    def start_gather(idx_ref, s):
        for r in range(rb):
            pltpu.make_async_copy(h1_hbm.at[pl.ds(idx_ref[0, r], 1)], xbuf.at[s, pl.ds(r, 1)],
                                  gsem.at[s]).start()

    def wait_gather(s):
        pltpu.make_async_copy(h1_hbm.at[pl.ds(0, rb)], xbuf.at[s], gsem.at[s]).wait()

    def start_scatter(s):
        for r in range(rb):
            pltpu.make_async_copy(ybuf.at[s, pl.ds(r, 1)], ysc_hbm.at[pl.ds(dst_prev_ref[0, r], 1)],
                                  ssem.at[0]).start()

    def wait_scatter(s):
        pltpu.make_async_copy(ybuf.at[s], ysc_hbm.at[pl.ds(0, rb)], ssem.at[0]).wait()

    @pl.when(i == 0)
    def _():
        ybuf[...] = jnp.zeros_like(ybuf)
        start_gather(src_cur_ref, 0)

    @pl.when(i <= n_used)
    def _():
        prev = jnp.maximum(i - 1, 0)

        @pl.when(jnp.logical_or(i == 0, be_ref[i] != be_ref[prev]))
        def _():
            wgb[...] = wg_ref[...].astype(jnp.bfloat16)
            wub[...] = wu_ref[...].astype(jnp.bfloat16)
            wdb[...] = wd_ref[...].astype(jnp.bfloat16)

        wait_gather(slot)

        @pl.when(i >= 1)
        def _():
            wait_scatter(slot)

        start_gather(src_next_ref, other)
        start_scatter(other)
        xb = xbuf[slot].astype(jnp.bfloat16)
        g = jnp.dot(xb, wgb[...], preferred_element_type=jnp.float32)
        u = jnp.dot(xb, wub[...], preferred_element_type=jnp.float32)
        hb = (jax.nn.silu(g) * u).astype(jnp.bfloat16)
        ybuf[slot] = jnp.dot(hb, wdb[...], preferred_element_type=jnp.float32)

        @pl.when(i == n_used)
        def _():
            wait_gather(other)
            wait_scatter(other)


def _moe(block_exp, n_used, src_blocks, dst_blocks, h1, wg, wu, wd, n_blocks):
    t_rows = h1.shape[0]
    rb = ROW_BLOCK
    wsel = lambda i, be, nu: (be[i], 0, 0)
    smem_blk = lambda off: pl.BlockSpec((None, 1, rb), lambda i, be, nu: (i + off, 0, 0),
                                        memory_space=pltpu.SMEM)
    grid_spec = pltpu.PrefetchScalarGridSpec(
        num_scalar_prefetch=2,
        grid=(n_blocks + 1,),
        in_specs=[
            smem_blk(0), smem_blk(1), smem_blk(0),
            pl.BlockSpec(memory_space=pl.ANY),
            pl.BlockSpec((None, D_MODEL, D_FF_EXPERT), wsel),
            pl.BlockSpec((None, D_MODEL, D_FF_EXPERT), wsel),
            pl.BlockSpec((None, D_FF_EXPERT, D_MODEL), wsel),
        ],
        out_specs=pl.BlockSpec(memory_space=pl.ANY),
        scratch_shapes=[
            pltpu.VMEM((2, rb, D_MODEL), jnp.float32),
            pltpu.VMEM((2, rb, D_MODEL), jnp.float32),
            pltpu.VMEM((D_MODEL, D_FF_EXPERT), jnp.bfloat16),
            pltpu.VMEM((D_MODEL, D_FF_EXPERT), jnp.bfloat16),
            pltpu.VMEM((D_FF_EXPERT, D_MODEL), jnp.bfloat16),
            pltpu.SemaphoreType.DMA((2,)),
            pltpu.SemaphoreType.DMA((1,)),
        ],
    )
    return pl.pallas_call(
        _moe_kernel,
        grid_spec=grid_spec,
        out_shape=jax.ShapeDtypeStruct((TOP_K * t_rows + rb, D_MODEL), jnp.float32),
        compiler_params=pltpu.CompilerParams(
            dimension_semantics=("arbitrary",), vmem_limit_bytes=VMEM_LIMIT),
        name="experts",
    )(block_exp, n_used, src_blocks, src_blocks, dst_blocks, h1, wg, wu, wd)


def _combine_kernel(h1_ref, route_ref, y0_ref, y1_ref, g_ref, b_ref, o_ref):
    route = route_ref[...]
    ffn = route[:, 2:3] * y0_ref[...] + route[:, 3:4] * y1_ref[...]
    o_ref[...] = _layer_norm(DEEPNORM_ALPHA * h1_ref[...] + ffn, g_ref[...], b_ref[...])


def _combine(h1, route, ysc, g2, b2):
    t_rows = h1.shape[0]
    tm = COMBINE_TM
    assert t_rows % tm == 0 and ROW_BLOCK % tm == 0
    second = (t_rows + ROW_BLOCK) // tm
    row = lambda i: (i, 0)
    const = lambda i: (0, 0)
    return pl.pallas_call(
        _combine_kernel,
        grid=(t_rows // tm,),
        in_specs=[
            pl.BlockSpec((tm, D_MODEL), row),
            pl.BlockSpec((tm, LANES), row),
            pl.BlockSpec((tm, D_MODEL), row),
            pl.BlockSpec((tm, D_MODEL), lambda i: (i + second, 0)),
            pl.BlockSpec((1, D_MODEL), const),
            pl.BlockSpec((1, D_MODEL), const),
        ],
        out_specs=pl.BlockSpec((tm, D_MODEL), row),
        out_shape=jax.ShapeDtypeStruct((t_rows, D_MODEL), jnp.float32),
        compiler_params=pltpu.CompilerParams(
            dimension_semantics=("parallel",), vmem_limit_bytes=VMEM_LIMIT),
        name="combine",
    )(h1, route, ysc, ysc, g2, b2)


def kernel(x, meta_tokens, ln_in_g, ln_in_b, w_in, ssm_a_re, ssm_a_im, ssm_log_dt, ssm_b_re, ssm_b_im,
           ssm_c_re, ssm_c_im, ssm_d, ssm_w_glu, attn_lambda_q1, attn_lambda_k1, attn_lambda_q2,
           attn_lambda_k2, attn_subln_g, w_br_ssm, w_br_attn, w_o, ln1_g, ln1_b, router_g_w, router_g_b,
           router_e_w, router_e_b, exp_w_gate, exp_w_up, exp_w_down, ln2_g, ln2_b):
    f32, bf16 = jnp.float32, jnp.bfloat16
    nb, seq, d = x.shape
    assert d == D_MODEL and w_in.shape[0] == DEPTH == 1
    t_rows = nb * seq
    l = 0
    lambda_init = 0.8 - 0.6 * math.exp(-0.3 * l)
    row2 = lambda v: v.reshape(1, -1).astype(f32)

    x2 = x.reshape(t_rows, d)
    w_u = w_in[l, :, :SSM_WIDTH].astype(bf16)
    w_rest = w_in[l, :, SSM_WIDTH:].astype(bf16)
    gi, bi = row2(ln_in_g), row2(ln_in_b)
    u, qkv, gates = _ln_proj(x2, gi, bi, w_u, w_rest, min(PROJ_TM, seq))
    u_meta, qkv_meta, _ = _ln_proj(meta_tokens.astype(f32), gi, bi, w_u, w_rest, N_META)

    bb, ab, cm = _ssm_params(ssm_a_re[l], ssm_a_im[l], ssm_log_dt[l], ssm_b_re[l], ssm_b_im[l],
                             ssm_c_re[l], ssm_c_im[l])
    u_tm = u.reshape(nb, seq, SSM_WIDTH).transpose(1, 0, 2).reshape(seq * nb, SSM_WIDTH)
    u_meta_rows = jnp.repeat(u_meta, nb, axis=0)
    y_ssm_tm = _ssm(u_meta_rows, u_tm, bb, ab, cm, row2(ssm_d[l]), ssm_w_glu[l].astype(bf16), nb)
    y_ssm = y_ssm_tm.reshape(seq, nb, SSM_WIDTH).transpose(1, 0, 2).reshape(t_rows, SSM_WIDTH)

    lam = (jnp.exp(jnp.sum(attn_lambda_q1[l].astype(f32) * attn_lambda_k1[l].astype(f32)))
           - jnp.exp(jnp.sum(attn_lambda_q2[l].astype(f32) * attn_lambda_k2[l].astype(f32)))
           + lambda_init).reshape(1)
    g_scaled = row2(attn_subln_g[l]) * (1.0 - lambda_init)
    y_attn = _attention(lam, qkv.reshape(nb, seq, QKV_COLS), qkv_meta, g_scaled, nb, seq)
    y_attn = y_attn.reshape(t_rows, ATTN_WIDTH)

    w_r = jnp.concatenate([router_e_w[l].astype(f32), router_g_w[l].astype(f32)], axis=1)
    w_r = jnp.pad(w_r, ((0, 0), (0, LANES - w_r.shape[1])))
    w_r_hi = w_r.astype(bf16)
    w_r_lo = (w_r - w_r_hi.astype(f32)).astype(bf16)
    b_r = jnp.concatenate([router_e_b[l].astype(f32), router_g_b[l].astype(f32)])
    b_r = jnp.pad(b_r, (0, LANES - b_r.shape[0])).reshape(1, LANES)
    h1, route, cnt = _merge(x2, gi, bi, gates, y_ssm, y_attn, w_br_ssm[l].astype(bf16),
                            w_br_attn[l].astype(bf16), w_o[l].astype(bf16), row2(ln1_g[l]), row2(ln1_b[l]),
                            w_r_hi, w_r_lo, b_r)

    counts = cnt[0, :N_EXPERTS].astype(jnp.int32)
    padded = (counts + ROW_BLOCK - 1) // ROW_BLOCK * ROW_BLOCK
    pad_ends = jnp.cumsum(padded)
    pad_starts = pad_ends - padded
    expert = route[:, 0:TOP_K].astype(jnp.int32)
    rank = route[:, 4:4 + TOP_K].astype(jnp.int32)
    dest = pad_starts[expert] + rank
    n_blocks = -(-(t_rows * TOP_K + N_EXPERTS * (ROW_BLOCK - 1)) // ROW_BLOCK)
    block_start = jnp.arange(n_blocks, dtype=jnp.int32) * ROW_BLOCK
    block_exp = jnp.minimum(jnp.sum(pad_ends[None, :] <= block_start[:, None], axis=1),
                            N_EXPERTS - 1).astype(jnp.int32)
    block_exp = jnp.concatenate([block_exp, block_exp[-1:]])
    n_used = (pad_ends[-1:] // ROW_BLOCK).astype(jnp.int32)

    rb = ROW_BLOCK
    stride = t_rows + rb
    tok = jnp.arange(t_rows, dtype=jnp.int32)[:, None]
    slot_token = tok + jnp.arange(TOP_K, dtype=jnp.int32)[None, :] * stride
    dump = t_rows + jnp.arange(n_blocks * rb, dtype=jnp.int32) % rb
    row_map = dump.at[dest.reshape(-1)].set(slot_token.reshape(-1), unique_indices=True)
    dst_blocks = jnp.concatenate([dump[:rb], row_map]).reshape(n_blocks + 1, 1, rb)
    src = jnp.minimum(row_map % stride, t_rows - 1)
    src_blocks = jnp.concatenate([src, jnp.zeros((2 * rb,), jnp.int32)]).reshape(n_blocks + 2, 1, rb)

    ysc = _moe(block_exp, n_used, src_blocks, dst_blocks, h1, exp_w_gate[l], exp_w_up[l], exp_w_down[l],
               n_blocks)
    out = _combine(h1, route, ysc, row2(ln2_g[l]), row2(ln2_b[l]))
    return out.reshape(nb, seq, d).astype(x.dtype)
```

```python
import functools
import math

import jax
import jax.numpy as jnp
from jax import lax
from jax.experimental import pallas as pl
from jax.experimental.pallas import tpu as pltpu

D_MODEL = 2048
N_META = 16
CHUNK = 64
ATTN_HEADS = 8
ATTN_HEAD_DIM = 64
ATTN_VDIM = 2 * ATTN_HEAD_DIM
ATTN_WIDTH = ATTN_HEADS * ATTN_VDIM
SSM_WIDTH = D_MODEL // 4
SSM_GROUP = 16
SSM_GROUPS = SSM_WIDTH // SSM_GROUP
SSM_STATE = 64
SSM_COLS = SSM_GROUPS * SSM_STATE
N_EXPERT_GROUPS = 4
EXPERTS_PER_GROUP = 8
N_EXPERTS = N_EXPERT_GROUPS * EXPERTS_PER_GROUP
TOP_K = 2
D_FF_EXPERT = D_MODEL // 4
QKV_COLS = 3 * ATTN_WIDTH
GATE_COLS = 2 * D_MODEL
IN_COLS = SSM_WIDTH + QKV_COLS + GATE_COLS
LN_EPS = 1e-5
DEPTH = 1
DEEPNORM_ALPHA = (2.0 * DEPTH) ** 0.25
LANES = 128
NEG_INF = -1e30

PROJ_TN = 1024
PROJ_TM = 1024
SSM_TC = 32
SSM_CB = 512
SSM_HALVES = 2
ATTN_TQ = 256
MERGE_TM = 256
ROW_BLOCK = 256
COMBINE_TM = 256
ROWMAP_TM = 512
VMEM_LIMIT = 56 * 1024 * 1024


def _layer_norm(x, g, b):
    mu = jnp.mean(x, axis=-1, keepdims=True)
    xc = x - mu
    var = jnp.mean(xc * xc, axis=-1, keepdims=True)
    return xc * lax.rsqrt(var + LN_EPS) * g + b


def _ln_proj_kernel(x_ref, g_ref, b_ref, wu_ref, w_ref, u_ref, proj_ref, xn_ref):
    j = pl.program_id(1)

    @pl.when(j == 0)
    def _():
        xn = _layer_norm(x_ref[...], g_ref[...], b_ref[...]).astype(jnp.bfloat16)
        xn_ref[...] = xn
        u_ref[...] = jnp.dot(xn, wu_ref[...], preferred_element_type=jnp.float32)

    acc = jnp.dot(xn_ref[...], w_ref[...], preferred_element_type=jnp.float32)
    proj_ref[...] = jnp.where(j < GATE_COLS // PROJ_TN, jax.nn.sigmoid(acc), acc).astype(jnp.bfloat16)


def _ln_proj(x2, g, b, w_u, w_rest, tm):
    t_rows = x2.shape[0]
    n_col = (GATE_COLS + QKV_COLS) // PROJ_TN
    assert QKV_COLS % PROJ_TN == 0 and GATE_COLS % PROJ_TN == 0 and t_rows % tm == 0
    return pl.pallas_call(
        _ln_proj_kernel,
        grid=(t_rows // tm, n_col),
        in_specs=[
            pl.BlockSpec((tm, D_MODEL), lambda i, j: (i, 0)),
            pl.BlockSpec((1, D_MODEL), lambda i, j: (0, 0)),
            pl.BlockSpec((1, D_MODEL), lambda i, j: (0, 0)),
            pl.BlockSpec((D_MODEL, SSM_WIDTH), lambda i, j: (0, 0)),
            pl.BlockSpec((D_MODEL, PROJ_TN), lambda i, j: (0, j)),
        ],
        out_specs=[
            pl.BlockSpec((tm, SSM_WIDTH), lambda i, j: (i, 0)),
            pl.BlockSpec((tm, PROJ_TN), lambda i, j: (i, j)),
        ],
        out_shape=[
            jax.ShapeDtypeStruct((t_rows, SSM_WIDTH), jnp.float32),
            jax.ShapeDtypeStruct((t_rows, GATE_COLS + QKV_COLS), jnp.bfloat16),
        ],
        scratch_shapes=[pltpu.VMEM((tm, D_MODEL), jnp.bfloat16)],
        compiler_params=pltpu.CompilerParams(
            dimension_semantics=("parallel", "arbitrary"), vmem_limit_bytes=VMEM_LIMIT),
        name="ln_proj",
    )(x2, g, b, w_u, w_rest)


def _ssm_kernel(um_ref, u_ref, bb_ref, ab_ref, cm_ref, d_ref, wglu_ref, y_ref, state_ref, bu_ref, *, nb):
    half_u = SSM_WIDTH // SSM_HALVES
    half_c = SSM_COLS // SSM_HALVES

    def expand_and_scan(u_bf16, n_steps):
        rows = n_steps * nb
        for h in range(SSM_HALVES):
            bu_ref[pl.ds(0, rows), 2 * half_c * h:2 * half_c * (h + 1)] = jnp.dot(
                u_bf16[:, half_u * h:half_u * (h + 1)], bb_ref[h], preferred_element_type=jnp.float32)
        for cb in range(SSM_COLS // SSM_CB):
            h, off = divmod(cb * SSM_CB, half_c)
            re_cols = pl.ds(2 * half_c * h + off, SSM_CB)
            im_cols = pl.ds(2 * half_c * h + half_c + off, SSM_CB)
            a_re = ab_ref[0:1, cb * SSM_CB:(cb + 1) * SSM_CB]
            a_im = ab_ref[1:2, cb * SSM_CB:(cb + 1) * SSM_CB]

            def step(t, carry):
                s_re, s_im = carry
                r0 = pl.multiple_of(t * nb, nb)
                n_re = a_re * s_re - a_im * s_im + bu_ref[pl.ds(r0, nb), re_cols]
                n_im = a_re * s_im + a_im * s_re + bu_ref[pl.ds(r0, nb), im_cols]
                bu_ref[pl.ds(r0, nb), re_cols] = n_re
                bu_ref[pl.ds(r0, nb), im_cols] = n_im
                return n_re, n_im

            s_re, s_im = lax.fori_loop(0, n_steps, step, (state_ref[:, re_cols], state_ref[:, im_cols]))
            state_ref[:, re_cols] = s_re
            state_ref[:, im_cols] = s_im

    @pl.when(pl.program_id(0) == 0)
    def _():
        state_ref[...] = jnp.zeros_like(state_ref)
        expand_and_scan(um_ref[...].astype(jnp.bfloat16), N_META)

    u = u_ref[...]
    expand_and_scan(u.astype(jnp.bfloat16), SSM_TC)
    y = jnp.concatenate(
        [jnp.dot(bu_ref[:, 2 * half_c * h:2 * half_c * (h + 1)].astype(jnp.bfloat16), cm_ref[h],
                 preferred_element_type=jnp.float32) for h in range(SSM_HALVES)], axis=1)
    y = jax.nn.gelu(y + d_ref[...] * u)
    gate = jnp.dot(y.astype(jnp.bfloat16), wglu_ref[...], preferred_element_type=jnp.float32)
    y_ref[...] = (y * jax.nn.sigmoid(gate)).astype(y_ref.dtype)


def _ssm(u_meta_rows, u_tm, bb, ab, cm, d, wglu, nb):
    rows = u_tm.shape[0]
    blk = SSM_TC * nb
    assert rows % blk == 0 and nb % 8 == 0 and N_META <= SSM_TC
    const = lambda i: (0, 0)
    return pl.pallas_call(
        functools.partial(_ssm_kernel, nb=nb),
        grid=(rows // blk,),
        in_specs=[
            pl.BlockSpec((N_META * nb, SSM_WIDTH), const),
            pl.BlockSpec((blk, SSM_WIDTH), lambda i: (i, 0)),
            pl.BlockSpec((SSM_HALVES, SSM_WIDTH // SSM_HALVES, 2 * SSM_COLS // SSM_HALVES), lambda i: (0, 0, 0)),
            pl.BlockSpec((2, SSM_COLS), const),
            pl.BlockSpec((SSM_HALVES, 2 * SSM_COLS // SSM_HALVES, SSM_WIDTH // SSM_HALVES), lambda i: (0, 0, 0)),
            pl.BlockSpec((1, SSM_WIDTH), const),
            pl.BlockSpec((SSM_WIDTH, SSM_WIDTH), const),
        ],
        out_specs=pl.BlockSpec((blk, SSM_WIDTH), lambda i: (i, 0)),
        out_shape=jax.ShapeDtypeStruct((rows, SSM_WIDTH), jnp.bfloat16),
        scratch_shapes=[
            pltpu.VMEM((nb, 2 * SSM_COLS), jnp.float32),
            pltpu.VMEM((blk, 2 * SSM_COLS), jnp.float32),
        ],
        compiler_params=pltpu.CompilerParams(
            dimension_semantics=("arbitrary",), vmem_limit_bytes=VMEM_LIMIT),
        name="ssm",
    )(u_meta_rows, u_tm, bb, ab, cm, d, wglu)


def _ssm_params(a_re, a_im, log_dt, b_re, b_im, c_re, c_im):
    f32 = jnp.float32
    lam_re = jnp.minimum(a_re.astype(f32), -1e-4)
    lam_im = a_im.astype(f32)
    dt = jnp.exp(log_dt.astype(f32))[:, None]
    mag = jnp.exp(lam_re * dt)
    ab_re = mag * jnp.cos(lam_im * dt)
    ab_im = mag * jnp.sin(lam_im * dt)
    den = lam_re * lam_re + lam_im * lam_im
    nr = ab_re - 1.0
    ni = ab_im
    z_re = (nr * lam_re + ni * lam_im) / den
    z_im = (ni * lam_re - nr * lam_im) / den
    br32 = b_re.astype(f32)
    bi32 = b_im.astype(f32)
    bb_re = z_re[..., None] * br32 - z_im[..., None] * bi32
    bb_im = z_re[..., None] * bi32 + z_im[..., None] * br32
    gh = SSM_GROUPS // SSM_HALVES
    eye = jnp.eye(gh, dtype=f32)
    split = lambda a: a.reshape((SSM_HALVES, gh) + a.shape[1:])
    exp_re = jnp.einsum('kgpc,gh->kgchp', split(bb_re), eye).reshape(SSM_HALVES, gh * SSM_GROUP, gh * SSM_STATE)
    exp_im = jnp.einsum('kgpc,gh->kgchp', split(bb_im), eye).reshape(SSM_HALVES, gh * SSM_GROUP, gh * SSM_STATE)
    bb = jnp.concatenate([exp_re, exp_im], axis=2)
    ro_re = jnp.einsum('kgcp,gh->kgphc', split(c_re.astype(f32)), eye).reshape(
        SSM_HALVES, gh * SSM_STATE, gh * SSM_GROUP)
    ro_im = jnp.einsum('kgcp,gh->kgphc', split(c_im.astype(f32)), eye).reshape(
        SSM_HALVES, gh * SSM_STATE, gh * SSM_GROUP)
    cm = jnp.concatenate([ro_re, -ro_im], axis=1)
    ab = jnp.stack([ab_re.reshape(SSM_COLS), ab_im.reshape(SSM_COLS)], axis=0)
    return bb.astype(jnp.bfloat16), ab, cm.astype(jnp.bfloat16)


def _attn_kernel(lam_ref, q_ref, k_ref, v_ref, km_ref, vm_ref, g_ref, o_ref, vt_ref, *, scale, seq):
    tq = ATTN_TQ
    f32, bf16 = jnp.float32, jnp.bfloat16
    vt_ref[...] = v_ref[...].astype(f32).T.astype(bf16)
    vm_t = vm_ref[...].astype(f32).T.astype(bf16)
    k_meta = km_ref[...]
    lane = lax.broadcasted_iota(jnp.int32, (tq, ATTN_VDIM), 1)
    key = lax.broadcasted_iota(jnp.int32, (tq, 2 * tq), 0)
    qry = lax.broadcasted_iota(jnp.int32, (tq, 2 * tq), 1) % tq
    diag_visible = qry // CHUNK >= key // CHUNK
    col_max = lambda a: jnp.max(a, axis=0, keepdims=True)
    col_sum = lambda a: jnp.sum(a, axis=0, keepdims=True)

    for i in range(seq // tq):
        rows = slice(i * tq, (i + 1) * tq)
        q = q_ref[rows, :] * scale
        zero = jnp.zeros_like(q)
        qq = jnp.concatenate([jnp.where(lane < ATTN_HEAD_DIM, q, zero),
                              jnp.where(lane >= ATTN_HEAD_DIM, q, zero)], axis=0)
        scores_t = lambda kb: lax.dot_general(kb, qq, (((1,), (1,)), ((), ())), preferred_element_type=f32)

        s_meta = scores_t(k_meta)
        s_diag = jnp.where(diag_visible, scores_t(k_ref[rows, :]), NEG_INF)
        m = jnp.maximum(col_max(s_meta), col_max(s_diag))
        if i > 0:
            s_prev = scores_t(k_ref[0:i * tq, :])
            m = jnp.maximum(m, col_max(s_prev))
        p_meta = jnp.exp(s_meta - m)
        p_diag = jnp.exp(s_diag - m)
        l = col_sum(p_meta) + col_sum(p_diag)
        acc = (jnp.dot(vm_t, p_meta.astype(bf16), preferred_element_type=f32)
               + jnp.dot(vt_ref[:, rows], p_diag.astype(bf16), preferred_element_type=f32))
        if i > 0:
            p_prev = jnp.exp(s_prev - m)
            l = l + col_sum(p_prev)
            acc = acc + jnp.dot(vt_ref[:, 0:i * tq], p_prev.astype(bf16), preferred_element_type=f32)

        o_all = acc / l
        o_t = o_all[:, :tq] - lam_ref[0] * o_all[:, tq:]
        o_t = o_t * lax.rsqrt(jnp.mean(o_t * o_t, axis=0, keepdims=True) + LN_EPS)
        o_ref[rows, :] = (o_t.T * g_ref[...]).astype(o_ref.dtype)


def _attention(lam, proj, proj_meta, g_scaled, nb, seq):
    assert seq % ATTN_TQ == 0
    nh = ATTN_HEADS
    first = GATE_COLS // ATTN_VDIM
    head = lambda part: pl.BlockSpec((None, seq, ATTN_VDIM), lambda b, h: (b, 0, first + part * nh + h))
    return pl.pallas_call(
        functools.partial(_attn_kernel, scale=ATTN_HEAD_DIM ** -0.5, seq=seq),
        grid=(nb, nh),
        in_specs=[
            pl.BlockSpec(memory_space=pltpu.SMEM),
            head(0), head(1), head(2),
            pl.BlockSpec((N_META, ATTN_VDIM), lambda b, h: (0, first + nh + h)),
            pl.BlockSpec((N_META, ATTN_VDIM), lambda b, h: (0, first + 2 * nh + h)),
            pl.BlockSpec((1, ATTN_VDIM), lambda b, h: (0, 0)),
        ],
        out_specs=pl.BlockSpec((None, seq, ATTN_VDIM), lambda b, h: (b, 0, h)),
        out_shape=jax.ShapeDtypeStruct((nb, seq, ATTN_WIDTH), jnp.bfloat16),
        scratch_shapes=[pltpu.VMEM((ATTN_VDIM, seq), jnp.bfloat16)],
        compiler_params=pltpu.CompilerParams(
            dimension_semantics=("parallel", "parallel"), vmem_limit_bytes=VMEM_LIMIT),
        name="attention",
    )(lam, proj, proj, proj, proj_meta, proj_meta, g_scaled)


def _merge_kernel(x_ref, gi_ref, bi_ref, gate_ref, ys_ref, ya_ref, wbs_ref, wba_ref, wo_ref,
                  g1_ref, b1_ref, wr_ref, br_ref, h1_ref, route_ref, cnt_ref):
    tm = x_ref.shape[0]

    @pl.when(pl.program_id(0) == 0)
    def _():
        cnt_ref[...] = jnp.zeros_like(cnt_ref)

    h = _layer_norm(x_ref[...], gi_ref[...], bi_ref[...])
    gates = gate_ref[...].astype(jnp.float32)
    ps = jnp.dot(ys_ref[...], wbs_ref[...], preferred_element_type=jnp.float32)
    pa = jnp.dot(ya_ref[...], wba_ref[...], preferred_element_type=jnp.float32)
    merged = gates[:, :D_MODEL] * ps + gates[:, D_MODEL:] * pa
    mix = jnp.dot(merged.astype(jnp.bfloat16), wo_ref[...], preferred_element_type=jnp.float32)
    h1 = _layer_norm(DEEPNORM_ALPHA * h + mix, g1_ref[...], b1_ref[...])
    h1_ref[...] = h1

    logits = jnp.dot(h1.astype(jnp.bfloat16), wr_ref[...], preferred_element_type=jnp.float32) + br_ref[...]
    lane_i = lax.broadcasted_iota(jnp.int32, logits.shape, 1)
    lane = lane_i.astype(jnp.float32)

    def first_argmax(vals):
        mx = jnp.max(vals, axis=-1, keepdims=True)
        idx = jnp.min(jnp.where(vals == mx, lane, float(LANES)), axis=-1, keepdims=True)
        return mx, idx

    is_group = jnp.logical_and(lane_i >= N_EXPERTS, lane_i < N_EXPERTS + N_EXPERT_GROUPS)
    gl = jnp.where(is_group, logits, NEG_INF)
    gmax, glane = first_argmax(gl)
    g_val = 1.0 / jnp.sum(jnp.where(is_group, jnp.exp(gl - gmax), 0.0), axis=-1, keepdims=True)
    g_idx = glane - float(N_EXPERTS)
    lane_group = (lane_i // EXPERTS_PER_GROUP).astype(jnp.float32)
    in_group = jnp.logical_and(lane_i < N_EXPERTS, lane_group == g_idx)
    el = jnp.where(in_group, logits, NEG_INF)
    m1, i1 = first_argmax(el)
    el2 = jnp.where(lane == i1, NEG_INF, el)
    m2, i2 = first_argmax(el2)
    e2 = jnp.exp(m2 - m1)
    w1 = g_val / (1.0 + e2)
    w2 = g_val * e2 / (1.0 + e2)

    onehot = jnp.logical_or(lane == i1, lane == i2)
    oh = jnp.where(onehot, 1.0, 0.0)
    r_i = lax.broadcasted_iota(jnp.int32, (tm, tm), 0)
    c_i = lax.broadcasted_iota(jnp.int32, (tm, tm), 1)
    tri = jnp.where(c_i < r_i, 1.0, 0.0).astype(jnp.bfloat16)
    before = jnp.dot(tri, oh.astype(jnp.bfloat16), preferred_element_type=jnp.float32) + cnt_ref[...]
    rank1 = jnp.sum(jnp.where(lane == i1, before, 0.0), axis=-1, keepdims=True)
    rank2 = jnp.sum(jnp.where(lane == i2, before, 0.0), axis=-1, keepdims=True)
    cnt_ref[...] = cnt_ref[...] + jnp.sum(oh, axis=0, keepdims=True)

    route = jnp.where(lane_i == 0, i1, 0.0)
    route = jnp.where(lane_i == 1, i2, route)
    route = jnp.where(lane_i == 2, w1, route)
    route = jnp.where(lane_i == 3, w2, route)
    route = jnp.where(lane_i == 4, rank1, route)
    route = jnp.where(lane_i == 5, rank2, route)
    route_ref[...] = route


def _merge(x2, gi, bi, gates, ys, ya, wbs, wba, wo, g1, b1, wr, br):
    t_rows = x2.shape[0]
    tm = MERGE_TM
    assert t_rows % tm == 0
    row = lambda i: (i, 0)
    const = lambda i: (0, 0)
    return pl.pallas_call(
        _merge_kernel,
        grid=(t_rows // tm,),
        in_specs=[
            pl.BlockSpec((tm, D_MODEL), row),
            pl.BlockSpec((1, D_MODEL), const),
            pl.BlockSpec((1, D_MODEL), const),
            pl.BlockSpec((tm, GATE_COLS), row),
            pl.BlockSpec((tm, SSM_WIDTH), row),
            pl.BlockSpec((tm, ATTN_WIDTH), row),
            pl.BlockSpec((SSM_WIDTH, D_MODEL), const),
            pl.BlockSpec((ATTN_WIDTH, D_MODEL), const),
            pl.BlockSpec((D_MODEL, D_MODEL), const),
            pl.BlockSpec((1, D_MODEL), const),
            pl.BlockSpec((1, D_MODEL), const),
            pl.BlockSpec((D_MODEL, LANES), const),
            pl.BlockSpec((1, LANES), const),
        ],
        out_specs=[
            pl.BlockSpec((tm, D_MODEL), row),
            pl.BlockSpec((tm, LANES), row),
            pl.BlockSpec((1, LANES), const),
        ],
        out_shape=[
            jax.ShapeDtypeStruct((t_rows, D_MODEL), jnp.float32),
            jax.ShapeDtypeStruct((t_rows, LANES), jnp.float32),
            jax.ShapeDtypeStruct((1, LANES), jnp.float32),
        ],
        compiler_params=pltpu.CompilerParams(
            dimension_semantics=("arbitrary",), vmem_limit_bytes=VMEM_LIMIT),
        name="merge",
    )(x2, gi, bi, gates, ys, ya, wbs, wba, wo, g1, b1, wr, br)


def _row_map_kernel(lo_ref, hi_ref, dest_ref, table_ref, *, t_rows):
    i = pl.program_id(0)
    tm = ROWMAP_TM
    stride = t_rows + ROW_BLOCK

    @pl.when(i == 0)
    def _():
        for e in range(N_EXPERTS + 1):
            def fill(r, carry):
                table_ref[r] = t_rows + (r & (ROW_BLOCK - 1))
                return carry
            lax.fori_loop(lo_ref[e], hi_ref[e], fill, 0)

    base = i * tm
    for a in range(TOP_K * tm):
        table_ref[dest_ref[0, a]] = base + (a // TOP_K + (a % TOP_K) * stride)


def _row_map(fill_lo, fill_hi, dest3, n_rows, t_rows):
    tm = ROWMAP_TM
    assert t_rows % tm == 0 and ROW_BLOCK & (ROW_BLOCK - 1) == 0
    grid_spec = pltpu.PrefetchScalarGridSpec(
        num_scalar_prefetch=2,
        grid=(t_rows // tm,),
        in_specs=[pl.BlockSpec((None, 1, TOP_K * tm), lambda i, lo, hi: (i, 0, 0), memory_space=pltpu.SMEM)],
        out_specs=pl.BlockSpec(memory_space=pltpu.SMEM),
    )
    return pl.pallas_call(
        functools.partial(_row_map_kernel, t_rows=t_rows),
        grid_spec=grid_spec,
        out_shape=jax.ShapeDtypeStruct((n_rows,), jnp.int32),
        compiler_params=pltpu.CompilerParams(dimension_semantics=("arbitrary",)),
        name="row_map",
    )(fill_lo, fill_hi, dest3)


def _moe_kernel(be_ref, nu_ref, src_cur_ref, src_next_ref, dst_prev_ref, h1_hbm, wg_ref, wu_ref, wd_ref,
                ysc_hbm, xbuf0, xbuf1, ybuf0, ybuf1, wgb, wub, wdb, gsem, ssem):
    i = pl.program_id(0)
    n_used = nu_ref[0]
    rb = ROW_BLOCK
    xbuf = (xbuf0, xbuf1)
    ybuf = (ybuf0, ybuf1)

    def start_gather(idx_ref, s):
        for r in range(rb):
            pltpu.make_async_copy(h1_hbm.at[pl.ds(idx_ref[0, r], 1)], xbuf[s].at[pl.ds(r, 1)],
                                  gsem.at[s]).start()

    def wait_gather(s):
        pltpu.make_async_copy(h1_hbm.at[pl.ds(0, rb)], xbuf[s], gsem.at[s]).wait()

    def start_scatter(s):
        for r in range(rb):
            pltpu.make_async_copy(ybuf[s].at[pl.ds(r, 1)], ysc_hbm.at[pl.ds(dst_prev_ref[0, r], 1)],
                                  ssem.at[0]).start()

    def wait_scatter(s):
        pltpu.make_async_copy(ybuf[s], ysc_hbm.at[pl.ds(0, rb)], ssem.at[0]).wait()

    @pl.when(i == 0)
    def _():
        ybuf1[...] = jnp.zeros_like(ybuf1)
        start_gather(src_cur_ref, 0)

    expert_changed = jnp.logical_or(i == 0, be_ref[i] != be_ref[jnp.maximum(i - 1, 0)])

    @pl.when(jnp.logical_and(i <= n_used, expert_changed))
    def _():
        wgb[...] = wg_ref[...].astype(jnp.bfloat16)
        wub[...] = wu_ref[...].astype(jnp.bfloat16)
        wdb[...] = wd_ref[...].astype(jnp.bfloat16)

    def step(slot):
        other = 1 - slot
        wait_gather(slot)

        @pl.when(i >= 1)
        def _():
            wait_scatter(slot)

        start_gather(src_next_ref, other)
        start_scatter(other)
        xb = xbuf[slot][...].astype(jnp.bfloat16)
        g = jnp.dot(xb, wgb[...], preferred_element_type=jnp.float32)
        u = jnp.dot(xb, wub[...], preferred_element_type=jnp.float32)
        hb = (jax.nn.silu(g) * u).astype(jnp.bfloat16)
        ybuf[slot][...] = jnp.dot(hb, wdb[...], preferred_element_type=jnp.float32)

        @pl.when(i == n_used)
        def _():
            wait_gather(other)
            wait_scatter(other)

    for parity in range(2):
        pl.when(jnp.logical_and(i <= n_used, i % 2 == parity))(functools.partial(step, parity))


def _moe(block_exp, n_used, src_blocks, dst_blocks, h1, wg, wu, wd, n_blocks):
    t_rows = h1.shape[0]
    rb = ROW_BLOCK
    wsel = lambda i, be, nu: (be[i], 0, 0)
    smem_blk = lambda off: pl.BlockSpec((None, 1, rb), lambda i, be, nu: (i + off, 0, 0),
                                        memory_space=pltpu.SMEM)
    grid_spec = pltpu.PrefetchScalarGridSpec(
        num_scalar_prefetch=2,
        grid=(n_blocks + 1,),
        in_specs=[
            smem_blk(0), smem_blk(1), smem_blk(0),
            pl.BlockSpec(memory_space=pl.ANY),
            pl.BlockSpec((None, D_MODEL, D_FF_EXPERT), wsel),
            pl.BlockSpec((None, D_MODEL, D_FF_EXPERT), wsel),
            pl.BlockSpec((None, D_FF_EXPERT, D_MODEL), wsel),
        ],
        out_specs=pl.BlockSpec(memory_space=pl.ANY),
        scratch_shapes=[
            pltpu.VMEM((rb, D_MODEL), jnp.float32),
            pltpu.VMEM((rb, D_MODEL), jnp.float32),
            pltpu.VMEM((rb, D_MODEL), jnp.float32),
            pltpu.VMEM((rb, D_MODEL), jnp.float32),
            pltpu.VMEM((D_MODEL, D_FF_EXPERT), jnp.bfloat16),
            pltpu.VMEM((D_MODEL, D_FF_EXPERT), jnp.bfloat16),
            pltpu.VMEM((D_FF_EXPERT, D_MODEL), jnp.bfloat16),
            pltpu.SemaphoreType.DMA((2,)),
            pltpu.SemaphoreType.DMA((1,)),
        ],
    )
    return pl.pallas_call(
        _moe_kernel,
        grid_spec=grid_spec,
        out_shape=jax.ShapeDtypeStruct((TOP_K * t_rows + rb, D_MODEL), jnp.float32),
        compiler_params=pltpu.CompilerParams(
            dimension_semantics=("arbitrary",), vmem_limit_bytes=VMEM_LIMIT),
        name="experts",
    )(block_exp, n_used, src_blocks, src_blocks, dst_blocks, h1, wg, wu, wd)


def _combine_kernel(h1_ref, route_ref, y0_ref, y1_ref, g_ref, b_ref, o_ref):
    route = route_ref[...]
    ffn = route[:, 2:3] * y0_ref[...] + route[:, 3:4] * y1_ref[...]
    o_ref[...] = _layer_norm(DEEPNORM_ALPHA * h1_ref[...] + ffn, g_ref[...], b_ref[...])


def _combine(h1, route, ysc, g2, b2):
    t_rows = h1.shape[0]
    tm = COMBINE_TM
    assert t_rows % tm == 0 and ROW_BLOCK % tm == 0
    second = (t_rows + ROW_BLOCK) // tm
    row = lambda i: (i, 0)
    const = lambda i: (0, 0)
    return pl.pallas_call(
        _combine_kernel,
        grid=(t_rows // tm,),
        in_specs=[
            pl.BlockSpec((tm, D_MODEL), row),
            pl.BlockSpec((tm, LANES), row),
            pl.BlockSpec((tm, D_MODEL), row),
            pl.BlockSpec((tm, D_MODEL), lambda i: (i + second, 0)),
            pl.BlockSpec((1, D_MODEL), const),
            pl.BlockSpec((1, D_MODEL), const),
        ],
        out_specs=pl.BlockSpec((tm, D_MODEL), row),
        out_shape=jax.ShapeDtypeStruct((t_rows, D_MODEL), jnp.float32),
        compiler_params=pltpu.CompilerParams(
            dimension_semantics=("parallel",), vmem_limit_bytes=VMEM_LIMIT),
        name="combine",
    )(h1, route, ysc, ysc, g2, b2)


def kernel(x, meta_tokens, ln_in_g, ln_in_b, w_in, ssm_a_re, ssm_a_im, ssm_log_dt, ssm_b_re, ssm_b_im,
           ssm_c_re, ssm_c_im, ssm_d, ssm_w_glu, attn_lambda_q1, attn_lambda_k1, attn_lambda_q2,
           attn_lambda_k2, attn_subln_g, w_br_ssm, w_br_attn, w_o, ln1_g, ln1_b, router_g_w, router_g_b,
           router_e_w, router_e_b, exp_w_gate, exp_w_up, exp_w_down, ln2_g, ln2_b):
    f32, bf16 = jnp.float32, jnp.bfloat16
    nb, seq, d = x.shape
    assert d == D_MODEL and w_in.shape[0] == DEPTH == 1
    t_rows = nb * seq
    l = 0
    lambda_init = 0.8 - 0.6 * math.exp(-0.3 * l)
    row2 = lambda v: v.reshape(1, -1).astype(f32)

    x2 = x.reshape(t_rows, d)
    w_u = w_in[l, :, :SSM_WIDTH].astype(bf16)
    w_rest = jnp.concatenate([w_in[l, :, SSM_WIDTH + QKV_COLS:], w_in[l, :, SSM_WIDTH:SSM_WIDTH + QKV_COLS]],
                             axis=1).astype(bf16)
    gi, bi = row2(ln_in_g), row2(ln_in_b)
    u, proj = _ln_proj(x2, gi, bi, w_u, w_rest, min(PROJ_TM, seq))
    u_meta, proj_meta = _ln_proj(meta_tokens.astype(f32), gi, bi, w_u, w_rest, N_META)

    bb, ab, cm = _ssm_params(ssm_a_re[l], ssm_a_im[l], ssm_log_dt[l], ssm_b_re[l], ssm_b_im[l],
                             ssm_c_re[l], ssm_c_im[l])
    u_tm = u.reshape(nb, seq, SSM_WIDTH).transpose(1, 0, 2).reshape(seq * nb, SSM_WIDTH)
    u_meta_rows = jnp.repeat(u_meta, nb, axis=0)
    y_ssm_tm = _ssm(u_meta_rows, u_tm, bb, ab, cm, row2(ssm_d[l]), ssm_w_glu[l].astype(bf16), nb)
    y_ssm = y_ssm_tm.reshape(seq, nb, SSM_WIDTH).transpose(1, 0, 2).reshape(t_rows, SSM_WIDTH)

    lam = (jnp.exp(jnp.sum(attn_lambda_q1[l].astype(f32) * attn_lambda_k1[l].astype(f32)))
           - jnp.exp(jnp.sum(attn_lambda_q2[l].astype(f32) * attn_lambda_k2[l].astype(f32)))
           + lambda_init).reshape(1)
    g_scaled = row2(attn_subln_g[l]) * (1.0 - lambda_init)
    y_attn = _attention(lam, proj.reshape(nb, seq, GATE_COLS + QKV_COLS), proj_meta, g_scaled, nb, seq)
    y_attn = y_attn.reshape(t_rows, ATTN_WIDTH)

    w_r = jnp.concatenate([router_e_w[l].astype(f32), router_g_w[l].astype(f32)], axis=1)
    w_r = jnp.pad(w_r, ((0, 0), (0, LANES - w_r.shape[1])))
    b_r = jnp.concatenate([router_e_b[l].astype(f32), router_g_b[l].astype(f32)])
    b_r = jnp.pad(b_r, (0, LANES - b_r.shape[0])).reshape(1, LANES)
    h1, route, cnt = _merge(x2, gi, bi, proj, y_ssm, y_attn, w_br_ssm[l].astype(bf16),
                            w_br_attn[l].astype(bf16), w_o[l].astype(bf16), row2(ln1_g[l]), row2(ln1_b[l]),
                            w_r.astype(bf16), b_r)

    counts = cnt[0, :N_EXPERTS].astype(jnp.int32)
    padded = (counts + ROW_BLOCK - 1) // ROW_BLOCK * ROW_BLOCK
    pad_ends = jnp.cumsum(padded)
    pad_starts = pad_ends - padded
    expert = route[:, 0:TOP_K].astype(jnp.int32)
    rank = route[:, 4:4 + TOP_K].astype(jnp.int32)
    dest = pad_starts[expert] + rank
    n_blocks = -(-(t_rows * TOP_K + N_EXPERTS * (ROW_BLOCK - 1)) // ROW_BLOCK)
    block_start = jnp.arange(n_blocks, dtype=jnp.int32) * ROW_BLOCK
    block_exp = jnp.minimum(jnp.sum(pad_ends[None, :] <= block_start[:, None], axis=1),
                            N_EXPERTS - 1).astype(jnp.int32)
    block_exp = jnp.concatenate([block_exp, block_exp[-1:]])
    n_used = (pad_ends[-1:] // ROW_BLOCK).astype(jnp.int32)

    rb = ROW_BLOCK
    stride = t_rows + rb
    fill_lo = jnp.concatenate([pad_starts + counts, pad_ends[-1:]]).astype(jnp.int32)
    fill_hi = jnp.concatenate([pad_ends, jnp.full((1,), n_blocks * rb)]).astype(jnp.int32)
    row_map = _row_map(fill_lo, fill_hi, dest.reshape(t_rows // ROWMAP_TM, 1, TOP_K * ROWMAP_TM),
                       n_blocks * rb, t_rows)
    dst_blocks = jnp.concatenate([t_rows + jnp.arange(rb, dtype=jnp.int32), row_map]).reshape(
        n_blocks + 1, 1, rb)
    src = jnp.minimum(row_map % stride, t_rows - 1)
    src_blocks = jnp.concatenate([src, jnp.zeros((2 * rb,), jnp.int32)]).reshape(n_blocks + 2, 1, rb)

    ysc = _moe(block_exp, n_used, src_blocks, dst_blocks, h1, exp_w_gate[l], exp_w_up[l], exp_w_down[l],
               n_blocks)
    out = _combine(h1, route, ysc, row2(ln2_g[l]), row2(ln2_b[l]))
    return out.reshape(nb, seq, d).astype(x.dtype)
```

```python
import functools
import math

import jax
import jax.numpy as jnp
from jax import lax
from jax.experimental import pallas as pl
from jax.experimental.pallas import tpu as pltpu

D_MODEL = 2048
N_META = 16
CHUNK = 64
ATTN_HEADS = 8
ATTN_HEAD_DIM = 64
ATTN_VDIM = 2 * ATTN_HEAD_DIM
ATTN_WIDTH = ATTN_HEADS * ATTN_VDIM
SSM_WIDTH = D_MODEL // 4
SSM_GROUP = 16
SSM_GROUPS = SSM_WIDTH // SSM_GROUP
SSM_STATE = 64
SSM_COLS = SSM_GROUPS * SSM_STATE
N_EXPERT_GROUPS = 4
EXPERTS_PER_GROUP = 8
N_EXPERTS = N_EXPERT_GROUPS * EXPERTS_PER_GROUP
TOP_K = 2
D_FF_EXPERT = D_MODEL // 4
QKV_COLS = 3 * ATTN_WIDTH
GATE_COLS = 2 * D_MODEL
IN_COLS = SSM_WIDTH + QKV_COLS + GATE_COLS
LN_EPS = 1e-5
DEPTH = 1
DEEPNORM_ALPHA = (2.0 * DEPTH) ** 0.25
LANES = 128
NEG_INF = -1e30

PROJ_TN = 1024
PROJ_TM = 1024
SSM_TC = 32
SSM_CB = 512
SSM_HALVES = 2
ATTN_TQ = 256
MERGE_TM = 256
ROW_BLOCK = 256
COMBINE_TM = 256
ROWMAP_TM = 512
VMEM_LIMIT = 56 * 1024 * 1024


def _pack_halves(x):
    n = x.shape[1] // 2
    bits = lambda v: lax.bitcast_convert_type(v.astype(jnp.bfloat16).astype(jnp.float32), jnp.uint32)
    return bits(x[:, :n]) | (bits(x[:, n:]) >> 16)


def _unpack_halves(w):
    hi = lax.bitcast_convert_type(w & jnp.uint32(0xFFFF0000), jnp.float32)
    lo = lax.bitcast_convert_type(w << 16, jnp.float32)
    return hi, lo


def _layer_norm(x, g, b):
    mu = jnp.mean(x, axis=-1, keepdims=True)
    xc = x - mu
    var = jnp.mean(xc * xc, axis=-1, keepdims=True)
    return xc * lax.rsqrt(var + LN_EPS) * g + b


def _ln_proj_kernel(x_ref, g_ref, b_ref, wu_ref, w_ref, u_ref, proj_ref, xn_ref):
    j = pl.program_id(1)

    @pl.when(j == 0)
    def _():
        xn = _layer_norm(x_ref[...], g_ref[...], b_ref[...]).astype(jnp.bfloat16)
        xn_ref[...] = xn
        u_ref[...] = jnp.dot(xn, wu_ref[...], preferred_element_type=jnp.float32)

    acc = jnp.dot(xn_ref[...], w_ref[...], preferred_element_type=jnp.float32)
    proj_ref[...] = jnp.where(j < GATE_COLS // PROJ_TN, jax.nn.sigmoid(acc), acc).astype(jnp.bfloat16)


def _ln_proj(x2, g, b, w_u, w_rest, tm):
    t_rows = x2.shape[0]
    n_col = (GATE_COLS + QKV_COLS) // PROJ_TN
    assert QKV_COLS % PROJ_TN == 0 and GATE_COLS % PROJ_TN == 0 and t_rows % tm == 0
    return pl.pallas_call(
        _ln_proj_kernel,
        grid=(t_rows // tm, n_col),
        in_specs=[
            pl.BlockSpec((tm, D_MODEL), lambda i, j: (i, 0)),
            pl.BlockSpec((1, D_MODEL), lambda i, j: (0, 0)),
            pl.BlockSpec((1, D_MODEL), lambda i, j: (0, 0)),
            pl.BlockSpec((D_MODEL, SSM_WIDTH), lambda i, j: (0, 0)),
            pl.BlockSpec((D_MODEL, PROJ_TN), lambda i, j: (0, j)),
        ],
        out_specs=[
            pl.BlockSpec((tm, SSM_WIDTH), lambda i, j: (i, 0)),
            pl.BlockSpec((tm, PROJ_TN), lambda i, j: (i, j)),
        ],
        out_shape=[
            jax.ShapeDtypeStruct((t_rows, SSM_WIDTH), jnp.float32),
            jax.ShapeDtypeStruct((t_rows, GATE_COLS + QKV_COLS), jnp.bfloat16),
        ],
        scratch_shapes=[pltpu.VMEM((tm, D_MODEL), jnp.bfloat16)],
        compiler_params=pltpu.CompilerParams(
            dimension_semantics=("parallel", "arbitrary"), vmem_limit_bytes=VMEM_LIMIT),
        name="ln_proj",
    )(x2, g, b, w_u, w_rest)


def _ssm_kernel(um_ref, u_ref, bb_ref, ab_ref, cm_ref, d_ref, wglu_ref, y_ref, state_ref, bu_ref, *, nb):
    half_u = SSM_WIDTH // SSM_HALVES
    half_c = SSM_COLS // SSM_HALVES

    def expand_and_scan(u_bf16, n_steps):
        rows = n_steps * nb
        for h in range(SSM_HALVES):
            bu_ref[pl.ds(0, rows), 2 * half_c * h:2 * half_c * (h + 1)] = jnp.dot(
                u_bf16[:, half_u * h:half_u * (h + 1)], bb_ref[h], preferred_element_type=jnp.float32)
        for cb in range(SSM_COLS // SSM_CB):
            h, off = divmod(cb * SSM_CB, half_c)
            re_cols = pl.ds(2 * half_c * h + off, SSM_CB)
            im_cols = pl.ds(2 * half_c * h + half_c + off, SSM_CB)
            a_re = ab_ref[0:1, cb * SSM_CB:(cb + 1) * SSM_CB]
            a_im = ab_ref[1:2, cb * SSM_CB:(cb + 1) * SSM_CB]

            def step(t, carry):
                s_re, s_im = carry
                r0 = pl.multiple_of(t * nb, nb)
                n_re = a_re * s_re - a_im * s_im + bu_ref[pl.ds(r0, nb), re_cols]
                n_im = a_re * s_im + a_im * s_re + bu_ref[pl.ds(r0, nb), im_cols]
                bu_ref[pl.ds(r0, nb), re_cols] = n_re
                bu_ref[pl.ds(r0, nb), im_cols] = n_im
                return n_re, n_im

            s_re, s_im = lax.fori_loop(0, n_steps, step, (state_ref[:, re_cols], state_ref[:, im_cols]))
            state_ref[:, re_cols] = s_re
            state_ref[:, im_cols] = s_im

    @pl.when(pl.program_id(0) == 0)
    def _():
        state_ref[...] = jnp.zeros_like(state_ref)
        expand_and_scan(um_ref[...].astype(jnp.bfloat16), N_META)

    u = u_ref[...]
    expand_and_scan(u.astype(jnp.bfloat16), SSM_TC)
    y = jnp.concatenate(
        [jnp.dot(bu_ref[:, 2 * half_c * h:2 * half_c * (h + 1)].astype(jnp.bfloat16), cm_ref[h],
                 preferred_element_type=jnp.float32) for h in range(SSM_HALVES)], axis=1)
    y = jax.nn.gelu(y + d_ref[...] * u)
    gate = jnp.dot(y.astype(jnp.bfloat16), wglu_ref[...], preferred_element_type=jnp.float32)
    y_ref[...] = (y * jax.nn.sigmoid(gate)).astype(y_ref.dtype)


def _ssm(u_meta_rows, u_tm, bb, ab, cm, d, wglu, nb):
    rows = u_tm.shape[0]
    blk = SSM_TC * nb
    assert rows % blk == 0 and nb % 8 == 0 and N_META <= SSM_TC
    const = lambda i: (0, 0)
    return pl.pallas_call(
        functools.partial(_ssm_kernel, nb=nb),
        grid=(rows // blk,),
        in_specs=[
            pl.BlockSpec((N_META * nb, SSM_WIDTH), const),
            pl.BlockSpec((blk, SSM_WIDTH), lambda i: (i, 0)),
            pl.BlockSpec((SSM_HALVES, SSM_WIDTH // SSM_HALVES, 2 * SSM_COLS // SSM_HALVES), lambda i: (0, 0, 0)),
            pl.BlockSpec((2, SSM_COLS), const),
            pl.BlockSpec((SSM_HALVES, 2 * SSM_COLS // SSM_HALVES, SSM_WIDTH // SSM_HALVES), lambda i: (0, 0, 0)),
            pl.BlockSpec((1, SSM_WIDTH), const),
            pl.BlockSpec((SSM_WIDTH, SSM_WIDTH), const),
        ],
        out_specs=pl.BlockSpec((blk, SSM_WIDTH), lambda i: (i, 0)),
        out_shape=jax.ShapeDtypeStruct((rows, SSM_WIDTH), jnp.bfloat16),
        scratch_shapes=[
            pltpu.VMEM((nb, 2 * SSM_COLS), jnp.float32),
            pltpu.VMEM((blk, 2 * SSM_COLS), jnp.float32),
        ],
        compiler_params=pltpu.CompilerParams(
            dimension_semantics=("arbitrary",), vmem_limit_bytes=VMEM_LIMIT),
        name="ssm",
    )(u_meta_rows, u_tm, bb, ab, cm, d, wglu)


def _ssm_params(a_re, a_im, log_dt, b_re, b_im, c_re, c_im):
    f32 = jnp.float32
    lam_re = jnp.minimum(a_re.astype(f32), -1e-4)
    lam_im = a_im.astype(f32)
    dt = jnp.exp(log_dt.astype(f32))[:, None]
    mag = jnp.exp(lam_re * dt)
    ab_re = mag * jnp.cos(lam_im * dt)
    ab_im = mag * jnp.sin(lam_im * dt)
    den = lam_re * lam_re + lam_im * lam_im
    nr = ab_re - 1.0
    ni = ab_im
    z_re = (nr * lam_re + ni * lam_im) / den
    z_im = (ni * lam_re - nr * lam_im) / den
    br32 = b_re.astype(f32)
    bi32 = b_im.astype(f32)
    bb_re = z_re[..., None] * br32 - z_im[..., None] * bi32
    bb_im = z_re[..., None] * bi32 + z_im[..., None] * br32
    gh = SSM_GROUPS // SSM_HALVES
    eye = jnp.eye(gh, dtype=f32)
    split = lambda a: a.reshape((SSM_HALVES, gh) + a.shape[1:])
    exp_re = jnp.einsum('kgpc,gh->kgchp', split(bb_re), eye).reshape(SSM_HALVES, gh * SSM_GROUP, gh * SSM_STATE)
    exp_im = jnp.einsum('kgpc,gh->kgchp', split(bb_im), eye).reshape(SSM_HALVES, gh * SSM_GROUP, gh * SSM_STATE)
    bb = jnp.concatenate([exp_re, exp_im], axis=2)
    ro_re = jnp.einsum('kgcp,gh->kgphc', split(c_re.astype(f32)), eye).reshape(
        SSM_HALVES, gh * SSM_STATE, gh * SSM_GROUP)
    ro_im = jnp.einsum('kgcp,gh->kgphc', split(c_im.astype(f32)), eye).reshape(
        SSM_HALVES, gh * SSM_STATE, gh * SSM_GROUP)
    cm = jnp.concatenate([ro_re, -ro_im], axis=1)
    ab = jnp.stack([ab_re.reshape(SSM_COLS), ab_im.reshape(SSM_COLS)], axis=0)
    return bb.astype(jnp.bfloat16), ab, cm.astype(jnp.bfloat16)


def _attn_kernel(lam_ref, q_ref, k_ref, v_ref, km_ref, vm_ref, g_ref, o_ref, vt_ref, *, scale, seq):
    tq = ATTN_TQ
    f32, bf16 = jnp.float32, jnp.bfloat16
    vt_ref[...] = v_ref[...].astype(f32).T.astype(bf16)
    vm_t = vm_ref[...].astype(f32).T.astype(bf16)
    k_meta = km_ref[...]
    lane = lax.broadcasted_iota(jnp.int32, (tq, ATTN_VDIM), 1)
    key = lax.broadcasted_iota(jnp.int32, (tq, 2 * tq), 0)
    qry = lax.broadcasted_iota(jnp.int32, (tq, 2 * tq), 1) % tq
    diag_visible = qry // CHUNK >= key // CHUNK
    col_max = lambda a: jnp.max(a, axis=0, keepdims=True)
    col_sum = lambda a: jnp.sum(a, axis=0, keepdims=True)

    for i in range(seq // tq):
        rows = slice(i * tq, (i + 1) * tq)
        q = q_ref[rows, :] * scale
        zero = jnp.zeros_like(q)
        qq = jnp.concatenate([jnp.where(lane < ATTN_HEAD_DIM, q, zero),
                              jnp.where(lane >= ATTN_HEAD_DIM, q, zero)], axis=0)
        scores_t = lambda kb: lax.dot_general(kb, qq, (((1,), (1,)), ((), ())), preferred_element_type=f32)

        s_meta = scores_t(k_meta)
        s_diag = jnp.where(diag_visible, scores_t(k_ref[rows, :]), NEG_INF)
        m = jnp.maximum(col_max(s_meta), col_max(s_diag))
        if i > 0:
            s_prev = scores_t(k_ref[0:i * tq, :])
            m = jnp.maximum(m, col_max(s_prev))
        p_meta = jnp.exp(s_meta - m)
        p_diag = jnp.exp(s_diag - m)
        l = col_sum(p_meta) + col_sum(p_diag)
        acc = (jnp.dot(vm_t, p_meta.astype(bf16), preferred_element_type=f32)
               + jnp.dot(vt_ref[:, rows], p_diag.astype(bf16), preferred_element_type=f32))
        if i > 0:
            p_prev = jnp.exp(s_prev - m)
            l = l + col_sum(p_prev)
            acc = acc + jnp.dot(vt_ref[:, 0:i * tq], p_prev.astype(bf16), preferred_element_type=f32)

        o_all = acc / l
        o_t = o_all[:, :tq] - lam_ref[0] * o_all[:, tq:]
        o_t = o_t * lax.rsqrt(jnp.mean(o_t * o_t, axis=0, keepdims=True) + LN_EPS)
        o_ref[rows, :] = (o_t.T * g_ref[...]).astype(o_ref.dtype)


def _attention(lam, proj, proj_meta, g_scaled, nb, seq):
    assert seq % ATTN_TQ == 0
    nh = ATTN_HEADS
    first = GATE_COLS // ATTN_VDIM
    head = lambda part: pl.BlockSpec((None, seq, ATTN_VDIM), lambda b, h: (b, 0, first + part * nh + h))
    return pl.pallas_call(
        functools.partial(_attn_kernel, scale=ATTN_HEAD_DIM ** -0.5, seq=seq),
        grid=(nb, nh),
        in_specs=[
            pl.BlockSpec(memory_space=pltpu.SMEM),
            head(0), head(1), head(2),
            pl.BlockSpec((N_META, ATTN_VDIM), lambda b, h: (0, first + nh + h)),
            pl.BlockSpec((N_META, ATTN_VDIM), lambda b, h: (0, first + 2 * nh + h)),
            pl.BlockSpec((1, ATTN_VDIM), lambda b, h: (0, 0)),
        ],
        out_specs=pl.BlockSpec((None, seq, ATTN_VDIM), lambda b, h: (b, 0, h)),
        out_shape=jax.ShapeDtypeStruct((nb, seq, ATTN_WIDTH), jnp.bfloat16),
        scratch_shapes=[pltpu.VMEM((ATTN_VDIM, seq), jnp.bfloat16)],
        compiler_params=pltpu.CompilerParams(
            dimension_semantics=("parallel", "parallel"), vmem_limit_bytes=VMEM_LIMIT),
        name="attention",
    )(lam, proj, proj, proj, proj_meta, proj_meta, g_scaled)


def _merge_kernel(x_ref, gi_ref, bi_ref, gate_ref, ys_ref, ya_ref, wbs_ref, wba_ref, wo_ref,
                  g1_ref, b1_ref, wr_ref, br_ref, h1_ref, h1p_ref, route_ref, cnt_ref):
    tm = x_ref.shape[0]

    @pl.when(pl.program_id(0) == 0)
    def _():
        cnt_ref[...] = jnp.zeros_like(cnt_ref)

    h = _layer_norm(x_ref[...], gi_ref[...], bi_ref[...])
    gates = gate_ref[...].astype(jnp.float32)
    ps = jnp.dot(ys_ref[...], wbs_ref[...], preferred_element_type=jnp.float32)
    pa = jnp.dot(ya_ref[...], wba_ref[...], preferred_element_type=jnp.float32)
    merged = gates[:, :D_MODEL] * ps + gates[:, D_MODEL:] * pa
    mix = jnp.dot(merged.astype(jnp.bfloat16), wo_ref[...], preferred_element_type=jnp.float32)
    h1 = _layer_norm(DEEPNORM_ALPHA * h + mix, g1_ref[...], b1_ref[...])
    h1_ref[...] = h1
    h1p_ref[...] = _pack_halves(h1)

    logits = jnp.dot(h1.astype(jnp.bfloat16), wr_ref[...], preferred_element_type=jnp.float32) + br_ref[...]
    lane_i = lax.broadcasted_iota(jnp.int32, logits.shape, 1)
    lane = lane_i.astype(jnp.float32)

    def first_argmax(vals):
        mx = jnp.max(vals, axis=-1, keepdims=True)
        idx = jnp.min(jnp.where(vals == mx, lane, float(LANES)), axis=-1, keepdims=True)
        return mx, idx

    is_group = jnp.logical_and(lane_i >= N_EXPERTS, lane_i < N_EXPERTS + N_EXPERT_GROUPS)
    gl = jnp.where(is_group, logits, NEG_INF)
    gmax, glane = first_argmax(gl)
    g_val = 1.0 / jnp.sum(jnp.where(is_group, jnp.exp(gl - gmax), 0.0), axis=-1, keepdims=True)
    g_idx = glane - float(N_EXPERTS)
    lane_group = (lane_i // EXPERTS_PER_GROUP).astype(jnp.float32)
    in_group = jnp.logical_and(lane_i < N_EXPERTS, lane_group == g_idx)
    el = jnp.where(in_group, logits, NEG_INF)
    m1, i1 = first_argmax(el)
    el2 = jnp.where(lane == i1, NEG_INF, el)
    m2, i2 = first_argmax(el2)
    e2 = jnp.exp(m2 - m1)
    w1 = g_val / (1.0 + e2)
    w2 = g_val * e2 / (1.0 + e2)

    onehot = jnp.logical_or(lane == i1, lane == i2)
    oh = jnp.where(onehot, 1.0, 0.0)
    r_i = lax.broadcasted_iota(jnp.int32, (tm, tm), 0)
    c_i = lax.broadcasted_iota(jnp.int32, (tm, tm), 1)
    tri = jnp.where(c_i < r_i, 1.0, 0.0).astype(jnp.bfloat16)
    before = jnp.dot(tri, oh.astype(jnp.bfloat16), preferred_element_type=jnp.float32) + cnt_ref[...]
    rank1 = jnp.sum(jnp.where(lane == i1, before, 0.0), axis=-1, keepdims=True)
    rank2 = jnp.sum(jnp.where(lane == i2, before, 0.0), axis=-1, keepdims=True)
    cnt_ref[...] = cnt_ref[...] + jnp.sum(oh, axis=0, keepdims=True)

    route = jnp.where(lane_i == 0, i1, 0.0)
    route = jnp.where(lane_i == 1, i2, route)
    route = jnp.where(lane_i == 2, w1, route)
    route = jnp.where(lane_i == 3, w2, route)
    route = jnp.where(lane_i == 4, rank1, route)
    route = jnp.where(lane_i == 5, rank2, route)
    route_ref[...] = route


def _merge(x2, gi, bi, gates, ys, ya, wbs, wba, wo, g1, b1, wr, br):
    t_rows = x2.shape[0]
    tm = MERGE_TM
    assert t_rows % tm == 0
    row = lambda i: (i, 0)
    const = lambda i: (0, 0)
    return pl.pallas_call(
        _merge_kernel,
        grid=(t_rows // tm,),
        in_specs=[
            pl.BlockSpec((tm, D_MODEL), row),
            pl.BlockSpec((1, D_MODEL), const),
            pl.BlockSpec((1, D_MODEL), const),
            pl.BlockSpec((tm, GATE_COLS), row),
            pl.BlockSpec((tm, SSM_WIDTH), row),
            pl.BlockSpec((tm, ATTN_WIDTH), row),
            pl.BlockSpec((SSM_WIDTH, D_MODEL), const),
            pl.BlockSpec((ATTN_WIDTH, D_MODEL), const),
            pl.BlockSpec((D_MODEL, D_MODEL), const),
            pl.BlockSpec((1, D_MODEL), const),
            pl.BlockSpec((1, D_MODEL), const),
            pl.BlockSpec((D_MODEL, LANES), const),
            pl.BlockSpec((1, LANES), const),
        ],
        out_specs=[
            pl.BlockSpec((tm, D_MODEL), row),
            pl.BlockSpec((tm, D_MODEL // 2), row),
            pl.BlockSpec((tm, LANES), row),
            pl.BlockSpec((1, LANES), const),
        ],
        out_shape=[
            jax.ShapeDtypeStruct((t_rows, D_MODEL), jnp.float32),
            jax.ShapeDtypeStruct((t_rows, D_MODEL // 2), jnp.uint32),
            jax.ShapeDtypeStruct((t_rows, LANES), jnp.float32),
            jax.ShapeDtypeStruct((1, LANES), jnp.float32),
        ],
        compiler_params=pltpu.CompilerParams(
            dimension_semantics=("arbitrary",), vmem_limit_bytes=VMEM_LIMIT),
        name="merge",
    )(x2, gi, bi, gates, ys, ya, wbs, wba, wo, g1, b1, wr, br)


def _row_map_kernel(lo_ref, hi_ref, dest_ref, table_ref, *, t_rows):
    i = pl.program_id(0)
    tm = ROWMAP_TM
    stride = t_rows + ROW_BLOCK

    @pl.when(i == 0)
    def _():
        for e in range(N_EXPERTS + 1):
            def fill(r, carry):
                table_ref[r] = t_rows + (r & (ROW_BLOCK - 1))
                return carry
            lax.fori_loop(lo_ref[e], hi_ref[e], fill, 0)

    base = i * tm
    for a in range(TOP_K * tm):
        table_ref[dest_ref[0, a]] = base + (a // TOP_K + (a % TOP_K) * stride)


def _row_map(fill_lo, fill_hi, dest3, n_rows, t_rows):
    tm = ROWMAP_TM
    assert t_rows % tm == 0 and ROW_BLOCK & (ROW_BLOCK - 1) == 0
    grid_spec = pltpu.PrefetchScalarGridSpec(
        num_scalar_prefetch=2,
        grid=(t_rows // tm,),
        in_specs=[pl.BlockSpec((None, 1, TOP_K * tm), lambda i, lo, hi: (i, 0, 0), memory_space=pltpu.SMEM)],
        out_specs=pl.BlockSpec(memory_space=pltpu.SMEM),
    )
    return pl.pallas_call(
        functools.partial(_row_map_kernel, t_rows=t_rows),
        grid_spec=grid_spec,
        out_shape=jax.ShapeDtypeStruct((n_rows,), jnp.int32),
        compiler_params=pltpu.CompilerParams(dimension_semantics=("arbitrary",)),
        name="row_map",
    )(fill_lo, fill_hi, dest3)


def _moe_kernel(be_ref, nu_ref, src_cur_ref, src_next_ref, dst_prev_ref, h1_hbm, wg_ref, wu_ref, wd_ref,
                ysc_hbm, xbuf0, xbuf1, ybuf0, ybuf1, wgb, wub, wdb, gsem, ssem):
    i = pl.program_id(0)
    n_used = nu_ref[0]
    rb = ROW_BLOCK
    xbuf = (xbuf0, xbuf1)
    ybuf = (ybuf0, ybuf1)

    def start_gather(idx_ref, s):
        for r in range(rb):
            pltpu.make_async_copy(h1_hbm.at[pl.ds(idx_ref[0, r], 1)], xbuf[s].at[pl.ds(r, 1)],
                                  gsem.at[s]).start(priority=r % 2)

    def wait_gather(s):
        pltpu.make_async_copy(h1_hbm.at[pl.ds(0, rb)], xbuf[s], gsem.at[s]).wait()

    def start_scatter(s):
        for r in range(rb):
            pltpu.make_async_copy(ybuf[s].at[pl.ds(r, 1)], ysc_hbm.at[pl.ds(dst_prev_ref[0, r], 1)],
                                  ssem.at[0]).start(priority=r % 2)

    def wait_scatter(s):
        pltpu.make_async_copy(ybuf[s], ysc_hbm.at[pl.ds(0, rb)], ssem.at[0]).wait()

    @pl.when(i == 0)
    def _():
        ybuf1[...] = jnp.zeros_like(ybuf1)
        start_gather(src_cur_ref, 0)

    expert_changed = jnp.logical_or(i == 0, be_ref[i] != be_ref[jnp.maximum(i - 1, 0)])

    @pl.when(jnp.logical_and(i <= n_used, expert_changed))
    def _():
        wgb[...] = wg_ref[...].astype(jnp.bfloat16)
        wub[...] = wu_ref[...].astype(jnp.bfloat16)
        wdb[...] = wd_ref[...].astype(jnp.bfloat16)

    def step(slot):
        other = 1 - slot
        wait_gather(slot)

        @pl.when(i >= 1)
        def _():
            wait_scatter(slot)

        start_gather(src_next_ref, other)
        start_scatter(other)
        xb = jnp.concatenate([half.astype(jnp.bfloat16) for half in _unpack_halves(xbuf[slot][...])], axis=1)
        g = jnp.dot(xb, wgb[...], preferred_element_type=jnp.float32)
        u = jnp.dot(xb, wub[...], preferred_element_type=jnp.float32)
        hb = (jax.nn.silu(g) * u).astype(jnp.bfloat16)
        ybuf[slot][...] = _pack_halves(jnp.dot(hb, wdb[...], preferred_element_type=jnp.float32))

        @pl.when(i == n_used)
        def _():
            wait_gather(other)
            wait_scatter(other)

    for parity in range(2):
        pl.when(jnp.logical_and(i <= n_used, i % 2 == parity))(functools.partial(step, parity))


def _moe(block_exp, n_used, src_blocks, dst_blocks, h1, wg, wu, wd, n_blocks):
    t_rows = h1.shape[0]
    rb = ROW_BLOCK
    wsel = lambda i, be, nu: (be[i], 0, 0)
    smem_blk = lambda off: pl.BlockSpec((None, 1, rb), lambda i, be, nu: (i + off, 0, 0),
                                        memory_space=pltpu.SMEM)
    grid_spec = pltpu.PrefetchScalarGridSpec(
        num_scalar_prefetch=2,
        grid=(n_blocks + 1,),
        in_specs=[
            smem_blk(0), smem_blk(1), smem_blk(0),
            pl.BlockSpec(memory_space=pl.ANY),
            pl.BlockSpec((None, D_MODEL, D_FF_EXPERT), wsel),
            pl.BlockSpec((None, D_MODEL, D_FF_EXPERT), wsel),
            pl.BlockSpec((None, D_FF_EXPERT, D_MODEL), wsel),
        ],
        out_specs=pl.BlockSpec(memory_space=pl.ANY),
        scratch_shapes=[
            pltpu.VMEM((rb, D_MODEL // 2), jnp.uint32),
            pltpu.VMEM((rb, D_MODEL // 2), jnp.uint32),
            pltpu.VMEM((rb, D_MODEL // 2), jnp.uint32),
            pltpu.VMEM((rb, D_MODEL // 2), jnp.uint32),
            pltpu.VMEM((D_MODEL, D_FF_EXPERT), jnp.bfloat16),
            pltpu.VMEM((D_MODEL, D_FF_EXPERT), jnp.bfloat16),
            pltpu.VMEM((D_FF_EXPERT, D_MODEL), jnp.bfloat16),
            pltpu.SemaphoreType.DMA((2,)),
            pltpu.SemaphoreType.DMA((1,)),
        ],
    )
    return pl.pallas_call(
        _moe_kernel,
        grid_spec=grid_spec,
        out_shape=jax.ShapeDtypeStruct((TOP_K * t_rows + rb, D_MODEL // 2), jnp.uint32),
        compiler_params=pltpu.CompilerParams(
            dimension_semantics=("arbitrary",), vmem_limit_bytes=VMEM_LIMIT),
        name="experts",
    )(block_exp, n_used, src_blocks, src_blocks, dst_blocks, h1, wg, wu, wd)


def _combine_kernel(h1_ref, route_ref, y0_ref, y1_ref, g_ref, b_ref, o_ref):
    route = route_ref[...]
    y0 = _unpack_halves(y0_ref[...])
    y1 = _unpack_halves(y1_ref[...])
    ffn = jnp.concatenate([route[:, 2:3] * a + route[:, 3:4] * b for a, b in zip(y0, y1)], axis=1)
    o_ref[...] = _layer_norm(DEEPNORM_ALPHA * h1_ref[...] + ffn, g_ref[...], b_ref[...])


def _combine(h1, route, ysc, g2, b2):
    t_rows = h1.shape[0]
    tm = COMBINE_TM
    assert t_rows % tm == 0 and ROW_BLOCK % tm == 0
    second = (t_rows + ROW_BLOCK) // tm
    row = lambda i: (i, 0)
    const = lambda i: (0, 0)
    return pl.pallas_call(
        _combine_kernel,
        grid=(t_rows // tm,),
        in_specs=[
            pl.BlockSpec((tm, D_MODEL), row),
            pl.BlockSpec((tm, LANES), row),
            pl.BlockSpec((tm, D_MODEL // 2), row),
            pl.BlockSpec((tm, D_MODEL // 2), lambda i: (i + second, 0)),
            pl.BlockSpec((1, D_MODEL), const),
            pl.BlockSpec((1, D_MODEL), const),
        ],
        out_specs=pl.BlockSpec((tm, D_MODEL), row),
        out_shape=jax.ShapeDtypeStruct((t_rows, D_MODEL), jnp.float32),
        compiler_params=pltpu.CompilerParams(
            dimension_semantics=("parallel",), vmem_limit_bytes=VMEM_LIMIT),
        name="combine",
    )(h1, route, ysc, ysc, g2, b2)


def kernel(x, meta_tokens, ln_in_g, ln_in_b, w_in, ssm_a_re, ssm_a_im, ssm_log_dt, ssm_b_re, ssm_b_im,
           ssm_c_re, ssm_c_im, ssm_d, ssm_w_glu, attn_lambda_q1, attn_lambda_k1, attn_lambda_q2,
           attn_lambda_k2, attn_subln_g, w_br_ssm, w_br_attn, w_o, ln1_g, ln1_b, router_g_w, router_g_b,
           router_e_w, router_e_b, exp_w_gate, exp_w_up, exp_w_down, ln2_g, ln2_b):
    f32, bf16 = jnp.float32, jnp.bfloat16
    nb, seq, d = x.shape
    assert d == D_MODEL and w_in.shape[0] == DEPTH == 1
    t_rows = nb * seq
    l = 0
    lambda_init = 0.8 - 0.6 * math.exp(-0.3 * l)
    row2 = lambda v: v.reshape(1, -1).astype(f32)

    x2 = x.reshape(t_rows, d)
    w_u = w_in[l, :, :SSM_WIDTH].astype(bf16)
    w_rest = jnp.concatenate([w_in[l, :, SSM_WIDTH + QKV_COLS:], w_in[l, :, SSM_WIDTH:SSM_WIDTH + QKV_COLS]],
                             axis=1).astype(bf16)
    gi, bi = row2(ln_in_g), row2(ln_in_b)
    u, proj = _ln_proj(x2, gi, bi, w_u, w_rest, min(PROJ_TM, seq))
    u_meta, proj_meta = _ln_proj(meta_tokens.astype(f32), gi, bi, w_u, w_rest, N_META)

    bb, ab, cm = _ssm_params(ssm_a_re[l], ssm_a_im[l], ssm_log_dt[l], ssm_b_re[l], ssm_b_im[l],
                             ssm_c_re[l], ssm_c_im[l])
    u_tm = u.reshape(nb, seq, SSM_WIDTH).transpose(1, 0, 2).reshape(seq * nb, SSM_WIDTH)
    u_meta_rows = jnp.repeat(u_meta, nb, axis=0)
    y_ssm_tm = _ssm(u_meta_rows, u_tm, bb, ab, cm, row2(ssm_d[l]), ssm_w_glu[l].astype(bf16), nb)
    y_ssm = y_ssm_tm.reshape(seq, nb, SSM_WIDTH).transpose(1, 0, 2).reshape(t_rows, SSM_WIDTH)

    lam = (jnp.exp(jnp.sum(attn_lambda_q1[l].astype(f32) * attn_lambda_k1[l].astype(f32)))
           - jnp.exp(jnp.sum(attn_lambda_q2[l].astype(f32) * attn_lambda_k2[l].astype(f32)))
           + lambda_init).reshape(1)
    g_scaled = row2(attn_subln_g[l]) * (1.0 - lambda_init)
    y_attn = _attention(lam, proj.reshape(nb, seq, GATE_COLS + QKV_COLS), proj_meta, g_scaled, nb, seq)
    y_attn = y_attn.reshape(t_rows, ATTN_WIDTH)

    w_r = jnp.concatenate([router_e_w[l].astype(f32), router_g_w[l].astype(f32)], axis=1)
    w_r = jnp.pad(w_r, ((0, 0), (0, LANES - w_r.shape[1])))
    b_r = jnp.concatenate([router_e_b[l].astype(f32), router_g_b[l].astype(f32)])
    b_r = jnp.pad(b_r, (0, LANES - b_r.shape[0])).reshape(1, LANES)
    h1, h1_packed, route, cnt = _merge(x2, gi, bi, proj, y_ssm, y_attn, w_br_ssm[l].astype(bf16),
                                       w_br_attn[l].astype(bf16), w_o[l].astype(bf16), row2(ln1_g[l]),
                                       row2(ln1_b[l]), w_r.astype(bf16), b_r)

    counts = cnt[0, :N_EXPERTS].astype(jnp.int32)
    padded = (counts + ROW_BLOCK - 1) // ROW_BLOCK * ROW_BLOCK
    pad_ends = jnp.cumsum(padded)
    pad_starts = pad_ends - padded
    expert = route[:, 0:TOP_K].astype(jnp.int32)
    rank = route[:, 4:4 + TOP_K].astype(jnp.int32)
    dest = pad_starts[expert] + rank
    n_blocks = -(-(t_rows * TOP_K + N_EXPERTS * (ROW_BLOCK - 1)) // ROW_BLOCK)
    block_start = jnp.arange(n_blocks, dtype=jnp.int32) * ROW_BLOCK
    block_exp = jnp.minimum(jnp.sum(pad_ends[None, :] <= block_start[:, None], axis=1),
                            N_EXPERTS - 1).astype(jnp.int32)
    block_exp = jnp.concatenate([block_exp, block_exp[-1:]])
    n_used = (pad_ends[-1:] // ROW_BLOCK).astype(jnp.int32)

    rb = ROW_BLOCK
    stride = t_rows + rb
    fill_lo = jnp.concatenate([pad_starts + counts, pad_ends[-1:]]).astype(jnp.int32)
    fill_hi = jnp.concatenate([pad_ends, jnp.full((1,), n_blocks * rb)]).astype(jnp.int32)
    row_map = _row_map(fill_lo, fill_hi, dest.reshape(t_rows // ROWMAP_TM, 1, TOP_K * ROWMAP_TM),
                       n_blocks * rb, t_rows)
    dst_blocks = jnp.concatenate([t_rows + jnp.arange(rb, dtype=jnp.int32), row_map]).reshape(
        n_blocks + 1, 1, rb)
    src = jnp.minimum(row_map % stride, t_rows - 1)
    src_blocks = jnp.concatenate([src, jnp.zeros((2 * rb,), jnp.int32)]).reshape(n_blocks + 2, 1, rb)

    ysc = _moe(block_exp, n_used, src_blocks, dst_blocks, h1_packed, exp_w_gate[l], exp_w_up[l],
               exp_w_down[l], n_blocks)
    out = _combine(h1, route, ysc, row2(ln2_g[l]), row2(ln2_b[l]))
    return out.reshape(nb, seq, d).astype(x.dtype)
```

```python
import functools
import math

import jax
import jax.numpy as jnp
from jax import lax
from jax.experimental import pallas as pl
from jax.experimental.pallas import tpu as pltpu

D_MODEL = 2048
N_META = 16
CHUNK = 64
ATTN_HEADS = 8
ATTN_HEAD_DIM = 64
ATTN_VDIM = 2 * ATTN_HEAD_DIM
ATTN_WIDTH = ATTN_HEADS * ATTN_VDIM
SSM_WIDTH = D_MODEL // 4
SSM_GROUP = 16
SSM_GROUPS = SSM_WIDTH // SSM_GROUP
SSM_STATE = 64
SSM_COLS = SSM_GROUPS * SSM_STATE
N_EXPERT_GROUPS = 4
EXPERTS_PER_GROUP = 8
N_EXPERTS = N_EXPERT_GROUPS * EXPERTS_PER_GROUP
TOP_K = 2
D_FF_EXPERT = D_MODEL // 4
QKV_COLS = 3 * ATTN_WIDTH
GATE_COLS = 2 * D_MODEL
IN_COLS = SSM_WIDTH + QKV_COLS + GATE_COLS
LN_EPS = 1e-5
DEPTH = 1
DEEPNORM_ALPHA = (2.0 * DEPTH) ** 0.25
LANES = 128
NEG_INF = -1e30

PROJ_TN = 1024
PROJ_TM = 1024
SSM_TC = 32
SSM_CB = 512
SSM_HALVES = 2
ATTN_TQ = 256
MERGE_TM = 256
MERGE_SUB = 256
ROW_BLOCK = 512
COMBINE_TM = 256
ROWMAP_TM = 512
VMEM_LIMIT = 56 * 1024 * 1024


def _pack_halves(x):
    n = x.shape[1] // 2
    bits = lambda v: lax.bitcast_convert_type(v.astype(jnp.bfloat16).astype(jnp.float32), jnp.uint32)
    return bits(x[:, :n]) | (bits(x[:, n:]) >> 16)


def _unpack_halves(w):
    hi = lax.bitcast_convert_type(w & jnp.uint32(0xFFFF0000), jnp.float32)
    lo = lax.bitcast_convert_type(w << 16, jnp.float32)
    return hi, lo


def _layer_norm(x, g, b):
    mu = jnp.mean(x, axis=-1, keepdims=True)
    xc = x - mu
    var = jnp.mean(xc * xc, axis=-1, keepdims=True)
    return xc * lax.rsqrt(var + LN_EPS) * g + b


def _ln_proj_kernel(x_ref, g_ref, b_ref, wu_ref, w_ref, u_ref, proj_ref, xn_ref):
    j = pl.program_id(1)

    @pl.when(j == 0)
    def _():
        xn = _layer_norm(x_ref[...], g_ref[...], b_ref[...]).astype(jnp.bfloat16)
        xn_ref[...] = xn
        u_ref[...] = jnp.dot(xn, wu_ref[...], preferred_element_type=jnp.float32)

    acc = jnp.dot(xn_ref[...], w_ref[...], preferred_element_type=jnp.float32)
    proj_ref[...] = jnp.where(j < GATE_COLS // PROJ_TN, jax.nn.sigmoid(acc), acc).astype(jnp.bfloat16)


def _ln_proj(x2, g, b, w_u, w_rest, tm):
    t_rows = x2.shape[0]
    n_col = (GATE_COLS + QKV_COLS) // PROJ_TN
    assert QKV_COLS % PROJ_TN == 0 and GATE_COLS % PROJ_TN == 0 and t_rows % tm == 0
    return pl.pallas_call(
        _ln_proj_kernel,
        grid=(t_rows // tm, n_col),
        in_specs=[
            pl.BlockSpec((tm, D_MODEL), lambda i, j: (i, 0)),
            pl.BlockSpec((1, D_MODEL), lambda i, j: (0, 0)),
            pl.BlockSpec((1, D_MODEL), lambda i, j: (0, 0)),
            pl.BlockSpec((D_MODEL, SSM_WIDTH), lambda i, j: (0, 0)),
            pl.BlockSpec((D_MODEL, PROJ_TN), lambda i, j: (0, j)),
        ],
        out_specs=[
            pl.BlockSpec((tm, SSM_WIDTH), lambda i, j: (i, 0)),
            pl.BlockSpec((tm, PROJ_TN), lambda i, j: (i, j)),
        ],
        out_shape=[
            jax.ShapeDtypeStruct((t_rows, SSM_WIDTH), jnp.float32),
            jax.ShapeDtypeStruct((t_rows, GATE_COLS + QKV_COLS), jnp.bfloat16),
        ],
        scratch_shapes=[pltpu.VMEM((tm, D_MODEL), jnp.bfloat16)],
        compiler_params=pltpu.CompilerParams(
            dimension_semantics=("parallel", "arbitrary"), vmem_limit_bytes=VMEM_LIMIT),
        name="ln_proj",
    )(x2, g, b, w_u, w_rest)


def _ssm_kernel(um_ref, u_ref, bb_ref, ab_ref, cm_ref, d_ref, wglu_ref, y_ref, state_ref, bu_ref, *, nb):
    half_u = SSM_WIDTH // SSM_HALVES
    half_c = SSM_COLS // SSM_HALVES

    def expand_and_scan(u_bf16, n_steps):
        rows = n_steps * nb
        for h in range(SSM_HALVES):
            bu_ref[pl.ds(0, rows), 2 * half_c * h:2 * half_c * (h + 1)] = jnp.dot(
                u_bf16[:, half_u * h:half_u * (h + 1)], bb_ref[h], preferred_element_type=jnp.float32)
        for cb in range(SSM_COLS // SSM_CB):
            h, off = divmod(cb * SSM_CB, half_c)
            re_cols = pl.ds(2 * half_c * h + off, SSM_CB)
            im_cols = pl.ds(2 * half_c * h + half_c + off, SSM_CB)
            a_re = ab_ref[0:1, cb * SSM_CB:(cb + 1) * SSM_CB]
            a_im = ab_ref[1:2, cb * SSM_CB:(cb + 1) * SSM_CB]

            def step(t, carry):
                s_re, s_im = carry
                r0 = pl.multiple_of(t * nb, nb)
                n_re = a_re * s_re - a_im * s_im + bu_ref[pl.ds(r0, nb), re_cols]
                n_im = a_re * s_im + a_im * s_re + bu_ref[pl.ds(r0, nb), im_cols]
                bu_ref[pl.ds(r0, nb), re_cols] = n_re
                bu_ref[pl.ds(r0, nb), im_cols] = n_im
                return n_re, n_im

            s_re, s_im = lax.fori_loop(0, n_steps, step, (state_ref[:, re_cols], state_ref[:, im_cols]))
            state_ref[:, re_cols] = s_re
            state_ref[:, im_cols] = s_im

    @pl.when(pl.program_id(0) == 0)
    def _():
        state_ref[...] = jnp.zeros_like(state_ref)
        expand_and_scan(um_ref[...].astype(jnp.bfloat16), N_META)

    u = u_ref[...]
    expand_and_scan(u.astype(jnp.bfloat16), SSM_TC)
    y = jnp.concatenate(
        [jnp.dot(bu_ref[:, 2 * half_c * h:2 * half_c * (h + 1)].astype(jnp.bfloat16), cm_ref[h],
                 preferred_element_type=jnp.float32) for h in range(SSM_HALVES)], axis=1)
    y = jax.nn.gelu(y + d_ref[...] * u)
    gate = jnp.dot(y.astype(jnp.bfloat16), wglu_ref[...], preferred_element_type=jnp.float32)
    y_ref[...] = (y * jax.nn.sigmoid(gate)).astype(y_ref.dtype)


def _ssm(u_meta_rows, u_tm, bb, ab, cm, d, wglu, nb):
    rows = u_tm.shape[0]
    blk = SSM_TC * nb
    assert rows % blk == 0 and nb % 8 == 0 and N_META <= SSM_TC
    const = lambda i: (0, 0)
    return pl.pallas_call(
        functools.partial(_ssm_kernel, nb=nb),
        grid=(rows // blk,),
        in_specs=[
            pl.BlockSpec((N_META * nb, SSM_WIDTH), const),
            pl.BlockSpec((blk, SSM_WIDTH), lambda i: (i, 0)),
            pl.BlockSpec((SSM_HALVES, SSM_WIDTH // SSM_HALVES, 2 * SSM_COLS // SSM_HALVES), lambda i: (0, 0, 0)),
            pl.BlockSpec((2, SSM_COLS), const),
            pl.BlockSpec((SSM_HALVES, 2 * SSM_COLS // SSM_HALVES, SSM_WIDTH // SSM_HALVES), lambda i: (0, 0, 0)),
            pl.BlockSpec((1, SSM_WIDTH), const),
            pl.BlockSpec((SSM_WIDTH, SSM_WIDTH), const),
        ],
        out_specs=pl.BlockSpec((blk, SSM_WIDTH), lambda i: (i, 0)),
        out_shape=jax.ShapeDtypeStruct((rows, SSM_WIDTH), jnp.bfloat16),
        scratch_shapes=[
            pltpu.VMEM((nb, 2 * SSM_COLS), jnp.float32),
            pltpu.VMEM((blk, 2 * SSM_COLS), jnp.float32),
        ],
        compiler_params=pltpu.CompilerParams(
            dimension_semantics=("arbitrary",), vmem_limit_bytes=VMEM_LIMIT),
        name="ssm",
    )(u_meta_rows, u_tm, bb, ab, cm, d, wglu)


def _ssm_params(a_re, a_im, log_dt, b_re, b_im, c_re, c_im):
    f32 = jnp.float32
    lam_re = jnp.minimum(a_re.astype(f32), -1e-4)
    lam_im = a_im.astype(f32)
    dt = jnp.exp(log_dt.astype(f32))[:, None]
    mag = jnp.exp(lam_re * dt)
    ab_re = mag * jnp.cos(lam_im * dt)
    ab_im = mag * jnp.sin(lam_im * dt)
    den = lam_re * lam_re + lam_im * lam_im
    nr = ab_re - 1.0
    ni = ab_im
    z_re = (nr * lam_re + ni * lam_im) / den
    z_im = (ni * lam_re - nr * lam_im) / den
    br32 = b_re.astype(f32)
    bi32 = b_im.astype(f32)
    bb_re = z_re[..., None] * br32 - z_im[..., None] * bi32
    bb_im = z_re[..., None] * bi32 + z_im[..., None] * br32
    gh = SSM_GROUPS // SSM_HALVES
    eye = jnp.eye(gh, dtype=f32)
    split = lambda a: a.reshape((SSM_HALVES, gh) + a.shape[1:])
    exp_re = jnp.einsum('kgpc,gh->kgchp', split(bb_re), eye).reshape(SSM_HALVES, gh * SSM_GROUP, gh * SSM_STATE)
    exp_im = jnp.einsum('kgpc,gh->kgchp', split(bb_im), eye).reshape(SSM_HALVES, gh * SSM_GROUP, gh * SSM_STATE)
    bb = jnp.concatenate([exp_re, exp_im], axis=2)
    ro_re = jnp.einsum('kgcp,gh->kgphc', split(c_re.astype(f32)), eye).reshape(
        SSM_HALVES, gh * SSM_STATE, gh * SSM_GROUP)
    ro_im = jnp.einsum('kgcp,gh->kgphc', split(c_im.astype(f32)), eye).reshape(
        SSM_HALVES, gh * SSM_STATE, gh * SSM_GROUP)
    cm = jnp.concatenate([ro_re, -ro_im], axis=1)
    ab = jnp.stack([ab_re.reshape(SSM_COLS), ab_im.reshape(SSM_COLS)], axis=0)
    return bb.astype(jnp.bfloat16), ab, cm.astype(jnp.bfloat16)


def _attn_kernel(lam_ref, q_ref, k_ref, v_ref, km_ref, vm_ref, g_ref, o_ref, vt_ref, *, scale, seq):
    tq = ATTN_TQ
    f32, bf16 = jnp.float32, jnp.bfloat16
    vt_ref[...] = v_ref[...].astype(f32).T.astype(bf16)
    vm_t = vm_ref[...].astype(f32).T.astype(bf16)
    k_meta = km_ref[...]
    lane = lax.broadcasted_iota(jnp.int32, (tq, ATTN_VDIM), 1)
    key = lax.broadcasted_iota(jnp.int32, (tq, 2 * tq), 0)
    qry = lax.broadcasted_iota(jnp.int32, (tq, 2 * tq), 1) % tq
    diag_visible = qry // CHUNK >= key // CHUNK
    col_max = lambda a: jnp.max(a, axis=0, keepdims=True)
    col_sum = lambda a: jnp.sum(a, axis=0, keepdims=True)

    def scores(i):
        rows = slice(i * tq, (i + 1) * tq)
        q = q_ref[rows, :] * scale
        zero = jnp.zeros_like(q)
        qq = jnp.concatenate([jnp.where(lane < ATTN_HEAD_DIM, q, zero),
                              jnp.where(lane >= ATTN_HEAD_DIM, q, zero)], axis=0)
        scores_t = lambda kb: lax.dot_general(kb, qq, (((1,), (1,)), ((), ())), preferred_element_type=f32)
        s_meta = scores_t(k_meta)
        s_diag = jnp.where(diag_visible, scores_t(k_ref[rows, :]), NEG_INF)
        s_prev = scores_t(k_ref[0:i * tq, :]) if i > 0 else None
        return s_meta, s_diag, s_prev

    def softmax(i, s):
        s_meta, s_diag, s_prev = s
        m = jnp.maximum(col_max(s_meta), col_max(s_diag))
        if i > 0:
            m = jnp.maximum(m, col_max(s_prev))
        p_meta = jnp.exp(s_meta - m)
        p_diag = jnp.exp(s_diag - m)
        l = col_sum(p_meta) + col_sum(p_diag)
        p_prev = None
        if i > 0:
            p_prev = jnp.exp(s_prev - m)
            l = l + col_sum(p_prev)
            p_prev = p_prev.astype(bf16)
        return p_meta.astype(bf16), p_diag.astype(bf16), p_prev, l

    def weighted_values(i, p):
        p_meta, p_diag, p_prev, l = p
        rows = slice(i * tq, (i + 1) * tq)
        acc = (jnp.dot(vm_t, p_meta, preferred_element_type=f32)
               + jnp.dot(vt_ref[:, rows], p_diag, preferred_element_type=f32))
        if i > 0:
            acc = acc + jnp.dot(vt_ref[:, 0:i * tq], p_prev, preferred_element_type=f32)
        o_all = acc / l
        o_t = o_all[:, :tq] - lam_ref[0] * o_all[:, tq:]
        o_t = o_t * lax.rsqrt(jnp.mean(o_t * o_t, axis=0, keepdims=True) + LN_EPS)
        o_ref[rows, :] = (o_t.T * g_ref[...]).astype(o_ref.dtype)

    n_blk = seq // tq
    s_next = scores(0)
    p_last = None
    for i in range(n_blk):
        s_cur = s_next
        s_next = scores(i + 1) if i + 1 < n_blk else None
        if p_last is not None:
            weighted_values(i - 1, p_last)
        p_last = softmax(i, s_cur)
    weighted_values(n_blk - 1, p_last)


def _attention(lam, proj, proj_meta, g_scaled, nb, seq):
    assert seq % ATTN_TQ == 0
    nh = ATTN_HEADS
    first = GATE_COLS // ATTN_VDIM
    head = lambda part: pl.BlockSpec((None, seq, ATTN_VDIM), lambda b, h: (b, 0, first + part * nh + h))
    return pl.pallas_call(
        functools.partial(_attn_kernel, scale=ATTN_HEAD_DIM ** -0.5, seq=seq),
        grid=(nb, nh),
        in_specs=[
            pl.BlockSpec(memory_space=pltpu.SMEM),
            head(0), head(1), head(2),
            pl.BlockSpec((N_META, ATTN_VDIM), lambda b, h: (0, first + nh + h)),
            pl.BlockSpec((N_META, ATTN_VDIM), lambda b, h: (0, first + 2 * nh + h)),
            pl.BlockSpec((1, ATTN_VDIM), lambda b, h: (0, 0)),
        ],
        out_specs=pl.BlockSpec((None, seq, ATTN_VDIM), lambda b, h: (b, 0, h)),
        out_shape=jax.ShapeDtypeStruct((nb, seq, ATTN_WIDTH), jnp.bfloat16),
        scratch_shapes=[pltpu.VMEM((ATTN_VDIM, seq), jnp.bfloat16)],
        compiler_params=pltpu.CompilerParams(
            dimension_semantics=("parallel", "parallel"), vmem_limit_bytes=VMEM_LIMIT),
        name="attention",
    )(lam, proj, proj, proj, proj_meta, proj_meta, g_scaled)


def _merge_kernel(x_ref, gi_ref, bi_ref, gate_ref, ys_ref, ya_ref, wbs_ref, wba_ref, wo_ref,
                  g1_ref, b1_ref, wr_ref, br_ref, h1_ref, h1p_ref, route_ref, cnt_ref):
    sub = MERGE_SUB

    @pl.when(pl.program_id(0) == 0)
    def _():
        cnt_ref[...] = jnp.zeros_like(cnt_ref)

    lane_i = lax.broadcasted_iota(jnp.int32, (sub, LANES), 1)
    lane = lane_i.astype(jnp.float32)
    is_group = jnp.logical_and(lane_i >= N_EXPERTS, lane_i < N_EXPERTS + N_EXPERT_GROUPS)
    lane_group = (lane_i // EXPERTS_PER_GROUP).astype(jnp.float32)
    r_i = lax.broadcasted_iota(jnp.int32, (sub, sub), 0)
    c_i = lax.broadcasted_iota(jnp.int32, (sub, sub), 1)
    tri = jnp.where(c_i < r_i, 1.0, 0.0).astype(jnp.bfloat16)

    def first_argmax(vals):
        mx = jnp.max(vals, axis=-1, keepdims=True)
        idx = jnp.min(jnp.where(vals == mx, lane, float(LANES)), axis=-1, keepdims=True)
        return mx, idx

    def mix_stage(rows):
        gates = gate_ref[rows, :].astype(jnp.float32)
        ps = jnp.dot(ys_ref[rows, :], wbs_ref[...], preferred_element_type=jnp.float32)
        pa = jnp.dot(ya_ref[rows, :], wba_ref[...], preferred_element_type=jnp.float32)
        merged = gates[:, :D_MODEL] * ps + gates[:, D_MODEL:] * pa
        return jnp.dot(merged.astype(jnp.bfloat16), wo_ref[...], preferred_element_type=jnp.float32)

    def route_stage(rows, mix, count):
        h = _layer_norm(x_ref[rows, :], gi_ref[...], bi_ref[...])
        h1 = _layer_norm(DEEPNORM_ALPHA * h + mix, g1_ref[...], b1_ref[...])
        h1_ref[rows, :] = h1
        h1p_ref[rows, :] = _pack_halves(h1)

        logits = jnp.dot(h1.astype(jnp.bfloat16), wr_ref[...], preferred_element_type=jnp.float32) + br_ref[...]
        gl = jnp.where(is_group, logits, NEG_INF)
        gmax, glane = first_argmax(gl)
        g_val = 1.0 / jnp.sum(jnp.where(is_group, jnp.exp(gl - gmax), 0.0), axis=-1, keepdims=True)
        g_idx = glane - float(N_EXPERTS)
        in_group = jnp.logical_and(lane_i < N_EXPERTS, lane_group == g_idx)
        el = jnp.where(in_group, logits, NEG_INF)
        m1, i1 = first_argmax(el)
        el2 = jnp.where(lane == i1, NEG_INF, el)
        m2, i2 = first_argmax(el2)
        e2 = jnp.exp(m2 - m1)
        w1 = g_val / (1.0 + e2)
        w2 = g_val * e2 / (1.0 + e2)

        oh = jnp.where(jnp.logical_or(lane == i1, lane == i2), 1.0, 0.0)
        before = jnp.dot(tri, oh.astype(jnp.bfloat16), preferred_element_type=jnp.float32) + count
        rank1 = jnp.sum(jnp.where(lane == i1, before, 0.0), axis=-1, keepdims=True)
        rank2 = jnp.sum(jnp.where(lane == i2, before, 0.0), axis=-1, keepdims=True)

        route = jnp.where(lane_i == 0, i1, 0.0)
        route = jnp.where(lane_i == 1, i2, route)
        route = jnp.where(lane_i == 2, w1, route)
        route = jnp.where(lane_i == 3, w2, route)
        route = jnp.where(lane_i == 4, rank1, route)
        route = jnp.where(lane_i == 5, rank2, route)
        route_ref[rows, :] = route
        return count + jnp.sum(oh, axis=0, keepdims=True)

    n_chain = x_ref.shape[0] // sub
    rows = [slice(c * sub, (c + 1) * sub) for c in range(n_chain)]
    count = cnt_ref[...]
    mix = mix_stage(rows[0])
    for c in range(n_chain):
        mix_next = mix_stage(rows[c + 1]) if c + 1 < n_chain else None
        count = route_stage(rows[c], mix, count)
        mix = mix_next
    cnt_ref[...] = count


def _merge(x2, gi, bi, gates, ys, ya, wbs, wba, wo, g1, b1, wr, br):
    t_rows = x2.shape[0]
    tm = MERGE_TM
    assert t_rows % tm == 0 and tm % MERGE_SUB == 0
    row = lambda i: (i, 0)
    const = lambda i: (0, 0)
    return pl.pallas_call(
        _merge_kernel,
        grid=(t_rows // tm,),
        in_specs=[
            pl.BlockSpec((tm, D_MODEL), row),
            pl.BlockSpec((1, D_MODEL), const),
            pl.BlockSpec((1, D_MODEL), const),
            pl.BlockSpec((tm, GATE_COLS), row),
            pl.BlockSpec((tm, SSM_WIDTH), row),
            pl.BlockSpec((tm, ATTN_WIDTH), row),
            pl.BlockSpec((SSM_WIDTH, D_MODEL), const, pipeline_mode=pl.Buffered(1)),
            pl.BlockSpec((ATTN_WIDTH, D_MODEL), const, pipeline_mode=pl.Buffered(1)),
            pl.BlockSpec((D_MODEL, D_MODEL), const, pipeline_mode=pl.Buffered(1)),
            pl.BlockSpec((1, D_MODEL), const),
            pl.BlockSpec((1, D_MODEL), const),
            pl.BlockSpec((D_MODEL, LANES), const),
            pl.BlockSpec((1, LANES), const),
        ],
        out_specs=[
            pl.BlockSpec((tm, D_MODEL), row),
            pl.BlockSpec((tm, D_MODEL // 2), row),
            pl.BlockSpec((tm, LANES), row),
            pl.BlockSpec((1, LANES), const),
        ],
        out_shape=[
            jax.ShapeDtypeStruct((t_rows, D_MODEL), jnp.float32),
            jax.ShapeDtypeStruct((t_rows, D_MODEL // 2), jnp.uint32),
            jax.ShapeDtypeStruct((t_rows, LANES), jnp.float32),
            jax.ShapeDtypeStruct((1, LANES), jnp.float32),
        ],
        compiler_params=pltpu.CompilerParams(
            dimension_semantics=("arbitrary",), vmem_limit_bytes=VMEM_LIMIT),
        name="merge",
    )(x2, gi, bi, gates, ys, ya, wbs, wba, wo, g1, b1, wr, br)


def _row_map_kernel(lo_ref, hi_ref, dest_ref, table_ref, *, t_rows):
    i = pl.program_id(0)
    tm = ROWMAP_TM
    stride = t_rows + ROW_BLOCK

    @pl.when(i == 0)
    def _():
        for e in range(N_EXPERTS + 1):
            def fill(r, carry):
                table_ref[r] = t_rows + (r & (ROW_BLOCK - 1))
                return carry
            lax.fori_loop(lo_ref[e], hi_ref[e], fill, 0)

    base = i * tm
    for a in range(TOP_K * tm):
        table_ref[dest_ref[0, a]] = base + (a // TOP_K + (a % TOP_K) * stride)


def _row_map(fill_lo, fill_hi, dest3, n_rows, t_rows):
    tm = ROWMAP_TM
    assert t_rows % tm == 0 and ROW_BLOCK & (ROW_BLOCK - 1) == 0
    grid_spec = pltpu.PrefetchScalarGridSpec(
        num_scalar_prefetch=2,
        grid=(t_rows // tm,),
        in_specs=[pl.BlockSpec((None, 1, TOP_K * tm), lambda i, lo, hi: (i, 0, 0), memory_space=pltpu.SMEM)],
        out_specs=pl.BlockSpec(memory_space=pltpu.SMEM),
    )
    return pl.pallas_call(
        functools.partial(_row_map_kernel, t_rows=t_rows),
        grid_spec=grid_spec,
        out_shape=jax.ShapeDtypeStruct((n_rows,), jnp.int32),
        compiler_params=pltpu.CompilerParams(dimension_semantics=("arbitrary",)),
        name="row_map",
    )(fill_lo, fill_hi, dest3)


def _moe_kernel(be_ref, nu_ref, src_cur_ref, src_next_ref, dst_prev_ref, h1_hbm, wg_ref, wu_ref, wd_ref,
                ysc_hbm, xbuf0, xbuf1, ybuf0, ybuf1, wgb, wub, wdb, gsem, ssem):
    i = pl.program_id(0)
    n_used = nu_ref[0]
    rb = ROW_BLOCK
    xbuf = (xbuf0, xbuf1)
    ybuf = (ybuf0, ybuf1)

    def start_gather(idx_ref, s):
        for r in range(rb):
            pltpu.make_async_copy(h1_hbm.at[pl.ds(idx_ref[0, r], 1)], xbuf[s].at[pl.ds(r, 1)],
                                  gsem.at[s]).start(priority=r % 2)

    def wait_gather(s):
        pltpu.make_async_copy(h1_hbm.at[pl.ds(0, rb)], xbuf[s], gsem.at[s]).wait()

    def start_scatter(s):
        for r in range(rb):
            pltpu.make_async_copy(ybuf[s].at[pl.ds(r, 1)], ysc_hbm.at[pl.ds(dst_prev_ref[0, r], 1)],
                                  ssem.at[0]).start(priority=r % 2)

    def wait_scatter(s):
        pltpu.make_async_copy(ybuf[s], ysc_hbm.at[pl.ds(0, rb)], ssem.at[0]).wait()

    @pl.when(i == 0)
    def _():
        ybuf1[...] = jnp.zeros_like(ybuf1)
        start_gather(src_cur_ref, 0)

    expert_changed = jnp.logical_or(i == 0, be_ref[i] != be_ref[jnp.maximum(i - 1, 0)])

    @pl.when(jnp.logical_and(i <= n_used, expert_changed))
    def _():
        wgb[...] = wg_ref[...].astype(jnp.bfloat16)
        wub[...] = wu_ref[...].astype(jnp.bfloat16)
        wdb[...] = wd_ref[...].astype(jnp.bfloat16)

    def step(slot):
        other = 1 - slot
        wait_gather(slot)

        @pl.when(i >= 1)
        def _():
            wait_scatter(slot)

        start_gather(src_next_ref, other)
        start_scatter(other)
        xb = jnp.concatenate([half.astype(jnp.bfloat16) for half in _unpack_halves(xbuf[slot][...])], axis=1)
        g = jnp.dot(xb, wgb[...], preferred_element_type=jnp.float32)
        u = jnp.dot(xb, wub[...], preferred_element_type=jnp.float32)
        hb = (jax.nn.silu(g) * u).astype(jnp.bfloat16)
        ybuf[slot][...] = _pack_halves(jnp.dot(hb, wdb[...], preferred_element_type=jnp.float32))

        @pl.when(i == n_used)
        def _():
            wait_gather(other)
            wait_scatter(other)

    for parity in range(2):
        pl.when(jnp.logical_and(i <= n_used, i % 2 == parity))(functools.partial(step, parity))


def _moe(block_exp, n_used, src_blocks, dst_blocks, h1, wg, wu, wd, n_blocks):
    t_rows = h1.shape[0]
    rb = ROW_BLOCK
    wsel = lambda i, be, nu: (be[i], 0, 0)
    smem_blk = lambda off: pl.BlockSpec((None, 1, rb), lambda i, be, nu: (i + off, 0, 0),
                                        memory_space=pltpu.SMEM)
    grid_spec = pltpu.PrefetchScalarGridSpec(
        num_scalar_prefetch=2,
        grid=(n_blocks + 1,),
        in_specs=[
            smem_blk(0), smem_blk(1), smem_blk(0),
            pl.BlockSpec(memory_space=pl.ANY),
            pl.BlockSpec((None, D_MODEL, D_FF_EXPERT), wsel),
            pl.BlockSpec((None, D_MODEL, D_FF_EXPERT), wsel),
            pl.BlockSpec((None, D_FF_EXPERT, D_MODEL), wsel),
        ],
        out_specs=pl.BlockSpec(memory_space=pl.ANY),
        scratch_shapes=[
            pltpu.VMEM((rb, D_MODEL // 2), jnp.uint32),
            pltpu.VMEM((rb, D_MODEL // 2), jnp.uint32),
            pltpu.VMEM((rb, D_MODEL // 2), jnp.uint32),
            pltpu.VMEM((rb, D_MODEL // 2), jnp.uint32),
            pltpu.VMEM((D_MODEL, D_FF_EXPERT), jnp.bfloat16),
            pltpu.VMEM((D_MODEL, D_FF_EXPERT), jnp.bfloat16),
            pltpu.VMEM((D_FF_EXPERT, D_MODEL), jnp.bfloat16),
            pltpu.SemaphoreType.DMA((2,)),
            pltpu.SemaphoreType.DMA((1,)),
        ],
    )
    return pl.pallas_call(
        _moe_kernel,
        grid_spec=grid_spec,
        out_shape=jax.ShapeDtypeStruct((TOP_K * t_rows + rb, D_MODEL // 2), jnp.uint32),
        compiler_params=pltpu.CompilerParams(
            dimension_semantics=("arbitrary",), vmem_limit_bytes=VMEM_LIMIT),
        name="experts",
    )(block_exp, n_used, src_blocks, src_blocks, dst_blocks, h1, wg, wu, wd)


def _combine_kernel(h1_ref, route_ref, y0_ref, y1_ref, g_ref, b_ref, o_ref):
    route = route_ref[...]
    y0 = _unpack_halves(y0_ref[...])
    y1 = _unpack_halves(y1_ref[...])
    ffn = jnp.concatenate([route[:, 2:3] * a + route[:, 3:4] * b for a, b in zip(y0, y1)], axis=1)
    o_ref[...] = _layer_norm(DEEPNORM_ALPHA * h1_ref[...] + ffn, g_ref[...], b_ref[...])


def _combine(h1, route, ysc, g2, b2):
    t_rows = h1.shape[0]
    tm = COMBINE_TM
    assert t_rows % tm == 0 and ROW_BLOCK % tm == 0
    second = (t_rows + ROW_BLOCK) // tm
    row = lambda i: (i, 0)
    const = lambda i: (0, 0)
    return pl.pallas_call(
        _combine_kernel,
        grid=(t_rows // tm,),
        in_specs=[
            pl.BlockSpec((tm, D_MODEL), row),
            pl.BlockSpec((tm, LANES), row),
            pl.BlockSpec((tm, D_MODEL // 2), row),
            pl.BlockSpec((tm, D_MODEL // 2), lambda i: (i + second, 0)),
            pl.BlockSpec((1, D_MODEL), const),
            pl.BlockSpec((1, D_MODEL), const),
        ],
        out_specs=pl.BlockSpec((tm, D_MODEL), row),
        out_shape=jax.ShapeDtypeStruct((t_rows, D_MODEL), jnp.float32),
        compiler_params=pltpu.CompilerParams(
            dimension_semantics=("parallel",), vmem_limit_bytes=VMEM_LIMIT),
        name="combine",
    )(h1, route, ysc, ysc, g2, b2)


def kernel(x, meta_tokens, ln_in_g, ln_in_b, w_in, ssm_a_re, ssm_a_im, ssm_log_dt, ssm_b_re, ssm_b_im,
           ssm_c_re, ssm_c_im, ssm_d, ssm_w_glu, attn_lambda_q1, attn_lambda_k1, attn_lambda_q2,
           attn_lambda_k2, attn_subln_g, w_br_ssm, w_br_attn, w_o, ln1_g, ln1_b, router_g_w, router_g_b,
           router_e_w, router_e_b, exp_w_gate, exp_w_up, exp_w_down, ln2_g, ln2_b):
    f32, bf16 = jnp.float32, jnp.bfloat16
    nb, seq, d = x.shape
    assert d == D_MODEL and w_in.shape[0] == DEPTH == 1
    t_rows = nb * seq
    l = 0
    lambda_init = 0.8 - 0.6 * math.exp(-0.3 * l)
    row2 = lambda v: v.reshape(1, -1).astype(f32)

    x2 = x.reshape(t_rows, d)
    w_u = w_in[l, :, :SSM_WIDTH].astype(bf16)
    w_rest = jnp.concatenate([w_in[l, :, SSM_WIDTH + QKV_COLS:], w_in[l, :, SSM_WIDTH:SSM_WIDTH + QKV_COLS]],
                             axis=1).astype(bf16)
    gi, bi = row2(ln_in_g), row2(ln_in_b)
    u, proj = _ln_proj(x2, gi, bi, w_u, w_rest, min(PROJ_TM, seq))
    u_meta, proj_meta = _ln_proj(meta_tokens.astype(f32), gi, bi, w_u, w_rest, N_META)

    bb, ab, cm = _ssm_params(ssm_a_re[l], ssm_a_im[l], ssm_log_dt[l], ssm_b_re[l], ssm_b_im[l],
                             ssm_c_re[l], ssm_c_im[l])
    u_tm = u.reshape(nb, seq, SSM_WIDTH).transpose(1, 0, 2).reshape(seq * nb, SSM_WIDTH)
    u_meta_rows = jnp.repeat(u_meta, nb, axis=0)
    y_ssm_tm = _ssm(u_meta_rows, u_tm, bb, ab, cm, row2(ssm_d[l]), ssm_w_glu[l].astype(bf16), nb)
    y_ssm = y_ssm_tm.reshape(seq, nb, SSM_WIDTH).transpose(1, 0, 2).reshape(t_rows, SSM_WIDTH)

    lam = (jnp.exp(jnp.sum(attn_lambda_q1[l].astype(f32) * attn_lambda_k1[l].astype(f32)))
           - jnp.exp(jnp.sum(attn_lambda_q2[l].astype(f32) * attn_lambda_k2[l].astype(f32)))
           + lambda_init).reshape(1)
    g_scaled = row2(attn_subln_g[l]) * (1.0 - lambda_init)
    y_attn = _attention(lam, proj.reshape(nb, seq, GATE_COLS + QKV_COLS), proj_meta, g_scaled, nb, seq)
    y_attn = y_attn.reshape(t_rows, ATTN_WIDTH)

    w_r = jnp.concatenate([router_e_w[l].astype(f32), router_g_w[l].astype(f32)], axis=1)
    w_r = jnp.pad(w_r, ((0, 0), (0, LANES - w_r.shape[1])))
    b_r = jnp.concatenate([router_e_b[l].astype(f32), router_g_b[l].astype(f32)])
    b_r = jnp.pad(b_r, (0, LANES - b_r.shape[0])).reshape(1, LANES)
    h1, h1_packed, route, cnt = _merge(x2, gi, bi, proj, y_ssm, y_attn, w_br_ssm[l].astype(bf16),
                                       w_br_attn[l].astype(bf16), w_o[l].astype(bf16), row2(ln1_g[l]),
                                       row2(ln1_b[l]), w_r.astype(bf16), b_r)

    counts = cnt[0, :N_EXPERTS].astype(jnp.int32)
    padded = (counts + ROW_BLOCK - 1) // ROW_BLOCK * ROW_BLOCK
    pad_ends = jnp.cumsum(padded)
    pad_starts = pad_ends - padded
    expert = route[:, 0:TOP_K].astype(jnp.int32)
    rank = route[:, 4:4 + TOP_K].astype(jnp.int32)
    dest = pad_starts[expert] + rank
    n_blocks = -(-(t_rows * TOP_K + N_EXPERTS * (ROW_BLOCK - 1)) // ROW_BLOCK)
    block_start = jnp.arange(n_blocks, dtype=jnp.int32) * ROW_BLOCK
    block_exp = jnp.minimum(jnp.sum(pad_ends[None, :] <= block_start[:, None], axis=1),
                            N_EXPERTS - 1).astype(jnp.int32)
    block_exp = jnp.concatenate([block_exp, block_exp[-1:]])
    n_used = (pad_ends[-1:] // ROW_BLOCK).astype(jnp.int32)

    rb = ROW_BLOCK
    stride = t_rows + rb
    fill_lo = jnp.concatenate([pad_starts + counts, pad_ends[-1:]]).astype(jnp.int32)
    fill_hi = jnp.concatenate([pad_ends, jnp.full((1,), n_blocks * rb)]).astype(jnp.int32)
    row_map = _row_map(fill_lo, fill_hi, dest.reshape(t_rows // ROWMAP_TM, 1, TOP_K * ROWMAP_TM),
                       n_blocks * rb, t_rows)
    dst_blocks = jnp.concatenate([t_rows + jnp.arange(rb, dtype=jnp.int32), row_map]).reshape(
        n_blocks + 1, 1, rb)
    src = jnp.minimum(row_map % stride, t_rows - 1)
    src_blocks = jnp.concatenate([src, jnp.zeros((2 * rb,), jnp.int32)]).reshape(n_blocks + 2, 1, rb)

    ysc = _moe(block_exp, n_used, src_blocks, dst_blocks, h1_packed, exp_w_gate[l], exp_w_up[l],
               exp_w_down[l], n_blocks)
    out = _combine(h1, route, ysc, row2(ln2_g[l]), row2(ln2_b[l]))
    return out.reshape(nb, seq, d).astype(x.dtype)
```

```python
import functools
import math

import jax
import jax.numpy as jnp
from jax import lax
from jax.experimental import pallas as pl
from jax.experimental.pallas import tpu as pltpu

D_MODEL = 2048
N_META = 16
CHUNK = 64
ATTN_HEADS = 8
ATTN_HEAD_DIM = 64
ATTN_VDIM = 2 * ATTN_HEAD_DIM
ATTN_WIDTH = ATTN_HEADS * ATTN_VDIM
SSM_WIDTH = D_MODEL // 4
SSM_GROUP = 16
SSM_GROUPS = SSM_WIDTH // SSM_GROUP
SSM_STATE = 64
SSM_COLS = SSM_GROUPS * SSM_STATE
N_EXPERT_GROUPS = 4
EXPERTS_PER_GROUP = 8
N_EXPERTS = N_EXPERT_GROUPS * EXPERTS_PER_GROUP
TOP_K = 2
D_FF_EXPERT = D_MODEL // 4
QKV_COLS = 3 * ATTN_WIDTH
GATE_COLS = 2 * D_MODEL
IN_COLS = SSM_WIDTH + QKV_COLS + GATE_COLS
LN_EPS = 1e-5
DEPTH = 1
DEEPNORM_ALPHA = (2.0 * DEPTH) ** 0.25
LANES = 128
NEG_INF = -1e30

PROJ_TN = 1024
PROJ_TM = 1024
SSM_TC = 32
SSM_CB = 512
SSM_HALVES = 2
ATTN_TQ = 256
MERGE_TM = 256
MERGE_SUB = 256
ROW_BLOCK = 256
COMBINE_TM = 256
ROWMAP_TM = 512
ROUTE_ROWS = 8
VMEM_LIMIT = 56 * 1024 * 1024


def _pack_halves(x):
    n = x.shape[1] // 2
    bits = lambda v: lax.bitcast_convert_type(v.astype(jnp.bfloat16).astype(jnp.float32), jnp.uint32)
    return bits(x[:, :n]) | (bits(x[:, n:]) >> 16)


def _unpack_halves(w):
    hi = lax.bitcast_convert_type(w & jnp.uint32(0xFFFF0000), jnp.float32)
    lo = lax.bitcast_convert_type(w << 16, jnp.float32)
    return hi, lo


def _layer_norm(x, g, b):
    mu = jnp.mean(x, axis=-1, keepdims=True)
    xc = x - mu
    var = jnp.mean(xc * xc, axis=-1, keepdims=True)
    return xc * lax.rsqrt(var + LN_EPS) * g + b


def _ln_proj_kernel(x_ref, g_ref, b_ref, wu_ref, w_ref, u_ref, proj_ref, xn_ref, *, n_gate_blocks):
    j = pl.program_id(1)

    @pl.when(j == 0)
    def _():
        xn = _layer_norm(x_ref[...], g_ref[...], b_ref[...]).astype(jnp.bfloat16)
        xn_ref[...] = xn
        u_ref[...] = jnp.dot(xn, wu_ref[...], preferred_element_type=jnp.float32)

    acc = jnp.dot(xn_ref[...], w_ref[...], preferred_element_type=jnp.float32)
    if n_gate_blocks:
        acc = jnp.where(j < n_gate_blocks, jax.nn.sigmoid(acc), acc)
    proj_ref[...] = acc.astype(jnp.bfloat16)


def _ln_proj(x2, g, b, w_u, w_rest, tm, gate_cols):
    t_rows = x2.shape[0]
    cols = w_rest.shape[1]
    n_col = cols // PROJ_TN
    assert cols % PROJ_TN == 0 and gate_cols % PROJ_TN == 0 and t_rows % tm == 0
    return pl.pallas_call(
        functools.partial(_ln_proj_kernel, n_gate_blocks=gate_cols // PROJ_TN),
        grid=(t_rows // tm, n_col),
        in_specs=[
            pl.BlockSpec((tm, D_MODEL), lambda i, j: (i, 0)),
            pl.BlockSpec((1, D_MODEL), lambda i, j: (0, 0)),
            pl.BlockSpec((1, D_MODEL), lambda i, j: (0, 0)),
            pl.BlockSpec((D_MODEL, SSM_WIDTH), lambda i, j: (0, 0)),
            pl.BlockSpec((D_MODEL, PROJ_TN), lambda i, j: (0, j)),
        ],
        out_specs=[
            pl.BlockSpec((tm, SSM_WIDTH), lambda i, j: (i, 0)),
            pl.BlockSpec((tm, PROJ_TN), lambda i, j: (i, j)),
        ],
        out_shape=[
            jax.ShapeDtypeStruct((t_rows, SSM_WIDTH), jnp.float32),
            jax.ShapeDtypeStruct((t_rows, cols), jnp.bfloat16),
        ],
        scratch_shapes=[pltpu.VMEM((tm, D_MODEL), jnp.bfloat16)],
        compiler_params=pltpu.CompilerParams(
            dimension_semantics=("parallel", "arbitrary"), vmem_limit_bytes=VMEM_LIMIT),
        name="ln_proj",
    )(x2, g, b, w_u, w_rest)


def _ssm_kernel(um_ref, u_ref, bb_ref, ab_ref, cm_ref, d_ref, wglu_ref, y_ref, state_ref, bu_ref, *, nb):
    half_u = SSM_WIDTH // SSM_HALVES
    half_c = SSM_COLS // SSM_HALVES

    def expand_and_scan(u_bf16, n_steps):
        rows = n_steps * nb
        for h in range(SSM_HALVES):
            bu_ref[pl.ds(0, rows), 2 * half_c * h:2 * half_c * (h + 1)] = jnp.dot(
                u_bf16[:, half_u * h:half_u * (h + 1)], bb_ref[h], preferred_element_type=jnp.float32)
        for cb in range(SSM_COLS // SSM_CB):
            h, off = divmod(cb * SSM_CB, half_c)
            re_cols = pl.ds(2 * half_c * h + off, SSM_CB)
            im_cols = pl.ds(2 * half_c * h + half_c + off, SSM_CB)
            a_re = ab_ref[0, :, cb * SSM_CB:(cb + 1) * SSM_CB]
            a_im = ab_ref[1, :, cb * SSM_CB:(cb + 1) * SSM_CB]
            groups = [pl.ds(8 * g, 8) for g in range(nb // 8)]

            def step(t, carry):
                r0 = pl.multiple_of(t * nb, nb)
                out = []
                for g, (s_re, s_im) in enumerate(carry):
                    rows = pl.ds(r0 + 8 * g, 8)
                    n_re = a_re * s_re - a_im * s_im + bu_ref[rows, re_cols]
                    n_im = a_re * s_im + a_im * s_re + bu_ref[rows, im_cols]
                    bu_ref[rows, re_cols] = n_re
                    bu_ref[rows, im_cols] = n_im
                    out.append((n_re, n_im))
                return tuple(out)

            init = tuple((state_ref[g, re_cols], state_ref[g, im_cols]) for g in groups)
            final = lax.fori_loop(0, n_steps, step, init, unroll=2)
            for g, (s_re, s_im) in zip(groups, final):
                state_ref[g, re_cols] = s_re
                state_ref[g, im_cols] = s_im

    @pl.when(pl.program_id(0) == 0)
    def _():
        state_ref[...] = jnp.zeros_like(state_ref)
        expand_and_scan(um_ref[...].astype(jnp.bfloat16), N_META)

    u = u_ref[...]
    expand_and_scan(u.astype(jnp.bfloat16), SSM_TC)
    y = jnp.concatenate(
        [jnp.dot(bu_ref[:, 2 * half_c * h:2 * half_c * (h + 1)].astype(jnp.bfloat16), cm_ref[h],
                 preferred_element_type=jnp.float32) for h in range(SSM_HALVES)], axis=1)
    y = jax.nn.gelu(y + d_ref[...] * u)
    gate = jnp.dot(y.astype(jnp.bfloat16), wglu_ref[...], preferred_element_type=jnp.float32)
    y_ref[...] = (y * jax.nn.sigmoid(gate)).astype(y_ref.dtype)


def _ssm(u_meta_rows, u_tm, bb, ab, cm, d, wglu, nb):
    rows = u_tm.shape[0]
    blk = SSM_TC * nb
    assert rows % blk == 0 and nb % 8 == 0 and N_META <= SSM_TC
    const = lambda i: (0, 0)
    return pl.pallas_call(
        functools.partial(_ssm_kernel, nb=nb),
        grid=(rows // blk,),
        in_specs=[
            pl.BlockSpec((N_META * nb, SSM_WIDTH), const),
            pl.BlockSpec((blk, SSM_WIDTH), lambda i: (i, 0)),
            pl.BlockSpec((SSM_HALVES, SSM_WIDTH // SSM_HALVES, 2 * SSM_COLS // SSM_HALVES), lambda i: (0, 0, 0)),
            pl.BlockSpec((2, 8, SSM_COLS), lambda i: (0, 0, 0)),
            pl.BlockSpec((SSM_HALVES, 2 * SSM_COLS // SSM_HALVES, SSM_WIDTH // SSM_HALVES), lambda i: (0, 0, 0)),
            pl.BlockSpec((1, SSM_WIDTH), const),
            pl.BlockSpec((SSM_WIDTH, SSM_WIDTH), const),
        ],
        out_specs=pl.BlockSpec((blk, SSM_WIDTH), lambda i: (i, 0)),
        out_shape=jax.ShapeDtypeStruct((rows, SSM_WIDTH), jnp.bfloat16),
        scratch_shapes=[
            pltpu.VMEM((nb, 2 * SSM_COLS), jnp.float32),
            pltpu.VMEM((blk, 2 * SSM_COLS), jnp.float32),
        ],
        compiler_params=pltpu.CompilerParams(
            dimension_semantics=("arbitrary",), vmem_limit_bytes=VMEM_LIMIT),
        name="ssm",
    )(u_meta_rows, u_tm, bb, ab, cm, d, wglu)


def _ssm_params(a_re, a_im, log_dt, b_re, b_im, c_re, c_im):
    f32 = jnp.float32
    lam_re = jnp.minimum(a_re.astype(f32), -1e-4)
    lam_im = a_im.astype(f32)
    dt = jnp.exp(log_dt.astype(f32))[:, None]
    mag = jnp.exp(lam_re * dt)
    ab_re = mag * jnp.cos(lam_im * dt)
    ab_im = mag * jnp.sin(lam_im * dt)
    den = lam_re * lam_re + lam_im * lam_im
    nr = ab_re - 1.0
    ni = ab_im
    z_re = (nr * lam_re + ni * lam_im) / den
    z_im = (ni * lam_re - nr * lam_im) / den
    br32 = b_re.astype(f32)
    bi32 = b_im.astype(f32)
    bb_re = z_re[..., None] * br32 - z_im[..., None] * bi32
    bb_im = z_re[..., None] * bi32 + z_im[..., None] * br32
    gh = SSM_GROUPS // SSM_HALVES
    eye = jnp.eye(gh, dtype=f32)
    split = lambda a: a.reshape((SSM_HALVES, gh) + a.shape[1:])
    exp_re = jnp.einsum('kgpc,gh->kgchp', split(bb_re), eye).reshape(SSM_HALVES, gh * SSM_GROUP, gh * SSM_STATE)
    exp_im = jnp.einsum('kgpc,gh->kgchp', split(bb_im), eye).reshape(SSM_HALVES, gh * SSM_GROUP, gh * SSM_STATE)
    bb = jnp.concatenate([exp_re, exp_im], axis=2)
    ro_re = jnp.einsum('kgcp,gh->kgphc', split(c_re.astype(f32)), eye).reshape(
        SSM_HALVES, gh * SSM_STATE, gh * SSM_GROUP)
    ro_im = jnp.einsum('kgcp,gh->kgphc', split(c_im.astype(f32)), eye).reshape(
        SSM_HALVES, gh * SSM_STATE, gh * SSM_GROUP)
    cm = jnp.concatenate([ro_re, -ro_im], axis=1)
    ab = jnp.stack([ab_re.reshape(SSM_COLS), ab_im.reshape(SSM_COLS)], axis=0)
    ab = jnp.broadcast_to(ab[:, None, :], (2, 8, SSM_COLS))
    return bb.astype(jnp.bfloat16), ab, cm.astype(jnp.bfloat16)


def _attn_kernel(lam_ref, q_ref, k_ref, v_ref, km_ref, vm_ref, g_ref, o_ref, vt_ref, *, scale, seq):
    tq = ATTN_TQ
    f32, bf16 = jnp.float32, jnp.bfloat16
    vt_ref[...] = v_ref[...].astype(f32).T.astype(bf16)
    vm_t = vm_ref[...].astype(f32).T.astype(bf16)
    k_meta = km_ref[...]
    lane = lax.broadcasted_iota(jnp.int32, (tq, ATTN_VDIM), 1)
    key = lax.broadcasted_iota(jnp.int32, (tq, 2 * tq), 0)
    qry = lax.broadcasted_iota(jnp.int32, (tq, 2 * tq), 1) % tq
    diag_visible = qry // CHUNK >= key // CHUNK
    col_max = lambda a: jnp.max(a, axis=0, keepdims=True)
    col_sum = lambda a: jnp.sum(a, axis=0, keepdims=True)

    def scores(i):
        rows = slice(i * tq, (i + 1) * tq)
        q = q_ref[rows, :] * scale
        zero = jnp.zeros_like(q)
        qq = jnp.concatenate([jnp.where(lane < ATTN_HEAD_DIM, q, zero),
                              jnp.where(lane >= ATTN_HEAD_DIM, q, zero)], axis=0)
        scores_t = lambda kb: lax.dot_general(kb, qq, (((1,), (1,)), ((), ())), preferred_element_type=f32)
        s_meta = scores_t(k_meta)
        s_diag = jnp.where(diag_visible, scores_t(k_ref[rows, :]), NEG_INF)
        s_prev = scores_t(k_ref[0:i * tq, :]) if i > 0 else None
        return s_meta, s_diag, s_prev

    def softmax(i, s):
        s_meta, s_diag, s_prev = s
        m = jnp.maximum(col_max(s_meta), col_max(s_diag))
        if i > 0:
            m = jnp.maximum(m, col_max(s_prev))
        p_meta = jnp.exp(s_meta - m)
        p_diag = jnp.exp(s_diag - m)
        l = col_sum(p_meta) + col_sum(p_diag)
        p_prev = None
        if i > 0:
            p_prev = jnp.exp(s_prev - m)
            l = l + col_sum(p_prev)
            p_prev = p_prev.astype(bf16)
        return p_meta.astype(bf16), p_diag.astype(bf16), p_prev, l

    def weighted_values(i, p):
        p_meta, p_diag, p_prev, l = p
        rows = slice(i * tq, (i + 1) * tq)
        acc = (jnp.dot(vm_t, p_meta, preferred_element_type=f32)
               + jnp.dot(vt_ref[:, rows], p_diag, preferred_element_type=f32))
        if i > 0:
            acc = acc + jnp.dot(vt_ref[:, 0:i * tq], p_prev, preferred_element_type=f32)
        o_all = acc / l
        o_t = o_all[:, :tq] - lam_ref[0] * o_all[:, tq:]
        o_t = o_t * lax.rsqrt(jnp.mean(o_t * o_t, axis=0, keepdims=True) + LN_EPS)
        o_ref[rows, :] = (o_t.T * g_ref[...]).astype(o_ref.dtype)

    n_blk = seq // tq
    s_next = scores(0)
    p_last = None
    for i in range(n_blk):
        s_cur = s_next
        s_next = scores(i + 1) if i + 1 < n_blk else None
        if p_last is not None:
            weighted_values(i - 1, p_last)
        p_last = softmax(i, s_cur)
    weighted_values(n_blk - 1, p_last)


def _attention(lam, proj, proj_meta, g_scaled, nb, seq):
    assert seq % ATTN_TQ == 0
    nh = ATTN_HEADS
    first = GATE_COLS // ATTN_VDIM
    head = lambda part: pl.BlockSpec((None, seq, ATTN_VDIM), lambda b, h: (b, 0, first + part * nh + h))
    return pl.pallas_call(
        functools.partial(_attn_kernel, scale=ATTN_HEAD_DIM ** -0.5, seq=seq),
        grid=(nb, nh),
        in_specs=[
            pl.BlockSpec(memory_space=pltpu.SMEM),
            head(0), head(1), head(2),
            pl.BlockSpec((N_META, ATTN_VDIM), lambda b, h: (0, h)),
            pl.BlockSpec((N_META, ATTN_VDIM), lambda b, h: (0, nh + h)),
            pl.BlockSpec((1, ATTN_VDIM), lambda b, h: (0, 0)),
        ],
        out_specs=pl.BlockSpec((None, seq, ATTN_VDIM), lambda b, h: (b, 0, h)),
        out_shape=jax.ShapeDtypeStruct((nb, seq, ATTN_WIDTH), jnp.bfloat16),
        scratch_shapes=[pltpu.VMEM((ATTN_VDIM, seq), jnp.bfloat16)],
        compiler_params=pltpu.CompilerParams(
            dimension_semantics=("parallel", "parallel"), vmem_limit_bytes=VMEM_LIMIT),
        name="attention",
    )(lam, proj, proj, proj, proj_meta, proj_meta, g_scaled)


def _merge_kernel(x_ref, gi_ref, bi_ref, gate_ref, ys_ref, ya_ref, wbs_ref, wba_ref, wo_ref,
                  g1_ref, b1_ref, wr_ref, br_ref, h1_ref, h1p_ref, route_ref, route_t_ref, cnt_ref):
    sub = MERGE_SUB

    @pl.when(pl.program_id(0) == 0)
    def _():
        cnt_ref[...] = jnp.zeros_like(cnt_ref)

    lane_i = lax.broadcasted_iota(jnp.int32, (sub, LANES), 1)
    lane = lane_i.astype(jnp.float32)
    is_group = jnp.logical_and(lane_i >= N_EXPERTS, lane_i < N_EXPERTS + N_EXPERT_GROUPS)
    lane_group = (lane_i // EXPERTS_PER_GROUP).astype(jnp.float32)
    r_i = lax.broadcasted_iota(jnp.int32, (sub, sub), 0)
    c_i = lax.broadcasted_iota(jnp.int32, (sub, sub), 1)
    tri = jnp.where(c_i < r_i, 1.0, 0.0).astype(jnp.bfloat16)

    def first_argmax(vals):
        mx = jnp.max(vals, axis=-1, keepdims=True)
        idx = jnp.min(jnp.where(vals == mx, lane, float(LANES)), axis=-1, keepdims=True)
        return mx, idx

    def mix_stage(rows):
        gates = gate_ref[rows, :].astype(jnp.float32)
        ps = jnp.dot(ys_ref[rows, :], wbs_ref[...], preferred_element_type=jnp.float32)
        pa = jnp.dot(ya_ref[rows, :], wba_ref[...], preferred_element_type=jnp.float32)
        merged = gates[:, :D_MODEL] * ps + gates[:, D_MODEL:] * pa
        return jnp.dot(merged.astype(jnp.bfloat16), wo_ref[...], preferred_element_type=jnp.float32)

    def route_stage(rows, mix, count):
        h = _layer_norm(x_ref[rows, :], gi_ref[...], bi_ref[...])
        h1 = _layer_norm(DEEPNORM_ALPHA * h + mix, g1_ref[...], b1_ref[...])
        h1_ref[rows, :] = h1
        h1p_ref[rows, :] = _pack_halves(h1)

        logits = jnp.dot(h1.astype(jnp.bfloat16), wr_ref[...], preferred_element_type=jnp.float32) + br_ref[...]
        gl = jnp.where(is_group, logits, NEG_INF)
        gmax, glane = first_argmax(gl)
        g_val = 1.0 / jnp.sum(jnp.where(is_group, jnp.exp(gl - gmax), 0.0), axis=-1, keepdims=True)
        g_idx = glane - float(N_EXPERTS)
        in_group = jnp.logical_and(lane_i < N_EXPERTS, lane_group == g_idx)
        el = jnp.where(in_group, logits, NEG_INF)
        m1, i1 = first_argmax(el)
        el2 = jnp.where(lane == i1, NEG_INF, el)
        m2, i2 = first_argmax(el2)
        e2 = jnp.exp(m2 - m1)
        w1 = g_val / (1.0 + e2)
        w2 = g_val * e2 / (1.0 + e2)

        oh = jnp.where(jnp.logical_or(lane == i1, lane == i2), 1.0, 0.0)
        before = jnp.dot(tri, oh.astype(jnp.bfloat16), preferred_element_type=jnp.float32) + count
        rank1 = jnp.sum(jnp.where(lane == i1, before, 0.0), axis=-1, keepdims=True)
        rank2 = jnp.sum(jnp.where(lane == i2, before, 0.0), axis=-1, keepdims=True)

        route = jnp.where(lane_i == 0, i1, 0.0)
        route = jnp.where(lane_i == 1, i2, route)
        route = jnp.where(lane_i == 2, w1, route)
        route = jnp.where(lane_i == 3, w2, route)
        route = jnp.where(lane_i == 4, rank1, route)
        route = jnp.where(lane_i == 5, rank2, route)
        route_ref[rows, :] = route
        route_t_ref[:, rows] = route.T[0:ROUTE_ROWS, :]
        return count + jnp.sum(oh, axis=0, keepdims=True)

    n_chain = x_ref.shape[0] // sub
    rows = [slice(c * sub, (c + 1) * sub) for c in range(n_chain)]
    count = cnt_ref[...]
    mix = mix_stage(rows[0])
    for c in range(n_chain):
        mix_next = mix_stage(rows[c + 1]) if c + 1 < n_chain else None
        count = route_stage(rows[c], mix, count)
        mix = mix_next
    cnt_ref[...] = count


def _merge(x2, gi, bi, gates, ys, ya, wbs, wba, wo, g1, b1, wr, br):
    t_rows = x2.shape[0]
    tm = MERGE_TM
    assert t_rows % tm == 0 and tm % MERGE_SUB == 0
    row = lambda i: (i, 0)
    const = lambda i: (0, 0)
    return pl.pallas_call(
        _merge_kernel,
        grid=(t_rows // tm,),
        in_specs=[
            pl.BlockSpec((tm, D_MODEL), row),
            pl.BlockSpec((1, D_MODEL), const),
            pl.BlockSpec((1, D_MODEL), const),
            pl.BlockSpec((tm, GATE_COLS), row),
            pl.BlockSpec((tm, SSM_WIDTH), row),
            pl.BlockSpec((tm, ATTN_WIDTH), row),
            pl.BlockSpec((SSM_WIDTH, D_MODEL), const, pipeline_mode=pl.Buffered(1)),
            pl.BlockSpec((ATTN_WIDTH, D_MODEL), const, pipeline_mode=pl.Buffered(1)),
            pl.BlockSpec((D_MODEL, D_MODEL), const, pipeline_mode=pl.Buffered(1)),
            pl.BlockSpec((1, D_MODEL), const),
            pl.BlockSpec((1, D_MODEL), const),
            pl.BlockSpec((D_MODEL, LANES), const),
            pl.BlockSpec((1, LANES), const),
        ],
        out_specs=[
            pl.BlockSpec((tm, D_MODEL), row),
            pl.BlockSpec((tm, D_MODEL // 2), row),
            pl.BlockSpec((tm, LANES), row),
            pl.BlockSpec((ROUTE_ROWS, tm), lambda i: (0, i)),
            pl.BlockSpec((1, LANES), const),
        ],
        out_shape=[
            jax.ShapeDtypeStruct((t_rows, D_MODEL), jnp.float32),
            jax.ShapeDtypeStruct((t_rows, D_MODEL // 2), jnp.uint32),
            jax.ShapeDtypeStruct((t_rows, LANES), jnp.float32),
            jax.ShapeDtypeStruct((ROUTE_ROWS, t_rows), jnp.float32),
            jax.ShapeDtypeStruct((1, LANES), jnp.float32),
        ],
        compiler_params=pltpu.CompilerParams(
            dimension_semantics=("arbitrary",), vmem_limit_bytes=VMEM_LIMIT),
        name="merge",
    )(x2, gi, bi, gates, ys, ya, wbs, wba, wo, g1, b1, wr, br)


def _row_map_kernel(lo_ref, hi_ref, dest0_ref, dest1_ref, table_ref, *, t_rows):
    i = pl.program_id(0)
    tm = ROWMAP_TM
    stride = t_rows + ROW_BLOCK

    @pl.when(i == 0)
    def _():
        for e in range(N_EXPERTS + 1):
            def fill(r, carry):
                table_ref[r] = t_rows + (r & (ROW_BLOCK - 1))
                return carry
            lax.fori_loop(lo_ref[e], hi_ref[e], fill, 0)

    base = i * tm
    for a in range(tm):
        table_ref[dest0_ref[0, a]] = base + a
        table_ref[dest1_ref[0, a]] = base + (a + stride)


def _row_map(fill_lo, fill_hi, dest4, n_rows, t_rows):
    tm = ROWMAP_TM
    assert t_rows % tm == 0 and ROW_BLOCK & (ROW_BLOCK - 1) == 0
    grid_spec = pltpu.PrefetchScalarGridSpec(
        num_scalar_prefetch=2,
        grid=(t_rows // tm,),
        in_specs=[pl.BlockSpec((None, None, 1, tm), lambda i, lo, hi, k=k: (k, i, 0, 0), memory_space=pltpu.SMEM)
                  for k in range(TOP_K)],
        out_specs=pl.BlockSpec(memory_space=pltpu.SMEM),
    )
    return pl.pallas_call(
        functools.partial(_row_map_kernel, t_rows=t_rows),
        grid_spec=grid_spec,
        out_shape=jax.ShapeDtypeStruct((n_rows,), jnp.int32),
        compiler_params=pltpu.CompilerParams(dimension_semantics=("arbitrary",)),
        name="row_map",
    )(fill_lo, fill_hi, dest4, dest4)


def _moe_kernel(be_ref, nu_ref, src_cur_ref, src_next_ref, dst_prev_ref, h1_hbm, wg_ref, wu_ref, wd_ref,
                ysc_hbm, xbuf0, xbuf1, ybuf0, ybuf1, wgb, wub, wdb, gsem, ssem):
    i = pl.program_id(0)
    n_used = nu_ref[0]
    rb = ROW_BLOCK
    xbuf = (xbuf0, xbuf1)
    ybuf = (ybuf0, ybuf1)

    def start_gather(idx_ref, s):
        for r in range(rb):
            pltpu.make_async_copy(h1_hbm.at[pl.ds(idx_ref[0, r], 1)], xbuf[s].at[pl.ds(r, 1)],
                                  gsem.at[s]).start(priority=r % 2)

    def wait_gather(s):
        pltpu.make_async_copy(h1_hbm.at[pl.ds(0, rb)], xbuf[s], gsem.at[s]).wait()

    def start_scatter(s):
        for r in range(rb):
            pltpu.make_async_copy(ybuf[s].at[pl.ds(r, 1)], ysc_hbm.at[pl.ds(dst_prev_ref[0, r], 1)],
                                  ssem.at[0]).start(priority=r % 2)

    def wait_scatter(s):
        pltpu.make_async_copy(ybuf[s], ysc_hbm.at[pl.ds(0, rb)], ssem.at[0]).wait()

    @pl.when(i == 0)
    def _():
        ybuf1[...] = jnp.zeros_like(ybuf1)
        start_gather(src_cur_ref, 0)

    expert_changed = jnp.logical_or(i == 0, be_ref[i] != be_ref[jnp.maximum(i - 1, 0)])

    @pl.when(jnp.logical_and(i <= n_used, expert_changed))
    def _():
        wgb[...] = wg_ref[...].astype(jnp.bfloat16)
        wub[...] = wu_ref[...].astype(jnp.bfloat16)
        wdb[...] = wd_ref[...].astype(jnp.bfloat16)

    def step(slot):
        other = 1 - slot
        wait_gather(slot)

        @pl.when(i >= 1)
        def _():
            wait_scatter(slot)

        start_gather(src_next_ref, other)
        start_scatter(other)
        xb = jnp.concatenate([half.astype(jnp.bfloat16) for half in _unpack_halves(xbuf[slot][...])], axis=1)
        g = jnp.dot(xb, wgb[...], preferred_element_type=jnp.float32)
        u = jnp.dot(xb, wub[...], preferred_element_type=jnp.float32)
        hb = (jax.nn.silu(g) * u).astype(jnp.bfloat16)
        ybuf[slot][...] = _pack_halves(jnp.dot(hb, wdb[...], preferred_element_type=jnp.float32))

        @pl.when(i == n_used)
        def _():
            wait_gather(other)
            wait_scatter(other)

    for parity in range(2):
        pl.when(jnp.logical_and(i <= n_used, i % 2 == parity))(functools.partial(step, parity))


def _moe(block_exp, n_used, src_blocks, dst_blocks, h1, wg, wu, wd, n_blocks):
    t_rows = h1.shape[0]
    rb = ROW_BLOCK
    wsel = lambda i, be, nu: (be[i], 0, 0)
    smem_blk = lambda off: pl.BlockSpec((None, 1, rb), lambda i, be, nu: (i + off, 0, 0),
                                        memory_space=pltpu.SMEM)
    grid_spec = pltpu.PrefetchScalarGridSpec(
        num_scalar_prefetch=2,
        grid=(n_blocks + 1,),
        in_specs=[
            smem_blk(0), smem_blk(1), smem_blk(0),
            pl.BlockSpec(memory_space=pl.ANY),
            pl.BlockSpec((None, D_MODEL, D_FF_EXPERT), wsel),
            pl.BlockSpec((None, D_MODEL, D_FF_EXPERT), wsel),
            pl.BlockSpec((None, D_FF_EXPERT, D_MODEL), wsel),
        ],
        out_specs=pl.BlockSpec(memory_space=pl.ANY),
        scratch_shapes=[
            pltpu.VMEM((rb, D_MODEL // 2), jnp.uint32),
            pltpu.VMEM((rb, D_MODEL // 2), jnp.uint32),
            pltpu.VMEM((rb, D_MODEL // 2), jnp.uint32),
            pltpu.VMEM((rb, D_MODEL // 2), jnp.uint32),
            pltpu.VMEM((D_MODEL, D_FF_EXPERT), jnp.bfloat16),
            pltpu.VMEM((D_MODEL, D_FF_EXPERT), jnp.bfloat16),
            pltpu.VMEM((D_FF_EXPERT, D_MODEL), jnp.bfloat16),
            pltpu.SemaphoreType.DMA((2,)),
            pltpu.SemaphoreType.DMA((1,)),
        ],
    )
    return pl.pallas_call(
        _moe_kernel,
        grid_spec=grid_spec,
        out_shape=jax.ShapeDtypeStruct((TOP_K * t_rows + rb, D_MODEL // 2), jnp.uint32),
        compiler_params=pltpu.CompilerParams(
            dimension_semantics=("arbitrary",), vmem_limit_bytes=VMEM_LIMIT),
        name="experts",
    )(block_exp, n_used, src_blocks, src_blocks, dst_blocks, h1, wg, wu, wd)


def _combine_kernel(h1_ref, route_ref, y0_ref, y1_ref, g_ref, b_ref, o_ref):
    route = route_ref[...]
    y0 = _unpack_halves(y0_ref[...])
    y1 = _unpack_halves(y1_ref[...])
    ffn = jnp.concatenate([route[:, 2:3] * a + route[:, 3:4] * b for a, b in zip(y0, y1)], axis=1)
    o_ref[...] = _layer_norm(DEEPNORM_ALPHA * h1_ref[...] + ffn, g_ref[...], b_ref[...])


def _combine(h1, route, ysc, g2, b2):
    t_rows = h1.shape[0]
    tm = COMBINE_TM
    assert t_rows % tm == 0 and ROW_BLOCK % tm == 0
    second = (t_rows + ROW_BLOCK) // tm
    row = lambda i: (i, 0)
    const = lambda i: (0, 0)
    return pl.pallas_call(
        _combine_kernel,
        grid=(t_rows // tm,),
        in_specs=[
            pl.BlockSpec((tm, D_MODEL), row),
            pl.BlockSpec((tm, LANES), row),
            pl.BlockSpec((tm, D_MODEL // 2), row),
            pl.BlockSpec((tm, D_MODEL // 2), lambda i: (i + second, 0)),
            pl.BlockSpec((1, D_MODEL), const),
            pl.BlockSpec((1, D_MODEL), const),
        ],
        out_specs=pl.BlockSpec((tm, D_MODEL), row),
        out_shape=jax.ShapeDtypeStruct((t_rows, D_MODEL), jnp.float32),
        compiler_params=pltpu.CompilerParams(
            dimension_semantics=("parallel",), vmem_limit_bytes=VMEM_LIMIT),
        name="combine",
    )(h1, route, ysc, ysc, g2, b2)


def kernel(x, meta_tokens, ln_in_g, ln_in_b, w_in, ssm_a_re, ssm_a_im, ssm_log_dt, ssm_b_re, ssm_b_im,
           ssm_c_re, ssm_c_im, ssm_d, ssm_w_glu, attn_lambda_q1, attn_lambda_k1, attn_lambda_q2,
           attn_lambda_k2, attn_subln_g, w_br_ssm, w_br_attn, w_o, ln1_g, ln1_b, router_g_w, router_g_b,
           router_e_w, router_e_b, exp_w_gate, exp_w_up, exp_w_down, ln2_g, ln2_b):
    f32, bf16 = jnp.float32, jnp.bfloat16
    nb, seq, d = x.shape
    assert d == D_MODEL and w_in.shape[0] == DEPTH == 1
    t_rows = nb * seq
    l = 0
    lambda_init = 0.8 - 0.6 * math.exp(-0.3 * l)
    row2 = lambda v: v.reshape(1, -1).astype(f32)

    x2 = x.reshape(t_rows, d)
    w_u = w_in[l, :, :SSM_WIDTH].astype(bf16)
    w_rest = jnp.concatenate([w_in[l, :, SSM_WIDTH + QKV_COLS:], w_in[l, :, SSM_WIDTH:SSM_WIDTH + QKV_COLS]],
                             axis=1).astype(bf16)
    gi, bi = row2(ln_in_g), row2(ln_in_b)
    u, proj = _ln_proj(x2, gi, bi, w_u, w_rest, min(PROJ_TM, seq), GATE_COLS)
    u_meta, proj_meta = _ln_proj(meta_tokens.astype(f32), gi, bi, w_u, w_rest[:, GATE_COLS + ATTN_WIDTH:],
                                 N_META, 0)

    bb, ab, cm = _ssm_params(ssm_a_re[l], ssm_a_im[l], ssm_log_dt[l], ssm_b_re[l], ssm_b_im[l],
                             ssm_c_re[l], ssm_c_im[l])
    u_tm = u.reshape(nb, seq, SSM_WIDTH).transpose(1, 0, 2).reshape(seq * nb, SSM_WIDTH)
    u_meta_rows = jnp.repeat(u_meta, nb, axis=0)
    y_ssm_tm = _ssm(u_meta_rows, u_tm, bb, ab, cm, row2(ssm_d[l]), ssm_w_glu[l].astype(bf16), nb)
    y_ssm = y_ssm_tm.reshape(seq, nb, SSM_WIDTH).transpose(1, 0, 2).reshape(t_rows, SSM_WIDTH)

    lam = (jnp.exp(jnp.sum(attn_lambda_q1[l].astype(f32) * attn_lambda_k1[l].astype(f32)))
           - jnp.exp(jnp.sum(attn_lambda_q2[l].astype(f32) * attn_lambda_k2[l].astype(f32)))
           + lambda_init).reshape(1)
    g_scaled = row2(attn_subln_g[l]) * (1.0 - lambda_init)
    y_attn = _attention(lam, proj.reshape(nb, seq, GATE_COLS + QKV_COLS), proj_meta, g_scaled, nb, seq)
    y_attn = y_attn.reshape(t_rows, ATTN_WIDTH)

    w_r = jnp.concatenate([router_e_w[l].astype(f32), router_g_w[l].astype(f32)], axis=1)
    w_r = jnp.pad(w_r, ((0, 0), (0, LANES - w_r.shape[1])))
    b_r = jnp.concatenate([router_e_b[l].astype(f32), router_g_b[l].astype(f32)])
    b_r = jnp.pad(b_r, (0, LANES - b_r.shape[0])).reshape(1, LANES)
    h1, h1_packed, route, route_t, cnt = _merge(x2, gi, bi, proj, y_ssm, y_attn, w_br_ssm[l].astype(bf16),
                                       w_br_attn[l].astype(bf16), w_o[l].astype(bf16), row2(ln1_g[l]),
                                       row2(ln1_b[l]), w_r.astype(bf16), b_r)

    counts = cnt[0, :N_EXPERTS].astype(jnp.int32)
    padded = (counts + ROW_BLOCK - 1) // ROW_BLOCK * ROW_BLOCK
    pad_ends = jnp.cumsum(padded)
    pad_starts = pad_ends - padded
    expert = route_t[0:TOP_K].astype(jnp.int32)
    rank = route_t[4:4 + TOP_K].astype(jnp.int32)
    dest = pad_starts[expert] + rank
    n_blocks = -(-(t_rows * TOP_K + N_EXPERTS * (ROW_BLOCK - 1)) // ROW_BLOCK)
    block_start = jnp.arange(n_blocks, dtype=jnp.int32) * ROW_BLOCK
    block_exp = jnp.minimum(jnp.sum(pad_ends[None, :] <= block_start[:, None], axis=1),
                            N_EXPERTS - 1).astype(jnp.int32)
    block_exp = jnp.concatenate([block_exp, block_exp[-1:]])
    n_used = (pad_ends[-1:] // ROW_BLOCK).astype(jnp.int32)

    rb = ROW_BLOCK
    stride = t_rows + rb
    fill_lo = jnp.concatenate([pad_starts + counts, pad_ends[-1:]]).astype(jnp.int32)
    fill_hi = jnp.concatenate([pad_ends, jnp.full((1,), n_blocks * rb)]).astype(jnp.int32)
    row_map = _row_map(fill_lo, fill_hi, dest.reshape(TOP_K, t_rows // ROWMAP_TM, 1, ROWMAP_TM),
                       n_blocks * rb, t_rows)
    dst_blocks = jnp.concatenate([t_rows + jnp.arange(rb, dtype=jnp.int32), row_map]).reshape(
        n_blocks + 1, 1, rb)
    src = jnp.minimum(row_map % stride, t_rows - 1)
    src_blocks = jnp.concatenate([src, jnp.zeros((2 * rb,), jnp.int32)]).reshape(n_blocks + 2, 1, rb)

    ysc = _moe(block_exp, n_used, src_blocks, dst_blocks, h1_packed, exp_w_gate[l], exp_w_up[l],
               exp_w_down[l], n_blocks)
    out = _combine(h1, route, ysc, row2(ln2_g[l]), row2(ln2_b[l]))
    return out.reshape(nb, seq, d).astype(x.dtype)
```

```python
import functools
import math

import jax
import jax.numpy as jnp
from jax import lax
from jax.experimental import pallas as pl
from jax.experimental.pallas import tpu as pltpu

D_MODEL = 2048
N_META = 16
CHUNK = 64
ATTN_HEADS = 8
ATTN_HEAD_DIM = 64
ATTN_VDIM = 2 * ATTN_HEAD_DIM
ATTN_WIDTH = ATTN_HEADS * ATTN_VDIM
SSM_WIDTH = D_MODEL // 4
SSM_GROUP = 16
SSM_GROUPS = SSM_WIDTH // SSM_GROUP
SSM_STATE = 64
SSM_COLS = SSM_GROUPS * SSM_STATE
N_EXPERT_GROUPS = 4
EXPERTS_PER_GROUP = 8
N_EXPERTS = N_EXPERT_GROUPS * EXPERTS_PER_GROUP
TOP_K = 2
D_FF_EXPERT = D_MODEL // 4
QKV_COLS = 3 * ATTN_WIDTH
GATE_COLS = 2 * D_MODEL
IN_COLS = SSM_WIDTH + QKV_COLS + GATE_COLS
LN_EPS = 1e-5
DEPTH = 1
DEEPNORM_ALPHA = (2.0 * DEPTH) ** 0.25
LANES = 128
NEG_INF = -1e30

PROJ_TN = 1024
PROJ_TM = 1024
SSM_TC = 32
SSM_CB = 512
SSM_HALVES = 2
ATTN_TQ = 256
MERGE_TM = 256
MERGE_SUB = 256
ROW_BLOCK = 256
COMBINE_TM = 256
ROWMAP_TM = 512
ROUTE_ROWS = 8
VMEM_LIMIT = 56 * 1024 * 1024


def _pack_halves(x):
    n = x.shape[1] // 2
    bits = lambda v: lax.bitcast_convert_type(v.astype(jnp.bfloat16).astype(jnp.float32), jnp.uint32)
    return bits(x[:, :n]) | (bits(x[:, n:]) >> 16)


def _unpack_halves(w):
    hi = lax.bitcast_convert_type(w & jnp.uint32(0xFFFF0000), jnp.float32)
    lo = lax.bitcast_convert_type(w << 16, jnp.float32)
    return hi, lo


def _layer_norm(x, g, b):
    mu = jnp.mean(x, axis=-1, keepdims=True)
    xc = x - mu
    var = jnp.mean(xc * xc, axis=-1, keepdims=True)
    return xc * lax.rsqrt(var + LN_EPS) * g + b


def _ln_proj_kernel(x_ref, g_ref, b_ref, wu_ref, w_ref, u_ref, proj_ref, xn_ref, *, n_gate_blocks):
    j = pl.program_id(1)

    @pl.when(j == 0)
    def _():
        xn = _layer_norm(x_ref[...], g_ref[...], b_ref[...]).astype(jnp.bfloat16)
        xn_ref[...] = xn
        u_ref[...] = jnp.dot(xn, wu_ref[...], preferred_element_type=jnp.float32)

    acc = jnp.dot(xn_ref[...], w_ref[...], preferred_element_type=jnp.float32)
    if n_gate_blocks:
        acc = jnp.where(j < n_gate_blocks, jax.nn.sigmoid(acc), acc)
    proj_ref[...] = acc.astype(jnp.bfloat16)


def _ln_proj(x2, g, b, w_u, w_rest, tm, gate_cols, first_col=0, cols=None):
    t_rows = x2.shape[0]
    cols = w_rest.shape[1] - first_col if cols is None else cols
    n_col = cols // PROJ_TN
    first_blk = first_col // PROJ_TN
    assert cols % PROJ_TN == 0 and gate_cols % PROJ_TN == 0 and first_col % PROJ_TN == 0 and t_rows % tm == 0
    return pl.pallas_call(
        functools.partial(_ln_proj_kernel, n_gate_blocks=gate_cols // PROJ_TN),
        grid=(t_rows // tm, n_col),
        in_specs=[
            pl.BlockSpec((tm, D_MODEL), lambda i, j: (i, 0)),
            pl.BlockSpec((1, D_MODEL), lambda i, j: (0, 0)),
            pl.BlockSpec((1, D_MODEL), lambda i, j: (0, 0)),
            pl.BlockSpec((D_MODEL, SSM_WIDTH), lambda i, j: (0, 0)),
            pl.BlockSpec((D_MODEL, PROJ_TN), lambda i, j: (0, j + first_blk)),
        ],
        out_specs=[
            pl.BlockSpec((tm, SSM_WIDTH), lambda i, j: (i, 0)),
            pl.BlockSpec((tm, PROJ_TN), lambda i, j: (i, j)),
        ],
        out_shape=[
            jax.ShapeDtypeStruct((t_rows, SSM_WIDTH), jnp.float32),
            jax.ShapeDtypeStruct((t_rows, cols), jnp.bfloat16),
        ],
        scratch_shapes=[pltpu.VMEM((tm, D_MODEL), jnp.bfloat16)],
        compiler_params=pltpu.CompilerParams(
            dimension_semantics=("parallel", "arbitrary"), vmem_limit_bytes=VMEM_LIMIT),
        name="ln_proj",
    )(x2, g, b, w_u, w_rest)


def _ssm_kernel(um_ref, u_ref, bb_ref, ab_ref, cm_ref, d_ref, wglu_ref, y_ref, state_ref, bu_ref, *, nb):
    half_u = SSM_WIDTH // SSM_HALVES
    half_c = SSM_COLS // SSM_HALVES

    def expand_and_scan(u_bf16, n_steps):
        rows = n_steps * nb
        for h in range(SSM_HALVES):
            bu_ref[pl.ds(0, rows), 2 * half_c * h:2 * half_c * (h + 1)] = jnp.dot(
                u_bf16[:, half_u * h:half_u * (h + 1)], bb_ref[h], preferred_element_type=jnp.float32)
        for cb in range(SSM_COLS // SSM_CB):
            h, off = divmod(cb * SSM_CB, half_c)
            re_cols = pl.ds(2 * half_c * h + off, SSM_CB)
            im_cols = pl.ds(2 * half_c * h + half_c + off, SSM_CB)
            a_re = ab_ref[0, :, cb * SSM_CB:(cb + 1) * SSM_CB]
            a_im = ab_ref[1, :, cb * SSM_CB:(cb + 1) * SSM_CB]
            groups = [pl.ds(8 * g, 8) for g in range(nb // 8)]

            def step(t, carry):
                r0 = pl.multiple_of(t * nb, nb)
                out = []
                for g, (s_re, s_im) in enumerate(carry):
                    rows = pl.ds(r0 + 8 * g, 8)
                    n_re = a_re * s_re - a_im * s_im + bu_ref[rows, re_cols]
                    n_im = a_re * s_im + a_im * s_re + bu_ref[rows, im_cols]
                    bu_ref[rows, re_cols] = n_re
                    bu_ref[rows, im_cols] = n_im
                    out.append((n_re, n_im))
                return tuple(out)

            init = tuple((state_ref[g, re_cols], state_ref[g, im_cols]) for g in groups)
            final = lax.fori_loop(0, n_steps, step, init, unroll=2)
            for g, (s_re, s_im) in zip(groups, final):
                state_ref[g, re_cols] = s_re
                state_ref[g, im_cols] = s_im

    @pl.when(pl.program_id(0) == 0)
    def _():
        state_ref[...] = jnp.zeros_like(state_ref)
        expand_and_scan(um_ref[...].astype(jnp.bfloat16), N_META)

    u = u_ref[...]
    expand_and_scan(u.astype(jnp.bfloat16), SSM_TC)
    y = jnp.concatenate(
        [jnp.dot(bu_ref[:, 2 * half_c * h:2 * half_c * (h + 1)].astype(jnp.bfloat16), cm_ref[h],
                 preferred_element_type=jnp.float32) for h in range(SSM_HALVES)], axis=1)
    y = jax.nn.gelu(y + d_ref[...] * u)
    gate = jnp.dot(y.astype(jnp.bfloat16), wglu_ref[...], preferred_element_type=jnp.float32)
    y_ref[...] = (y * jax.nn.sigmoid(gate)).astype(y_ref.dtype)


def _ssm(u_meta_rows, u_tm, bb, ab, cm, d, wglu, nb):
    rows = u_tm.shape[0]
    blk = SSM_TC * nb
    assert rows % blk == 0 and nb % 8 == 0 and N_META <= SSM_TC
    const = lambda i: (0, 0)
    return pl.pallas_call(
        functools.partial(_ssm_kernel, nb=nb),
        grid=(rows // blk,),
        in_specs=[
            pl.BlockSpec((N_META * nb, SSM_WIDTH), const),
            pl.BlockSpec((blk, SSM_WIDTH), lambda i: (i, 0)),
            pl.BlockSpec((SSM_HALVES, SSM_WIDTH // SSM_HALVES, 2 * SSM_COLS // SSM_HALVES), lambda i: (0, 0, 0)),
            pl.BlockSpec((2, 8, SSM_COLS), lambda i: (0, 0, 0)),
            pl.BlockSpec((SSM_HALVES, 2 * SSM_COLS // SSM_HALVES, SSM_WIDTH // SSM_HALVES), lambda i: (0, 0, 0)),
            pl.BlockSpec((1, SSM_WIDTH), const),
            pl.BlockSpec((SSM_WIDTH, SSM_WIDTH), const),
        ],
        out_specs=pl.BlockSpec((blk, SSM_WIDTH), lambda i: (i, 0)),
        out_shape=jax.ShapeDtypeStruct((rows, SSM_WIDTH), jnp.bfloat16),
        scratch_shapes=[
            pltpu.VMEM((nb, 2 * SSM_COLS), jnp.float32),
            pltpu.VMEM((blk, 2 * SSM_COLS), jnp.float32),
        ],
        compiler_params=pltpu.CompilerParams(
            dimension_semantics=("arbitrary",), vmem_limit_bytes=VMEM_LIMIT),
        name="ssm",
    )(u_meta_rows, u_tm, bb, ab, cm, d, wglu)


def _ssm_params(a_re, a_im, log_dt, b_re, b_im, c_re, c_im):
    f32 = jnp.float32
    lam_re = jnp.minimum(a_re.astype(f32), -1e-4)
    lam_im = a_im.astype(f32)
    dt = jnp.exp(log_dt.astype(f32))[:, None]
    mag = jnp.exp(lam_re * dt)
    ab_re = mag * jnp.cos(lam_im * dt)
    ab_im = mag * jnp.sin(lam_im * dt)
    den = lam_re * lam_re + lam_im * lam_im
    nr = ab_re - 1.0
    ni = ab_im
    z_re = (nr * lam_re + ni * lam_im) / den
    z_im = (ni * lam_re - nr * lam_im) / den
    br32 = b_re.astype(f32)
    bi32 = b_im.astype(f32)
    bb_re = z_re[..., None] * br32 - z_im[..., None] * bi32
    bb_im = z_re[..., None] * bi32 + z_im[..., None] * br32
    gh = SSM_GROUPS // SSM_HALVES
    eye = jnp.eye(gh, dtype=f32)
    split = lambda a: a.reshape((SSM_HALVES, gh) + a.shape[1:])
    exp_re = jnp.einsum('kgpc,gh->kgchp', split(bb_re), eye).reshape(SSM_HALVES, gh * SSM_GROUP, gh * SSM_STATE)
    exp_im = jnp.einsum('kgpc,gh->kgchp', split(bb_im), eye).reshape(SSM_HALVES, gh * SSM_GROUP, gh * SSM_STATE)
    bb = jnp.concatenate([exp_re, exp_im], axis=2)
    ro_re = jnp.einsum('kgcp,gh->kgphc', split(c_re.astype(f32)), eye).reshape(
        SSM_HALVES, gh * SSM_STATE, gh * SSM_GROUP)
    ro_im = jnp.einsum('kgcp,gh->kgphc', split(c_im.astype(f32)), eye).reshape(
        SSM_HALVES, gh * SSM_STATE, gh * SSM_GROUP)
    cm = jnp.concatenate([ro_re, -ro_im], axis=1)
    ab = jnp.stack([ab_re.reshape(SSM_COLS), ab_im.reshape(SSM_COLS)], axis=0)
    ab = jnp.broadcast_to(ab[:, None, :], (2, 8, SSM_COLS))
    return bb.astype(jnp.bfloat16), ab, cm.astype(jnp.bfloat16)


def _attn_kernel(lam_ref, q_ref, k_ref, v_ref, km_ref, vm_ref, g_ref, o_ref, vt_ref, *, scale, seq):
    tq = ATTN_TQ
    f32, bf16 = jnp.float32, jnp.bfloat16
    vt_ref[...] = v_ref[...].astype(f32).T.astype(bf16)
    vm_t = vm_ref[...].astype(f32).T.astype(bf16)
    k_meta = km_ref[...]
    lane = lax.broadcasted_iota(jnp.int32, (tq, ATTN_VDIM), 1)
    key = lax.broadcasted_iota(jnp.int32, (tq, 2 * tq), 0)
    qry = lax.broadcasted_iota(jnp.int32, (tq, 2 * tq), 1) % tq
    diag_visible = qry // CHUNK >= key // CHUNK
    col_max = lambda a: jnp.max(a, axis=0, keepdims=True)
    col_sum = lambda a: jnp.sum(a, axis=0, keepdims=True)

    def scores(i):
        rows = slice(i * tq, (i + 1) * tq)
        q = q_ref[rows, :] * scale
        zero = jnp.zeros_like(q)
        qq = jnp.concatenate([jnp.where(lane < ATTN_HEAD_DIM, q, zero),
                              jnp.where(lane >= ATTN_HEAD_DIM, q, zero)], axis=0)
        scores_t = lambda kb: lax.dot_general(kb, qq, (((1,), (1,)), ((), ())), preferred_element_type=f32)
        s_meta = scores_t(k_meta)
        s_diag = jnp.where(diag_visible, scores_t(k_ref[rows, :]), NEG_INF)
        s_prev = scores_t(k_ref[0:i * tq, :]) if i > 0 else None
        return s_meta, s_diag, s_prev

    def softmax(i, s):
        s_meta, s_diag, s_prev = s
        m = jnp.maximum(col_max(s_meta), col_max(s_diag))
        if i > 0:
            m = jnp.maximum(m, col_max(s_prev))
        p_meta = jnp.exp(s_meta - m)
        p_diag = jnp.exp(s_diag - m)
        l = col_sum(p_meta) + col_sum(p_diag)
        p_prev = None
        if i > 0:
            p_prev = jnp.exp(s_prev - m)
            l = l + col_sum(p_prev)
            p_prev = p_prev.astype(bf16)
        return p_meta.astype(bf16), p_diag.astype(bf16), p_prev, l

    def weighted_values(i, p):
        p_meta, p_diag, p_prev, l = p
        rows = slice(i * tq, (i + 1) * tq)
        acc = (jnp.dot(vm_t, p_meta, preferred_element_type=f32)
               + jnp.dot(vt_ref[:, rows], p_diag, preferred_element_type=f32))
        if i > 0:
            acc = acc + jnp.dot(vt_ref[:, 0:i * tq], p_prev, preferred_element_type=f32)
        o_all = acc / l
        o_t = o_all[:, :tq] - lam_ref[0] * o_all[:, tq:]
        o_t = o_t * lax.rsqrt(jnp.mean(o_t * o_t, axis=0, keepdims=True) + LN_EPS)
        o_ref[rows, :] = (o_t.T * g_ref[...]).astype(o_ref.dtype)

    n_blk = seq // tq
    s_next = scores(0)
    p_last = None
    for i in range(n_blk):
        s_cur = s_next
        s_next = scores(i + 1) if i + 1 < n_blk else None
        if p_last is not None:
            weighted_values(i - 1, p_last)
        p_last = softmax(i, s_cur)
    weighted_values(n_blk - 1, p_last)


def _attention(lam, proj, proj_meta, g_scaled, nb, seq):
    assert seq % ATTN_TQ == 0
    nh = ATTN_HEADS
    first = GATE_COLS // ATTN_VDIM
    head = lambda part: pl.BlockSpec((None, seq, ATTN_VDIM), lambda b, h: (b, 0, first + part * nh + h))
    return pl.pallas_call(
        functools.partial(_attn_kernel, scale=ATTN_HEAD_DIM ** -0.5, seq=seq),
        grid=(nb, nh),
        in_specs=[
            pl.BlockSpec(memory_space=pltpu.SMEM),
            head(0), head(1), head(2),
            pl.BlockSpec((N_META, ATTN_VDIM), lambda b, h: (0, h)),
            pl.BlockSpec((N_META, ATTN_VDIM), lambda b, h: (0, nh + h)),
            pl.BlockSpec((1, ATTN_VDIM), lambda b, h: (0, 0)),
        ],
        out_specs=pl.BlockSpec((None, seq, ATTN_VDIM), lambda b, h: (b, 0, h)),
        out_shape=jax.ShapeDtypeStruct((nb, seq, ATTN_WIDTH), jnp.bfloat16),
        scratch_shapes=[pltpu.VMEM((ATTN_VDIM, seq), jnp.bfloat16)],
        compiler_params=pltpu.CompilerParams(
            dimension_semantics=("parallel", "parallel"), vmem_limit_bytes=VMEM_LIMIT),
        name="attention",
    )(lam, proj, proj, proj, proj_meta, proj_meta, g_scaled)


def _merge_kernel(x_ref, gi_ref, bi_ref, gate_ref, ys_ref, ya_ref, wbs_ref, wba_ref, wo_ref,
                  g1_ref, b1_ref, wr_ref, br_ref, h1_ref, h1p_ref, route_ref, route_t_ref, cnt_ref):
    sub = MERGE_SUB

    @pl.when(pl.program_id(0) == 0)
    def _():
        cnt_ref[...] = jnp.zeros_like(cnt_ref)

    lane_i = lax.broadcasted_iota(jnp.int32, (sub, LANES), 1)
    lane = lane_i.astype(jnp.float32)
    is_group = jnp.logical_and(lane_i >= N_EXPERTS, lane_i < N_EXPERTS + N_EXPERT_GROUPS)
    lane_group = (lane_i // EXPERTS_PER_GROUP).astype(jnp.float32)
    r_i = lax.broadcasted_iota(jnp.int32, (sub, sub), 0)
    c_i = lax.broadcasted_iota(jnp.int32, (sub, sub), 1)
    tri = jnp.where(c_i < r_i, 1.0, 0.0).astype(jnp.bfloat16)

    def first_argmax(vals):
        mx = jnp.max(vals, axis=-1, keepdims=True)
        idx = jnp.min(jnp.where(vals == mx, lane, float(LANES)), axis=-1, keepdims=True)
        return mx, idx

    def mix_stage(rows):
        gates = gate_ref[rows, :].astype(jnp.float32)
        ps = jnp.dot(ys_ref[rows, :], wbs_ref[...], preferred_element_type=jnp.float32)
        pa = jnp.dot(ya_ref[rows, :], wba_ref[...], preferred_element_type=jnp.float32)
        merged = gates[:, :D_MODEL] * ps + gates[:, D_MODEL:] * pa
        return jnp.dot(merged.astype(jnp.bfloat16), wo_ref[...], preferred_element_type=jnp.float32)

    def route_stage(rows, mix, count):
        h = _layer_norm(x_ref[rows, :], gi_ref[...], bi_ref[...])
        h1 = _layer_norm(DEEPNORM_ALPHA * h + mix, g1_ref[...], b1_ref[...])
        h1_ref[rows, :] = h1
        h1p_ref[rows, :] = _pack_halves(h1)

        logits = jnp.dot(h1.astype(jnp.bfloat16), wr_ref[...], preferred_element_type=jnp.float32) + br_ref[...]
        gl = jnp.where(is_group, logits, NEG_INF)
        gmax, glane = first_argmax(gl)
        g_val = 1.0 / jnp.sum(jnp.where(is_group, jnp.exp(gl - gmax), 0.0), axis=-1, keepdims=True)
        g_idx = glane - float(N_EXPERTS)
        in_group = jnp.logical_and(lane_i < N_EXPERTS, lane_group == g_idx)
        el = jnp.where(in_group, logits, NEG_INF)
        m1, i1 = first_argmax(el)
        el2 = jnp.where(lane == i1, NEG_INF, el)
        m2, i2 = first_argmax(el2)
        e2 = jnp.exp(m2 - m1)
        w1 = g_val / (1.0 + e2)
        w2 = g_val * e2 / (1.0 + e2)

        oh = jnp.where(jnp.logical_or(lane == i1, lane == i2), 1.0, 0.0)
        before = jnp.dot(tri, oh.astype(jnp.bfloat16), preferred_element_type=jnp.float32) + count
        rank1 = jnp.sum(jnp.where(lane == i1, before, 0.0), axis=-1, keepdims=True)
        rank2 = jnp.sum(jnp.where(lane == i2, before, 0.0), axis=-1, keepdims=True)

        route = jnp.where(lane_i == 0, i1, 0.0)
        route = jnp.where(lane_i == 1, i2, route)
        route = jnp.where(lane_i == 2, w1, route)
        route = jnp.where(lane_i == 3, w2, route)
        route = jnp.where(lane_i == 4, rank1, route)
        route = jnp.where(lane_i == 5, rank2, route)
        route_ref[rows, :] = route
        route_t_ref[:, rows] = route.T[0:ROUTE_ROWS, :]
        return count + jnp.sum(oh, axis=0, keepdims=True)

    n_chain = x_ref.shape[0] // sub
    rows = [slice(c * sub, (c + 1) * sub) for c in range(n_chain)]
    count = cnt_ref[...]
    mix = mix_stage(rows[0])
    for c in range(n_chain):
        mix_next = mix_stage(rows[c + 1]) if c + 1 < n_chain else None
        count = route_stage(rows[c], mix, count)
        mix = mix_next
    cnt_ref[...] = count


def _merge(x2, gi, bi, gates, ys, ya, wbs, wba, wo, g1, b1, wr, br):
    t_rows = x2.shape[0]
    tm = MERGE_TM
    assert t_rows % tm == 0 and tm % MERGE_SUB == 0
    row = lambda i: (i, 0)
    const = lambda i: (0, 0)
    return pl.pallas_call(
        _merge_kernel,
        grid=(t_rows // tm,),
        in_specs=[
            pl.BlockSpec((tm, D_MODEL), row),
            pl.BlockSpec((1, D_MODEL), const),
            pl.BlockSpec((1, D_MODEL), const),
            pl.BlockSpec((tm, GATE_COLS), row),
            pl.BlockSpec((tm, SSM_WIDTH), row),
            pl.BlockSpec((tm, ATTN_WIDTH), row),
            pl.BlockSpec((SSM_WIDTH, D_MODEL), const, pipeline_mode=pl.Buffered(1)),
            pl.BlockSpec((ATTN_WIDTH, D_MODEL), const, pipeline_mode=pl.Buffered(1)),
            pl.BlockSpec((D_MODEL, D_MODEL), const, pipeline_mode=pl.Buffered(1)),
            pl.BlockSpec((1, D_MODEL), const),
            pl.BlockSpec((1, D_MODEL), const),
            pl.BlockSpec((D_MODEL, LANES), const),
            pl.BlockSpec((1, LANES), const),
        ],
        out_specs=[
            pl.BlockSpec((tm, D_MODEL), row),
            pl.BlockSpec((tm, D_MODEL // 2), row),
            pl.BlockSpec((tm, LANES), row),
            pl.BlockSpec((ROUTE_ROWS, tm), lambda i: (0, i)),
            pl.BlockSpec((1, LANES), const),
        ],
        out_shape=[
            jax.ShapeDtypeStruct((t_rows, D_MODEL), jnp.float32),
            jax.ShapeDtypeStruct((t_rows, D_MODEL // 2), jnp.uint32),
            jax.ShapeDtypeStruct((t_rows, LANES), jnp.float32),
            jax.ShapeDtypeStruct((ROUTE_ROWS, t_rows), jnp.float32),
            jax.ShapeDtypeStruct((1, LANES), jnp.float32),
        ],
        compiler_params=pltpu.CompilerParams(
            dimension_semantics=("arbitrary",), vmem_limit_bytes=VMEM_LIMIT),
        name="merge",
    )(x2, gi, bi, gates, ys, ya, wbs, wba, wo, g1, b1, wr, br)


def _row_map_kernel(lo_ref, hi_ref, dest0_ref, dest1_ref, table_ref, *, t_rows):
    i = pl.program_id(0)
    tm = ROWMAP_TM
    stride = t_rows + ROW_BLOCK

    @pl.when(i == 0)
    def _():
        for e in range(N_EXPERTS + 1):
            def fill(r, carry):
                table_ref[r] = t_rows + (r & (ROW_BLOCK - 1))
                return carry
            lax.fori_loop(lo_ref[e], hi_ref[e], fill, 0)

    base = i * tm
    for a in range(tm):
        table_ref[dest0_ref[0, a]] = base + a
        table_ref[dest1_ref[0, a]] = base + (a + stride)


def _row_map(fill_lo, fill_hi, dest4, n_rows, t_rows):
    tm = ROWMAP_TM
    assert t_rows % tm == 0 and ROW_BLOCK & (ROW_BLOCK - 1) == 0
    grid_spec = pltpu.PrefetchScalarGridSpec(
        num_scalar_prefetch=2,
        grid=(t_rows // tm,),
        in_specs=[pl.BlockSpec((None, None, 1, tm), lambda i, lo, hi, k=k: (k, i, 0, 0), memory_space=pltpu.SMEM)
                  for k in range(TOP_K)],
        out_specs=pl.BlockSpec(memory_space=pltpu.SMEM),
    )
    return pl.pallas_call(
        functools.partial(_row_map_kernel, t_rows=t_rows),
        grid_spec=grid_spec,
        out_shape=jax.ShapeDtypeStruct((n_rows,), jnp.int32),
        compiler_params=pltpu.CompilerParams(dimension_semantics=("arbitrary",)),
        name="row_map",
    )(fill_lo, fill_hi, dest4, dest4)


def _moe_kernel(be_ref, nu_ref, src_cur_ref, src_next_ref, dst_prev_ref, h1_hbm, wg_ref, wu_ref, wd_ref,
                ysc_hbm, xbuf0, xbuf1, ybuf0, ybuf1, wgb, wub, wdb, gsem, ssem):
    i = pl.program_id(0)
    n_used = nu_ref[0]
    rb = ROW_BLOCK
    xbuf = (xbuf0, xbuf1)
    ybuf = (ybuf0, ybuf1)

    def start_gather(idx_ref, s):
        for r in range(rb):
            pltpu.make_async_copy(h1_hbm.at[pl.ds(idx_ref[0, r], 1)], xbuf[s].at[pl.ds(r, 1)],
                                  gsem.at[s]).start(priority=r % 2)

    def wait_gather(s):
        pltpu.make_async_copy(h1_hbm.at[pl.ds(0, rb)], xbuf[s], gsem.at[s]).wait()

    def start_scatter(s):
        for r in range(rb):
            pltpu.make_async_copy(ybuf[s].at[pl.ds(r, 1)], ysc_hbm.at[pl.ds(dst_prev_ref[0, r], 1)],
                                  ssem.at[0]).start(priority=r % 2)

    def wait_scatter(s):
        pltpu.make_async_copy(ybuf[s], ysc_hbm.at[pl.ds(0, rb)], ssem.at[0]).wait()

    @pl.when(i == 0)
    def _():
        ybuf1[...] = jnp.zeros_like(ybuf1)
        start_gather(src_cur_ref, 0)

    expert_changed = jnp.logical_or(i == 0, be_ref[i] != be_ref[jnp.maximum(i - 1, 0)])

    @pl.when(jnp.logical_and(i <= n_used, expert_changed))
    def _():
        wgb[...] = wg_ref[...].astype(jnp.bfloat16)
        wub[...] = wu_ref[...].astype(jnp.bfloat16)
        wdb[...] = wd_ref[...].astype(jnp.bfloat16)

    def step(slot):
        other = 1 - slot
        wait_gather(slot)

        @pl.when(i >= 1)
        def _():
            wait_scatter(slot)

        start_gather(src_next_ref, other)
        start_scatter(other)
        xb = jnp.concatenate([half.astype(jnp.bfloat16) for half in _unpack_halves(xbuf[slot][...])], axis=1)
        g = jnp.dot(xb, wgb[...], preferred_element_type=jnp.float32)
        u = jnp.dot(xb, wub[...], preferred_element_type=jnp.float32)
        hb = (jax.nn.silu(g) * u).astype(jnp.bfloat16)
        ybuf[slot][...] = _pack_halves(jnp.dot(hb, wdb[...], preferred_element_type=jnp.float32))

        @pl.when(i == n_used)
        def _():
            wait_gather(other)
            wait_scatter(other)

    for parity in range(2):
        pl.when(jnp.logical_and(i <= n_used, i % 2 == parity))(functools.partial(step, parity))


def _moe(block_exp, n_used, src_blocks, dst_blocks, h1, wg, wu, wd, n_blocks):
    t_rows = h1.shape[0]
    rb = ROW_BLOCK
    wsel = lambda i, be, nu: (be[i], 0, 0)
    smem_blk = lambda off: pl.BlockSpec((None, 1, rb), lambda i, be, nu: (i + off, 0, 0),
                                        memory_space=pltpu.SMEM)
    grid_spec = pltpu.PrefetchScalarGridSpec(
        num_scalar_prefetch=2,
        grid=(n_blocks + 1,),
        in_specs=[
            smem_blk(0), smem_blk(1), smem_blk(0),
            pl.BlockSpec(memory_space=pl.ANY),
            pl.BlockSpec((None, D_MODEL, D_FF_EXPERT), wsel),
            pl.BlockSpec((None, D_MODEL, D_FF_EXPERT), wsel),
            pl.BlockSpec((None, D_FF_EXPERT, D_MODEL), wsel),
        ],
        out_specs=pl.BlockSpec(memory_space=pl.ANY),
        scratch_shapes=[
            pltpu.VMEM((rb, D_MODEL // 2), jnp.uint32),
            pltpu.VMEM((rb, D_MODEL // 2), jnp.uint32),
            pltpu.VMEM((rb, D_MODEL // 2), jnp.uint32),
            pltpu.VMEM((rb, D_MODEL // 2), jnp.uint32),
            pltpu.VMEM((D_MODEL, D_FF_EXPERT), jnp.bfloat16),
            pltpu.VMEM((D_MODEL, D_FF_EXPERT), jnp.bfloat16),
            pltpu.VMEM((D_FF_EXPERT, D_MODEL), jnp.bfloat16),
            pltpu.SemaphoreType.DMA((2,)),
            pltpu.SemaphoreType.DMA((1,)),
        ],
    )
    return pl.pallas_call(
        _moe_kernel,
        grid_spec=grid_spec,
        out_shape=jax.ShapeDtypeStruct((TOP_K * t_rows + rb, D_MODEL // 2), jnp.uint32),
        compiler_params=pltpu.CompilerParams(
            dimension_semantics=("arbitrary",), vmem_limit_bytes=VMEM_LIMIT),
        name="experts",
    )(block_exp, n_used, src_blocks, src_blocks, dst_blocks, h1, wg, wu, wd)


def _combine_kernel(h1_ref, route_ref, y0_ref, y1_ref, g_ref, b_ref, o_ref):
    route = route_ref[...]
    y0 = _unpack_halves(y0_ref[...])
    y1 = _unpack_halves(y1_ref[...])
    ffn = jnp.concatenate([route[:, 2:3] * a + route[:, 3:4] * b for a, b in zip(y0, y1)], axis=1)
    o_ref[...] = _layer_norm(DEEPNORM_ALPHA * h1_ref[...] + ffn, g_ref[...], b_ref[...])


def _combine(h1, route, ysc, g2, b2):
    t_rows = h1.shape[0]
    tm = COMBINE_TM
    assert t_rows % tm == 0 and ROW_BLOCK % tm == 0
    second = (t_rows + ROW_BLOCK) // tm
    row = lambda i: (i, 0)
    const = lambda i: (0, 0)
    return pl.pallas_call(
        _combine_kernel,
        grid=(t_rows // tm,),
        in_specs=[
            pl.BlockSpec((tm, D_MODEL), row),
            pl.BlockSpec((tm, LANES), row),
            pl.BlockSpec((tm, D_MODEL // 2), row),
            pl.BlockSpec((tm, D_MODEL // 2), lambda i: (i + second, 0)),
            pl.BlockSpec((1, D_MODEL), const),
            pl.BlockSpec((1, D_MODEL), const),
        ],
        out_specs=pl.BlockSpec((tm, D_MODEL), row),
        out_shape=jax.ShapeDtypeStruct((t_rows, D_MODEL), jnp.float32),
        compiler_params=pltpu.CompilerParams(
            dimension_semantics=("parallel",), vmem_limit_bytes=VMEM_LIMIT),
        name="combine",
    )(h1, route, ysc, ysc, g2, b2)


def kernel(x, meta_tokens, ln_in_g, ln_in_b, w_in, ssm_a_re, ssm_a_im, ssm_log_dt, ssm_b_re, ssm_b_im,
           ssm_c_re, ssm_c_im, ssm_d, ssm_w_glu, attn_lambda_q1, attn_lambda_k1, attn_lambda_q2,
           attn_lambda_k2, attn_subln_g, w_br_ssm, w_br_attn, w_o, ln1_g, ln1_b, router_g_w, router_g_b,
           router_e_w, router_e_b, exp_w_gate, exp_w_up, exp_w_down, ln2_g, ln2_b):
    f32, bf16 = jnp.float32, jnp.bfloat16
    nb, seq, d = x.shape
    assert d == D_MODEL and w_in.shape[0] == DEPTH == 1
    t_rows = nb * seq
    l = 0
    lambda_init = 0.8 - 0.6 * math.exp(-0.3 * l)
    row2 = lambda v: v.reshape(1, -1).astype(f32)

    x2 = x.reshape(t_rows, d)
    w_u = w_in[l, :, :SSM_WIDTH].astype(bf16)
    w_rest = jnp.concatenate([w_in[l, :, SSM_WIDTH + QKV_COLS:], w_in[l, :, SSM_WIDTH:SSM_WIDTH + QKV_COLS]],
                             axis=1).astype(bf16)
    gi, bi = row2(ln_in_g), row2(ln_in_b)
    u, proj = _ln_proj(x2, gi, bi, w_u, w_rest, min(PROJ_TM, seq), GATE_COLS)
    u_meta, proj_meta = _ln_proj(meta_tokens.astype(f32), gi, bi, w_u, w_rest, N_META, 0,
                                 first_col=GATE_COLS + ATTN_WIDTH)

    bb, ab, cm = _ssm_params(ssm_a_re[l], ssm_a_im[l], ssm_log_dt[l], ssm_b_re[l], ssm_b_im[l],
                             ssm_c_re[l], ssm_c_im[l])
    u_tm = u.reshape(nb, seq, SSM_WIDTH).transpose(1, 0, 2).reshape(seq * nb, SSM_WIDTH)
    u_meta_rows = jnp.repeat(u_meta, nb, axis=0)
    y_ssm_tm = _ssm(u_meta_rows, u_tm, bb, ab, cm, row2(ssm_d[l]), ssm_w_glu[l].astype(bf16), nb)
    y_ssm = y_ssm_tm.reshape(seq, nb, SSM_WIDTH).transpose(1, 0, 2).reshape(t_rows, SSM_WIDTH)

    lam = (jnp.exp(jnp.sum(attn_lambda_q1[l].astype(f32) * attn_lambda_k1[l].astype(f32)))
           - jnp.exp(jnp.sum(attn_lambda_q2[l].astype(f32) * attn_lambda_k2[l].astype(f32)))
           + lambda_init).reshape(1)
    g_scaled = row2(attn_subln_g[l]) * (1.0 - lambda_init)
    y_attn = _attention(lam, proj.reshape(nb, seq, GATE_COLS + QKV_COLS), proj_meta, g_scaled, nb, seq)
    y_attn = y_attn.reshape(t_rows, ATTN_WIDTH)

    w_r = jnp.concatenate([router_e_w[l].astype(f32), router_g_w[l].astype(f32)], axis=1)
    w_r = jnp.pad(w_r, ((0, 0), (0, LANES - w_r.shape[1])))
    b_r = jnp.concatenate([router_e_b[l].astype(f32), router_g_b[l].astype(f32)])
    b_r = jnp.pad(b_r, (0, LANES - b_r.shape[0])).reshape(1, LANES)
    h1, h1_packed, route, route_t, cnt = _merge(x2, gi, bi, proj, y_ssm, y_attn, w_br_ssm[l].astype(bf16),
                                       w_br_attn[l].astype(bf16), w_o[l].astype(bf16), row2(ln1_g[l]),
                                       row2(ln1_b[l]), w_r.astype(bf16), b_r)

    counts = cnt[0, :N_EXPERTS].astype(jnp.int32)
    padded = (counts + ROW_BLOCK - 1) // ROW_BLOCK * ROW_BLOCK
    pad_ends = jnp.cumsum(padded)
    pad_starts = pad_ends - padded
    expert = route_t[0:TOP_K].astype(jnp.int32)
    rank = route_t[4:4 + TOP_K].astype(jnp.int32)
    first_row = jnp.sum(jnp.where(expert[None] == jnp.arange(N_EXPERTS, dtype=jnp.int32)[:, None, None],
                                  pad_starts[:, None, None], 0), axis=0)
    dest = first_row + rank
    n_blocks = -(-(t_rows * TOP_K + N_EXPERTS * (ROW_BLOCK - 1)) // ROW_BLOCK)
    block_start = jnp.arange(n_blocks, dtype=jnp.int32) * ROW_BLOCK
    block_exp = jnp.minimum(jnp.sum(pad_ends[None, :] <= block_start[:, None], axis=1),
                            N_EXPERTS - 1).astype(jnp.int32)
    block_exp = jnp.concatenate([block_exp, block_exp[-1:]])
    n_used = (pad_ends[-1:] // ROW_BLOCK).astype(jnp.int32)

    rb = ROW_BLOCK
    stride = t_rows + rb
    fill_lo = jnp.concatenate([pad_starts + counts, pad_ends[-1:]]).astype(jnp.int32)
    fill_hi = jnp.concatenate([pad_ends, jnp.full((1,), n_blocks * rb)]).astype(jnp.int32)
    row_map = _row_map(fill_lo, fill_hi, dest.reshape(TOP_K, t_rows // ROWMAP_TM, 1, ROWMAP_TM),
                       n_blocks * rb, t_rows)
    dst_blocks = jnp.concatenate([t_rows + jnp.arange(rb, dtype=jnp.int32), row_map]).reshape(
        n_blocks + 1, 1, rb)
    src = jnp.minimum(row_map % stride, t_rows - 1)
    src_blocks = jnp.concatenate([src, jnp.zeros((2 * rb,), jnp.int32)]).reshape(n_blocks + 2, 1, rb)

    ysc = _moe(block_exp, n_used, src_blocks, dst_blocks, h1_packed, exp_w_gate[l], exp_w_up[l],
               exp_w_down[l], n_blocks)
    out = _combine(h1, route, ysc, row2(ln2_g[l]), row2(ln2_b[l]))
    return out.reshape(nb, seq, d).astype(x.dtype)
```

```python
import functools
import math

import jax
import jax.numpy as jnp
from jax import lax
from jax.experimental import pallas as pl
from jax.experimental.pallas import tpu as pltpu

D_MODEL = 2048
N_META = 16
CHUNK = 64
ATTN_HEADS = 8
ATTN_HEAD_DIM = 64
ATTN_VDIM = 2 * ATTN_HEAD_DIM
ATTN_WIDTH = ATTN_HEADS * ATTN_VDIM
SSM_WIDTH = D_MODEL // 4
SSM_GROUP = 16
SSM_GROUPS = SSM_WIDTH // SSM_GROUP
SSM_STATE = 64
SSM_COLS = SSM_GROUPS * SSM_STATE
N_EXPERT_GROUPS = 4
EXPERTS_PER_GROUP = 8
N_EXPERTS = N_EXPERT_GROUPS * EXPERTS_PER_GROUP
TOP_K = 2
D_FF_EXPERT = D_MODEL // 4
QKV_COLS = 3 * ATTN_WIDTH
GATE_COLS = 2 * D_MODEL
IN_COLS = SSM_WIDTH + QKV_COLS + GATE_COLS
LN_EPS = 1e-5
DEPTH = 1
DEEPNORM_ALPHA = (2.0 * DEPTH) ** 0.25
LANES = 128
NEG_INF = -1e30

PROJ_TN = 1024
PROJ_TM = 1024
SSM_TC = 32
SSM_CB = 512
SSM_HALVES = 2
ATTN_TQ = 256
MERGE_TM = 256
MERGE_SUB = 256
ROW_BLOCK = 256
COMBINE_TM = 256
ROWMAP_TM = 512
MOE_SLOTS = 3
DUMP_ROWS = 2 * ROW_BLOCK
ROUTE_ROWS = 8
VMEM_LIMIT = 56 * 1024 * 1024


def _pack_halves(x):
    n = x.shape[1] // 2
    bits = lambda v: lax.bitcast_convert_type(v.astype(jnp.bfloat16).astype(jnp.float32), jnp.uint32)
    return bits(x[:, :n]) | (bits(x[:, n:]) >> 16)


def _unpack_halves(w):
    hi = lax.bitcast_convert_type(w & jnp.uint32(0xFFFF0000), jnp.float32)
    lo = lax.bitcast_convert_type(w << 16, jnp.float32)
    return hi, lo


def _layer_norm(x, g, b):
    mu = jnp.mean(x, axis=-1, keepdims=True)
    xc = x - mu
    var = jnp.mean(xc * xc, axis=-1, keepdims=True)
    return xc * lax.rsqrt(var + LN_EPS) * g + b


def _ln_proj_kernel(x_ref, g_ref, b_ref, wu_ref, w_ref, u_ref, proj_ref, xn_ref, *, n_gate_blocks):
    j = pl.program_id(1)

    @pl.when(j == 0)
    def _():
        xn = _layer_norm(x_ref[...], g_ref[...], b_ref[...]).astype(jnp.bfloat16)
        xn_ref[...] = xn
        u_ref[...] = jnp.dot(xn, wu_ref[...], preferred_element_type=jnp.float32)

    acc = jnp.dot(xn_ref[...], w_ref[...], preferred_element_type=jnp.float32)
    if n_gate_blocks:
        acc = jnp.where(j < n_gate_blocks, jax.nn.sigmoid(acc), acc)
    proj_ref[...] = acc.astype(jnp.bfloat16)


def _ln_proj(x2, g, b, w_u, w_rest, tm, gate_cols, first_col=0, cols=None):
    t_rows = x2.shape[0]
    cols = w_rest.shape[1] - first_col if cols is None else cols
    n_col = cols // PROJ_TN
    first_blk = first_col // PROJ_TN
    assert cols % PROJ_TN == 0 and gate_cols % PROJ_TN == 0 and first_col % PROJ_TN == 0 and t_rows % tm == 0
    return pl.pallas_call(
        functools.partial(_ln_proj_kernel, n_gate_blocks=gate_cols // PROJ_TN),
        grid=(t_rows // tm, n_col),
        in_specs=[
            pl.BlockSpec((tm, D_MODEL), lambda i, j: (i, 0)),
            pl.BlockSpec((1, D_MODEL), lambda i, j: (0, 0)),
            pl.BlockSpec((1, D_MODEL), lambda i, j: (0, 0)),
            pl.BlockSpec((D_MODEL, SSM_WIDTH), lambda i, j: (0, 0)),
            pl.BlockSpec((D_MODEL, PROJ_TN), lambda i, j: (0, j + first_blk)),
        ],
        out_specs=[
            pl.BlockSpec((tm, SSM_WIDTH), lambda i, j: (i, 0)),
            pl.BlockSpec((tm, PROJ_TN), lambda i, j: (i, j)),
        ],
        out_shape=[
            jax.ShapeDtypeStruct((t_rows, SSM_WIDTH), jnp.float32),
            jax.ShapeDtypeStruct((t_rows, cols), jnp.bfloat16),
        ],
        scratch_shapes=[pltpu.VMEM((tm, D_MODEL), jnp.bfloat16)],
        compiler_params=pltpu.CompilerParams(
            dimension_semantics=("parallel", "arbitrary"), vmem_limit_bytes=VMEM_LIMIT),
        name="ln_proj",
    )(x2, g, b, w_u, w_rest)


def _ssm_kernel(um_ref, u_ref, bb_ref, ab_ref, cm_ref, d_ref, wglu_ref, y_ref, state_ref, bu_ref, *, nb):
    half_u = SSM_WIDTH // SSM_HALVES
    half_c = SSM_COLS // SSM_HALVES

    def expand_and_scan(u_bf16, n_steps):
        rows = n_steps * nb
        for h in range(SSM_HALVES):
            bu_ref[pl.ds(0, rows), 2 * half_c * h:2 * half_c * (h + 1)] = jnp.dot(
                u_bf16[:, half_u * h:half_u * (h + 1)], bb_ref[h], preferred_element_type=jnp.float32)
        for cb in range(SSM_COLS // SSM_CB):
            h, off = divmod(cb * SSM_CB, half_c)
            re_cols = pl.ds(2 * half_c * h + off, SSM_CB)
            im_cols = pl.ds(2 * half_c * h + half_c + off, SSM_CB)
            a_re = ab_ref[0, :, cb * SSM_CB:(cb + 1) * SSM_CB]
            a_im = ab_ref[1, :, cb * SSM_CB:(cb + 1) * SSM_CB]
            groups = [pl.ds(8 * g, 8) for g in range(nb // 8)]

            def step(t, carry):
                r0 = pl.multiple_of(t * nb, nb)
                out = []
                for g, (s_re, s_im) in enumerate(carry):
                    rows = pl.ds(r0 + 8 * g, 8)
                    n_re = a_re * s_re - a_im * s_im + bu_ref[rows, re_cols]
                    n_im = a_re * s_im + a_im * s_re + bu_ref[rows, im_cols]
                    bu_ref[rows, re_cols] = n_re
                    bu_ref[rows, im_cols] = n_im
                    out.append((n_re, n_im))
                return tuple(out)

            init = tuple((state_ref[g, re_cols], state_ref[g, im_cols]) for g in groups)
            final = lax.fori_loop(0, n_steps, step, init, unroll=2)
            for g, (s_re, s_im) in zip(groups, final):
                state_ref[g, re_cols] = s_re
                state_ref[g, im_cols] = s_im

    @pl.when(pl.program_id(0) == 0)
    def _():
        state_ref[...] = jnp.zeros_like(state_ref)
        expand_and_scan(um_ref[...].astype(jnp.bfloat16), N_META)

    u = u_ref[...]
    expand_and_scan(u.astype(jnp.bfloat16), SSM_TC)
    y = jnp.concatenate(
        [jnp.dot(bu_ref[:, 2 * half_c * h:2 * half_c * (h + 1)].astype(jnp.bfloat16), cm_ref[h],
                 preferred_element_type=jnp.float32) for h in range(SSM_HALVES)], axis=1)
    y = jax.nn.gelu(y + d_ref[...] * u)
    gate = jnp.dot(y.astype(jnp.bfloat16), wglu_ref[...], preferred_element_type=jnp.float32)
    y_ref[...] = (y * jax.nn.sigmoid(gate)).astype(y_ref.dtype)


def _ssm(u_meta_rows, u_tm, bb, ab, cm, d, wglu, nb):
    rows = u_tm.shape[0]
    blk = SSM_TC * nb
    assert rows % blk == 0 and nb % 8 == 0 and N_META <= SSM_TC
    const = lambda i: (0, 0)
    return pl.pallas_call(
        functools.partial(_ssm_kernel, nb=nb),
        grid=(rows // blk,),
        in_specs=[
            pl.BlockSpec((N_META * nb, SSM_WIDTH), const),
            pl.BlockSpec((blk, SSM_WIDTH), lambda i: (i, 0)),
            pl.BlockSpec((SSM_HALVES, SSM_WIDTH // SSM_HALVES, 2 * SSM_COLS // SSM_HALVES), lambda i: (0, 0, 0)),
            pl.BlockSpec((2, 8, SSM_COLS), lambda i: (0, 0, 0)),
            pl.BlockSpec((SSM_HALVES, 2 * SSM_COLS // SSM_HALVES, SSM_WIDTH // SSM_HALVES), lambda i: (0, 0, 0)),
            pl.BlockSpec((1, SSM_WIDTH), const),
            pl.BlockSpec((SSM_WIDTH, SSM_WIDTH), const),
        ],
        out_specs=pl.BlockSpec((blk, SSM_WIDTH), lambda i: (i, 0)),
        out_shape=jax.ShapeDtypeStruct((rows, SSM_WIDTH), jnp.bfloat16),
        scratch_shapes=[
            pltpu.VMEM((nb, 2 * SSM_COLS), jnp.float32),
            pltpu.VMEM((blk, 2 * SSM_COLS), jnp.float32),
        ],
        compiler_params=pltpu.CompilerParams(
            dimension_semantics=("arbitrary",), vmem_limit_bytes=VMEM_LIMIT),
        name="ssm",
    )(u_meta_rows, u_tm, bb, ab, cm, d, wglu)


def _ssm_params(a_re, a_im, log_dt, b_re, b_im, c_re, c_im):
    f32 = jnp.float32
    lam_re = jnp.minimum(a_re.astype(f32), -1e-4)
    lam_im = a_im.astype(f32)
    dt = jnp.exp(log_dt.astype(f32))[:, None]
    mag = jnp.exp(lam_re * dt)
    ab_re = mag * jnp.cos(lam_im * dt)
    ab_im = mag * jnp.sin(lam_im * dt)
    den = lam_re * lam_re + lam_im * lam_im
    nr = ab_re - 1.0
    ni = ab_im
    z_re = (nr * lam_re + ni * lam_im) / den
    z_im = (ni * lam_re - nr * lam_im) / den
    br32 = b_re.astype(f32)
    bi32 = b_im.astype(f32)
    bb_re = z_re[..., None] * br32 - z_im[..., None] * bi32
    bb_im = z_re[..., None] * bi32 + z_im[..., None] * br32
    gh = SSM_GROUPS // SSM_HALVES
    eye = jnp.eye(gh, dtype=f32)
    split = lambda a: a.reshape((SSM_HALVES, gh) + a.shape[1:])
    exp_re = jnp.einsum('kgpc,gh->kgchp', split(bb_re), eye).reshape(SSM_HALVES, gh * SSM_GROUP, gh * SSM_STATE)
    exp_im = jnp.einsum('kgpc,gh->kgchp', split(bb_im), eye).reshape(SSM_HALVES, gh * SSM_GROUP, gh * SSM_STATE)
    bb = jnp.concatenate([exp_re, exp_im], axis=2)
    ro_re = jnp.einsum('kgcp,gh->kgphc', split(c_re.astype(f32)), eye).reshape(
        SSM_HALVES, gh * SSM_STATE, gh * SSM_GROUP)
    ro_im = jnp.einsum('kgcp,gh->kgphc', split(c_im.astype(f32)), eye).reshape(
        SSM_HALVES, gh * SSM_STATE, gh * SSM_GROUP)
    cm = jnp.concatenate([ro_re, -ro_im], axis=1)
    ab = jnp.stack([ab_re.reshape(SSM_COLS), ab_im.reshape(SSM_COLS)], axis=0)
    ab = jnp.broadcast_to(ab[:, None, :], (2, 8, SSM_COLS))
    return bb.astype(jnp.bfloat16), ab, cm.astype(jnp.bfloat16)


def _attn_kernel(lam_ref, q_ref, k_ref, v_ref, km_ref, vm_ref, g_ref, o_ref, vt_ref, *, scale, seq):
    tq = ATTN_TQ
    f32, bf16 = jnp.float32, jnp.bfloat16
    vt_ref[...] = v_ref[...].astype(f32).T.astype(bf16)
    vm_t = vm_ref[...].astype(f32).T.astype(bf16)
    k_meta = km_ref[...]
    lane = lax.broadcasted_iota(jnp.int32, (tq, ATTN_VDIM), 1)
    key = lax.broadcasted_iota(jnp.int32, (tq, 2 * tq), 0)
    qry = lax.broadcasted_iota(jnp.int32, (tq, 2 * tq), 1) % tq
    diag_visible = qry // CHUNK >= key // CHUNK
    col_max = lambda a: jnp.max(a, axis=0, keepdims=True)
    col_sum = lambda a: jnp.sum(a, axis=0, keepdims=True)

    def scores(i):
        rows = slice(i * tq, (i + 1) * tq)
        q = q_ref[rows, :] * scale
        zero = jnp.zeros_like(q)
        qq = jnp.concatenate([jnp.where(lane < ATTN_HEAD_DIM, q, zero),
                              jnp.where(lane >= ATTN_HEAD_DIM, q, zero)], axis=0)
        scores_t = lambda kb: lax.dot_general(kb, qq, (((1,), (1,)), ((), ())), preferred_element_type=f32)
        s_meta = scores_t(k_meta)
        s_diag = jnp.where(diag_visible, scores_t(k_ref[rows, :]), NEG_INF)
        s_prev = scores_t(k_ref[0:i * tq, :]) if i > 0 else None
        return s_meta, s_diag, s_prev

    def softmax(i, s):
        s_meta, s_diag, s_prev = s
        m = jnp.maximum(col_max(s_meta), col_max(s_diag))
        if i > 0:
            m = jnp.maximum(m, col_max(s_prev))
        p_meta = jnp.exp(s_meta - m)
        p_diag = jnp.exp(s_diag - m)
        l = col_sum(p_meta) + col_sum(p_diag)
        p_prev = None
        if i > 0:
            p_prev = jnp.exp(s_prev - m)
            l = l + col_sum(p_prev)
            p_prev = p_prev.astype(bf16)
        return p_meta.astype(bf16), p_diag.astype(bf16), p_prev, l

    def weighted_values(i, p):
        p_meta, p_diag, p_prev, l = p
        rows = slice(i * tq, (i + 1) * tq)
        acc = (jnp.dot(vm_t, p_meta, preferred_element_type=f32)
               + jnp.dot(vt_ref[:, rows], p_diag, preferred_element_type=f32))
        if i > 0:
            acc = acc + jnp.dot(vt_ref[:, 0:i * tq], p_prev, preferred_element_type=f32)
        o_all = acc / l
        o_t = o_all[:, :tq] - lam_ref[0] * o_all[:, tq:]
        o_t = o_t * lax.rsqrt(jnp.mean(o_t * o_t, axis=0, keepdims=True) + LN_EPS)
        o_ref[rows, :] = (o_t.T * g_ref[...]).astype(o_ref.dtype)

    n_blk = seq // tq
    s_next = scores(0)
    p_last = None
    for i in range(n_blk):
        s_cur = s_next
        s_next = scores(i + 1) if i + 1 < n_blk else None
        if p_last is not None:
            weighted_values(i - 1, p_last)
        p_last = softmax(i, s_cur)
    weighted_values(n_blk - 1, p_last)


def _attention(lam, proj, proj_meta, g_scaled, nb, seq):
    assert seq % ATTN_TQ == 0
    nh = ATTN_HEADS
    first = GATE_COLS // ATTN_VDIM
    head = lambda part: pl.BlockSpec((None, seq, ATTN_VDIM), lambda b, h: (b, 0, first + part * nh + h))
    return pl.pallas_call(
        functools.partial(_attn_kernel, scale=ATTN_HEAD_DIM ** -0.5, seq=seq),
        grid=(nb, nh),
        in_specs=[
            pl.BlockSpec(memory_space=pltpu.SMEM),
            head(0), head(1), head(2),
            pl.BlockSpec((N_META, ATTN_VDIM), lambda b, h: (0, h)),
            pl.BlockSpec((N_META, ATTN_VDIM), lambda b, h: (0, nh + h)),
            pl.BlockSpec((1, ATTN_VDIM), lambda b, h: (0, 0)),
        ],
        out_specs=pl.BlockSpec((None, seq, ATTN_VDIM), lambda b, h: (b, 0, h)),
        out_shape=jax.ShapeDtypeStruct((nb, seq, ATTN_WIDTH), jnp.bfloat16),
        scratch_shapes=[pltpu.VMEM((ATTN_VDIM, seq), jnp.bfloat16)],
        compiler_params=pltpu.CompilerParams(
            dimension_semantics=("parallel", "parallel"), vmem_limit_bytes=VMEM_LIMIT),
        name="attention",
    )(lam, proj, proj, proj, proj_meta, proj_meta, g_scaled)


def _merge_kernel(x_ref, gi_ref, bi_ref, gate_ref, ys_ref, ya_ref, wbs_ref, wba_ref, wo_ref,
                  g1_ref, b1_ref, wr_ref, br_ref, h1_ref, h1p_ref, route_ref, route_t_ref, cnt_ref):
    sub = MERGE_SUB

    @pl.when(pl.program_id(0) == 0)
    def _():
        cnt_ref[...] = jnp.zeros_like(cnt_ref)

    lane_i = lax.broadcasted_iota(jnp.int32, (sub, LANES), 1)
    lane = lane_i.astype(jnp.float32)
    is_group = jnp.logical_and(lane_i >= N_EXPERTS, lane_i < N_EXPERTS + N_EXPERT_GROUPS)
    lane_group = (lane_i // EXPERTS_PER_GROUP).astype(jnp.float32)
    r_i = lax.broadcasted_iota(jnp.int32, (sub, sub), 0)
    c_i = lax.broadcasted_iota(jnp.int32, (sub, sub), 1)
    tri = jnp.where(c_i < r_i, 1.0, 0.0).astype(jnp.bfloat16)

    def first_argmax(vals):
        mx = jnp.max(vals, axis=-1, keepdims=True)
        idx = jnp.min(jnp.where(vals == mx, lane, float(LANES)), axis=-1, keepdims=True)
        return mx, idx

    def mix_stage(rows):
        gates = gate_ref[rows, :].astype(jnp.float32)
        ps = jnp.dot(ys_ref[rows, :], wbs_ref[...], preferred_element_type=jnp.float32)
        pa = jnp.dot(ya_ref[rows, :], wba_ref[...], preferred_element_type=jnp.float32)
        merged = gates[:, :D_MODEL] * ps + gates[:, D_MODEL:] * pa
        return jnp.dot(merged.astype(jnp.bfloat16), wo_ref[...], preferred_element_type=jnp.float32)

    def route_stage(rows, mix, count):
        h = _layer_norm(x_ref[rows, :], gi_ref[...], bi_ref[...])
        h1 = _layer_norm(DEEPNORM_ALPHA * h + mix, g1_ref[...], b1_ref[...])
        h1_ref[rows, :] = h1
        h1p_ref[rows, :] = _pack_halves(h1)

        logits = jnp.dot(h1.astype(jnp.bfloat16), wr_ref[...], preferred_element_type=jnp.float32) + br_ref[...]
        gl = jnp.where(is_group, logits, NEG_INF)
        gmax, glane = first_argmax(gl)
        g_val = 1.0 / jnp.sum(jnp.where(is_group, jnp.exp(gl - gmax), 0.0), axis=-1, keepdims=True)
        g_idx = glane - float(N_EXPERTS)
        in_group = jnp.logical_and(lane_i < N_EXPERTS, lane_group == g_idx)
        el = jnp.where(in_group, logits, NEG_INF)
        m1, i1 = first_argmax(el)
        el2 = jnp.where(lane == i1, NEG_INF, el)
        m2, i2 = first_argmax(el2)
        e2 = jnp.exp(m2 - m1)
        w1 = g_val / (1.0 + e2)
        w2 = g_val * e2 / (1.0 + e2)

        oh = jnp.where(jnp.logical_or(lane == i1, lane == i2), 1.0, 0.0)
        before = jnp.dot(tri, oh.astype(jnp.bfloat16), preferred_element_type=jnp.float32) + count
        rank1 = jnp.sum(jnp.where(lane == i1, before, 0.0), axis=-1, keepdims=True)
        rank2 = jnp.sum(jnp.where(lane == i2, before, 0.0), axis=-1, keepdims=True)

        route = jnp.where(lane_i == 0, i1, 0.0)
        route = jnp.where(lane_i == 1, i2, route)
        route = jnp.where(lane_i == 2, w1, route)
        route = jnp.where(lane_i == 3, w2, route)
        route = jnp.where(lane_i == 4, rank1, route)
        route = jnp.where(lane_i == 5, rank2, route)
        route_ref[rows, :] = route
        route_t_ref[:, rows] = route.T[0:ROUTE_ROWS, :]
        return count + jnp.sum(oh, axis=0, keepdims=True)

    n_chain = x_ref.shape[0] // sub
    rows = [slice(c * sub, (c + 1) * sub) for c in range(n_chain)]
    count = cnt_ref[...]
    mix = mix_stage(rows[0])
    for c in range(n_chain):
        mix_next = mix_stage(rows[c + 1]) if c + 1 < n_chain else None
        count = route_stage(rows[c], mix, count)
        mix = mix_next
    cnt_ref[...] = count


def _merge(x2, gi, bi, gates, ys, ya, wbs, wba, wo, g1, b1, wr, br):
    t_rows = x2.shape[0]
    tm = MERGE_TM
    assert t_rows % tm == 0 and tm % MERGE_SUB == 0
    row = lambda i: (i, 0)
    const = lambda i: (0, 0)
    return pl.pallas_call(
        _merge_kernel,
        grid=(t_rows // tm,),
        in_specs=[
            pl.BlockSpec((tm, D_MODEL), row),
            pl.BlockSpec((1, D_MODEL), const),
            pl.BlockSpec((1, D_MODEL), const),
            pl.BlockSpec((tm, GATE_COLS), row),
            pl.BlockSpec((tm, SSM_WIDTH), row),
            pl.BlockSpec((tm, ATTN_WIDTH), row),
            pl.BlockSpec((SSM_WIDTH, D_MODEL), const, pipeline_mode=pl.Buffered(1)),
            pl.BlockSpec((ATTN_WIDTH, D_MODEL), const, pipeline_mode=pl.Buffered(1)),
            pl.BlockSpec((D_MODEL, D_MODEL), const, pipeline_mode=pl.Buffered(1)),
            pl.BlockSpec((1, D_MODEL), const),
            pl.BlockSpec((1, D_MODEL), const),
            pl.BlockSpec((D_MODEL, LANES), const),
            pl.BlockSpec((1, LANES), const),
        ],
        out_specs=[
            pl.BlockSpec((tm, D_MODEL), row),
            pl.BlockSpec((tm, D_MODEL // 2), row),
            pl.BlockSpec((tm, LANES), row),
            pl.BlockSpec((ROUTE_ROWS, tm), lambda i: (0, i)),
            pl.BlockSpec((1, LANES), const),
        ],
        out_shape=[
            jax.ShapeDtypeStruct((t_rows, D_MODEL), jnp.float32),
            jax.ShapeDtypeStruct((t_rows, D_MODEL // 2), jnp.uint32),
            jax.ShapeDtypeStruct((t_rows, LANES), jnp.float32),
            jax.ShapeDtypeStruct((ROUTE_ROWS, t_rows), jnp.float32),
            jax.ShapeDtypeStruct((1, LANES), jnp.float32),
        ],
        compiler_params=pltpu.CompilerParams(
            dimension_semantics=("arbitrary",), vmem_limit_bytes=VMEM_LIMIT),
        name="merge",
    )(x2, gi, bi, gates, ys, ya, wbs, wba, wo, g1, b1, wr, br)


def _row_map_kernel(lo_ref, hi_ref, dest0_ref, dest1_ref, table_ref, *, t_rows):
    i = pl.program_id(0)
    tm = ROWMAP_TM
    stride = t_rows + DUMP_ROWS

    @pl.when(i == 0)
    def _():
        for e in range(N_EXPERTS + 1):
            def fill(r, carry):
                table_ref[r] = t_rows + (r & (DUMP_ROWS - 1))
                return carry
            lax.fori_loop(lo_ref[e], hi_ref[e], fill, 0)

    base = i * tm
    for a in range(tm):
        table_ref[dest0_ref[0, a]] = base + a
        table_ref[dest1_ref[0, a]] = base + (a + stride)


def _row_map(fill_lo, fill_hi, dest4, n_rows, t_rows):
    tm = ROWMAP_TM
    assert t_rows % tm == 0 and ROW_BLOCK & (ROW_BLOCK - 1) == 0
    grid_spec = pltpu.PrefetchScalarGridSpec(
        num_scalar_prefetch=2,
        grid=(t_rows // tm,),
        in_specs=[pl.BlockSpec((None, None, 1, tm), lambda i, lo, hi, k=k: (k, i, 0, 0), memory_space=pltpu.SMEM)
                  for k in range(TOP_K)],
        out_specs=pl.BlockSpec(memory_space=pltpu.SMEM),
    )
    return pl.pallas_call(
        functools.partial(_row_map_kernel, t_rows=t_rows),
        grid_spec=grid_spec,
        out_shape=jax.ShapeDtypeStruct((n_rows,), jnp.int32),
        compiler_params=pltpu.CompilerParams(dimension_semantics=("arbitrary",)),
        name="row_map",
    )(fill_lo, fill_hi, dest4, dest4)


def _moe_kernel(be_ref, nu_ref, src0_ref, src1_ref, src2_ref, dst0_ref, dst1_ref, h1_hbm, wg_ref, wu_ref,
                wd_ref, ysc_hbm, xbuf0, xbuf1, xbuf2, ybuf0, ybuf1, ybuf2, wgb, wub, wdb, gsem, ssem):
    i = pl.program_id(0)
    n_used = nu_ref[0]
    rb = ROW_BLOCK
    ns = MOE_SLOTS
    xbuf = (xbuf0, xbuf1, xbuf2)
    ybuf = (ybuf0, ybuf1, ybuf2)

    def start_gather(idx_ref, s):
        for r in range(rb):
            pltpu.make_async_copy(h1_hbm.at[pl.ds(idx_ref[0, r], 1)], xbuf[s].at[pl.ds(r, 1)],
                                  gsem.at[s]).start(priority=r % 2)

    def wait_gather(s):
        pltpu.make_async_copy(h1_hbm.at[pl.ds(0, rb)], xbuf[s], gsem.at[s]).wait()

    def start_scatter(idx_ref, s):
        for r in range(rb):
            pltpu.make_async_copy(ybuf[s].at[pl.ds(r, 1)], ysc_hbm.at[pl.ds(idx_ref[0, r], 1)],
                                  ssem.at[s]).start(priority=r % 2)

    def wait_scatter(s):
        pltpu.make_async_copy(ybuf[s], ysc_hbm.at[pl.ds(0, rb)], ssem.at[s]).wait()

    @pl.when(i == 0)
    def _():
        ybuf1[...] = jnp.zeros_like(ybuf1)
        ybuf2[...] = jnp.zeros_like(ybuf2)
        start_gather(src0_ref, 0)
        start_gather(src1_ref, 1)
        start_scatter(dst0_ref, 1)

    expert_changed = jnp.logical_or(i == 0, be_ref[i] != be_ref[jnp.maximum(i - 1, 0)])

    @pl.when(jnp.logical_and(i <= n_used, expert_changed))
    def _():
        wgb[...] = wg_ref[...].astype(jnp.bfloat16)
        wub[...] = wu_ref[...].astype(jnp.bfloat16)
        wdb[...] = wd_ref[...].astype(jnp.bfloat16)

    def step(slot):
        ahead = (slot + 2) % ns
        wait_gather(slot)

        @pl.when(i >= 1)
        def _():
            wait_scatter(slot)

        start_gather(src2_ref, ahead)
        start_scatter(dst1_ref, ahead)
        xb = jnp.concatenate([half.astype(jnp.bfloat16) for half in _unpack_halves(xbuf[slot][...])], axis=1)
        g = jnp.dot(xb, wgb[...], preferred_element_type=jnp.float32)
        u = jnp.dot(xb, wub[...], preferred_element_type=jnp.float32)
        hb = (jax.nn.silu(g) * u).astype(jnp.bfloat16)
        ybuf[slot][...] = _pack_halves(jnp.dot(hb, wdb[...], preferred_element_type=jnp.float32))

        @pl.when(i == n_used)
        def _():
            for s in ((slot + 1) % ns, ahead):
                wait_gather(s)
                wait_scatter(s)

    for s in range(ns):
        pl.when(jnp.logical_and(i <= n_used, i % ns == s))(functools.partial(step, s))


def _moe(block_exp, n_used, src_blocks, dst_blocks, h1, wg, wu, wd, n_blocks):
    t_rows = h1.shape[0]
    rb = ROW_BLOCK
    assert MOE_SLOTS == 3
    wsel = lambda i, be, nu: (be[i], 0, 0)
    smem_blk = lambda off: pl.BlockSpec((None, 1, rb), lambda i, be, nu: (i + off, 0, 0),
                                        memory_space=pltpu.SMEM)
    row_buf = pltpu.VMEM((rb, D_MODEL // 2), jnp.uint32)
    grid_spec = pltpu.PrefetchScalarGridSpec(
        num_scalar_prefetch=2,
        grid=(n_blocks + 1,),
        in_specs=[
            smem_blk(0), smem_blk(1), smem_blk(2), smem_blk(0), smem_blk(1),
            pl.BlockSpec(memory_space=pl.ANY),
            pl.BlockSpec((None, D_MODEL, D_FF_EXPERT), wsel),
            pl.BlockSpec((None, D_MODEL, D_FF_EXPERT), wsel),
            pl.BlockSpec((None, D_FF_EXPERT, D_MODEL), wsel),
        ],
        out_specs=pl.BlockSpec(memory_space=pl.ANY),
        scratch_shapes=[row_buf] * (2 * MOE_SLOTS) + [
            pltpu.VMEM((D_MODEL, D_FF_EXPERT), jnp.bfloat16),
            pltpu.VMEM((D_MODEL, D_FF_EXPERT), jnp.bfloat16),
            pltpu.VMEM((D_FF_EXPERT, D_MODEL), jnp.bfloat16),
            pltpu.SemaphoreType.DMA((MOE_SLOTS,)),
            pltpu.SemaphoreType.DMA((MOE_SLOTS,)),
        ],
    )
    return pl.pallas_call(
        _moe_kernel,
        grid_spec=grid_spec,
        out_shape=jax.ShapeDtypeStruct((TOP_K * t_rows + 2 * rb, D_MODEL // 2), jnp.uint32),
        compiler_params=pltpu.CompilerParams(
            dimension_semantics=("arbitrary",), vmem_limit_bytes=VMEM_LIMIT),
        name="experts",
    )(block_exp, n_used, src_blocks, src_blocks, src_blocks, dst_blocks, dst_blocks, h1, wg, wu, wd)


def _combine_kernel(h1_ref, route_ref, y0_ref, y1_ref, g_ref, b_ref, o_ref):
    route = route_ref[...]
    y0 = _unpack_halves(y0_ref[...])
    y1 = _unpack_halves(y1_ref[...])
    ffn = jnp.concatenate([route[:, 2:3] * a + route[:, 3:4] * b for a, b in zip(y0, y1)], axis=1)
    o_ref[...] = _layer_norm(DEEPNORM_ALPHA * h1_ref[...] + ffn, g_ref[...], b_ref[...])


def _combine(h1, route, ysc, g2, b2):
    t_rows = h1.shape[0]
    tm = COMBINE_TM
    assert t_rows % tm == 0 and ROW_BLOCK % tm == 0
    second = (t_rows + DUMP_ROWS) // tm
    row = lambda i: (i, 0)
    const = lambda i: (0, 0)
    return pl.pallas_call(
        _combine_kernel,
        grid=(t_rows // tm,),
        in_specs=[
            pl.BlockSpec((tm, D_MODEL), row),
            pl.BlockSpec((tm, LANES), row),
            pl.BlockSpec((tm, D_MODEL // 2), row),
            pl.BlockSpec((tm, D_MODEL // 2), lambda i: (i + second, 0)),
            pl.BlockSpec((1, D_MODEL), const),
            pl.BlockSpec((1, D_MODEL), const),
        ],
        out_specs=pl.BlockSpec((tm, D_MODEL), row),
        out_shape=jax.ShapeDtypeStruct((t_rows, D_MODEL), jnp.float32),
        compiler_params=pltpu.CompilerParams(
            dimension_semantics=("parallel",), vmem_limit_bytes=VMEM_LIMIT),
        name="combine",
    )(h1, route, ysc, ysc, g2, b2)


def kernel(x, meta_tokens, ln_in_g, ln_in_b, w_in, ssm_a_re, ssm_a_im, ssm_log_dt, ssm_b_re, ssm_b_im,
           ssm_c_re, ssm_c_im, ssm_d, ssm_w_glu, attn_lambda_q1, attn_lambda_k1, attn_lambda_q2,
           attn_lambda_k2, attn_subln_g, w_br_ssm, w_br_attn, w_o, ln1_g, ln1_b, router_g_w, router_g_b,
           router_e_w, router_e_b, exp_w_gate, exp_w_up, exp_w_down, ln2_g, ln2_b):
    f32, bf16 = jnp.float32, jnp.bfloat16
    nb, seq, d = x.shape
    assert d == D_MODEL and w_in.shape[0] == DEPTH == 1
    t_rows = nb * seq
    l = 0
    lambda_init = 0.8 - 0.6 * math.exp(-0.3 * l)
    row2 = lambda v: v.reshape(1, -1).astype(f32)

    x2 = x.reshape(t_rows, d)
    w_u = w_in[l, :, :SSM_WIDTH].astype(bf16)
    w_rest = jnp.concatenate([w_in[l, :, SSM_WIDTH + QKV_COLS:], w_in[l, :, SSM_WIDTH:SSM_WIDTH + QKV_COLS]],
                             axis=1).astype(bf16)
    gi, bi = row2(ln_in_g), row2(ln_in_b)
    u, proj = _ln_proj(x2, gi, bi, w_u, w_rest, min(PROJ_TM, seq), GATE_COLS)
    u_meta, proj_meta = _ln_proj(meta_tokens.astype(f32), gi, bi, w_u, w_rest, N_META, 0,
                                 first_col=GATE_COLS + ATTN_WIDTH)

    bb, ab, cm = _ssm_params(ssm_a_re[l], ssm_a_im[l], ssm_log_dt[l], ssm_b_re[l], ssm_b_im[l],
                             ssm_c_re[l], ssm_c_im[l])
    u_tm = u.reshape(nb, seq, SSM_WIDTH).transpose(1, 0, 2).reshape(seq * nb, SSM_WIDTH)
    u_meta_rows = jnp.repeat(u_meta, nb, axis=0)
    y_ssm_tm = _ssm(u_meta_rows, u_tm, bb, ab, cm, row2(ssm_d[l]), ssm_w_glu[l].astype(bf16), nb)
    y_ssm = y_ssm_tm.reshape(seq, nb, SSM_WIDTH).transpose(1, 0, 2).reshape(t_rows, SSM_WIDTH)

    lam = (jnp.exp(jnp.sum(attn_lambda_q1[l].astype(f32) * attn_lambda_k1[l].astype(f32)))
           - jnp.exp(jnp.sum(attn_lambda_q2[l].astype(f32) * attn_lambda_k2[l].astype(f32)))
           + lambda_init).reshape(1)
    g_scaled = row2(attn_subln_g[l]) * (1.0 - lambda_init)
    y_attn = _attention(lam, proj.reshape(nb, seq, GATE_COLS + QKV_COLS), proj_meta, g_scaled, nb, seq)
    y_attn = y_attn.reshape(t_rows, ATTN_WIDTH)

    w_r = jnp.concatenate([router_e_w[l].astype(f32), router_g_w[l].astype(f32)], axis=1)
    w_r = jnp.pad(w_r, ((0, 0), (0, LANES - w_r.shape[1])))
    b_r = jnp.concatenate([router_e_b[l].astype(f32), router_g_b[l].astype(f32)])
    b_r = jnp.pad(b_r, (0, LANES - b_r.shape[0])).reshape(1, LANES)
    h1, h1_packed, route, route_t, cnt = _merge(x2, gi, bi, proj, y_ssm, y_attn, w_br_ssm[l].astype(bf16),
                                       w_br_attn[l].astype(bf16), w_o[l].astype(bf16), row2(ln1_g[l]),
                                       row2(ln1_b[l]), w_r.astype(bf16), b_r)

    counts = cnt[0, :N_EXPERTS].astype(jnp.int32)
    padded = (counts + ROW_BLOCK - 1) // ROW_BLOCK * ROW_BLOCK
    pad_ends = jnp.cumsum(padded)
    pad_starts = pad_ends - padded
    expert = route_t[0:TOP_K].astype(jnp.int32)
    rank = route_t[4:4 + TOP_K].astype(jnp.int32)
    first_row = jnp.sum(jnp.where(expert[None] == jnp.arange(N_EXPERTS, dtype=jnp.int32)[:, None, None],
                                  pad_starts[:, None, None], 0), axis=0)
    dest = first_row + rank
    n_blocks = -(-(t_rows * TOP_K + N_EXPERTS * (ROW_BLOCK - 1)) // ROW_BLOCK)
    block_start = jnp.arange(n_blocks, dtype=jnp.int32) * ROW_BLOCK
    block_exp = jnp.minimum(jnp.sum(pad_ends[None, :] <= block_start[:, None], axis=1),
                            N_EXPERTS - 1).astype(jnp.int32)
    block_exp = jnp.concatenate([block_exp, block_exp[-1:]])
    n_used = (pad_ends[-1:] // ROW_BLOCK).astype(jnp.int32)

    rb = ROW_BLOCK
    stride = t_rows + DUMP_ROWS
    fill_lo = jnp.concatenate([pad_starts + counts, pad_ends[-1:]]).astype(jnp.int32)
    fill_hi = jnp.concatenate([pad_ends, jnp.full((1,), n_blocks * rb)]).astype(jnp.int32)
    row_map = _row_map(fill_lo, fill_hi, dest.reshape(TOP_K, t_rows // ROWMAP_TM, 1, ROWMAP_TM),
                       n_blocks * rb, t_rows)
    dst_blocks = jnp.concatenate([t_rows + jnp.arange(DUMP_ROWS, dtype=jnp.int32), row_map]).reshape(
        n_blocks + 2, 1, rb)
    src = jnp.minimum(row_map % stride, t_rows - 1)
    src_blocks = jnp.concatenate([src, jnp.zeros((3 * rb,), jnp.int32)]).reshape(n_blocks + 3, 1, rb)

    ysc = _moe(block_exp, n_used, src_blocks, dst_blocks, h1_packed, exp_w_gate[l], exp_w_up[l],
               exp_w_down[l], n_blocks)
    out = _combine(h1, route, ysc, row2(ln2_g[l]), row2(ln2_b[l]))
    return out.reshape(nb, seq, d).astype(x.dtype)
```

```python
import functools
import math

import jax
import jax.numpy as jnp
from jax import lax
from jax.experimental import pallas as pl
from jax.experimental.pallas import tpu as pltpu

D_MODEL = 2048
N_META = 16
CHUNK = 64
ATTN_HEADS = 8
ATTN_HEAD_DIM = 64
ATTN_VDIM = 2 * ATTN_HEAD_DIM
ATTN_WIDTH = ATTN_HEADS * ATTN_VDIM
SSM_WIDTH = D_MODEL // 4
SSM_GROUP = 16
SSM_GROUPS = SSM_WIDTH // SSM_GROUP
SSM_STATE = 64
SSM_COLS = SSM_GROUPS * SSM_STATE
N_EXPERT_GROUPS = 4
EXPERTS_PER_GROUP = 8
N_EXPERTS = N_EXPERT_GROUPS * EXPERTS_PER_GROUP
TOP_K = 2
D_FF_EXPERT = D_MODEL // 4
QKV_COLS = 3 * ATTN_WIDTH
GATE_COLS = 2 * D_MODEL
IN_COLS = SSM_WIDTH + QKV_COLS + GATE_COLS
LN_EPS = 1e-5
DEPTH = 1
DEEPNORM_ALPHA = (2.0 * DEPTH) ** 0.25
LANES = 128
NEG_INF = -1e30
LOG2_E = math.log2(math.e)

PROJ_TN = 1024
PROJ_TM = 1024
SSM_TC = 32
SSM_CB = 512
SSM_HALVES = 2
ATTN_TQ = 256
MERGE_TM = 256
MERGE_SUB = 256
ROW_BLOCK = 256
COMBINE_TM = 256
ROWMAP_TM = 512
MOE_SLOTS = 3
DUMP_ROWS = 2 * ROW_BLOCK
ROUTE_ROWS = 8
VMEM_LIMIT = 56 * 1024 * 1024


def _pack_halves(x):
    n = x.shape[1] // 2
    bits = lambda v: lax.bitcast_convert_type(v.astype(jnp.bfloat16).astype(jnp.float32), jnp.uint32)
    return bits(x[:, :n]) | (bits(x[:, n:]) >> 16)


def _unpack_halves(w):
    hi = lax.bitcast_convert_type(w & jnp.uint32(0xFFFF0000), jnp.float32)
    lo = lax.bitcast_convert_type(w << 16, jnp.float32)
    return hi, lo


def _layer_norm(x, g, b):
    mu = jnp.mean(x, axis=-1, keepdims=True)
    xc = x - mu
    var = jnp.mean(xc * xc, axis=-1, keepdims=True)
    return xc * lax.rsqrt(var + LN_EPS) * g + b


def _ln_proj_kernel(x_ref, g_ref, b_ref, wu_ref, w_ref, u_ref, proj_ref, xn_ref, *, n_gate_blocks):
    j = pl.program_id(1)

    @pl.when(j == 0)
    def _():
        xn = _layer_norm(x_ref[...], g_ref[...], b_ref[...]).astype(jnp.bfloat16)
        xn_ref[...] = xn
        u_ref[...] = jnp.dot(xn, wu_ref[...], preferred_element_type=jnp.float32)

    acc = jnp.dot(xn_ref[...], w_ref[...], preferred_element_type=jnp.float32)
    if n_gate_blocks:
        acc = jnp.where(j < n_gate_blocks, jax.nn.sigmoid(acc), acc)
    proj_ref[...] = acc.astype(jnp.bfloat16)


def _ln_proj(x2, g, b, w_u, w_rest, tm, gate_cols, first_col=0, cols=None):
    t_rows = x2.shape[0]
    cols = w_rest.shape[1] - first_col if cols is None else cols
    n_col = cols // PROJ_TN
    first_blk = first_col // PROJ_TN
    assert cols % PROJ_TN == 0 and gate_cols % PROJ_TN == 0 and first_col % PROJ_TN == 0 and t_rows % tm == 0
    return pl.pallas_call(
        functools.partial(_ln_proj_kernel, n_gate_blocks=gate_cols // PROJ_TN),
        grid=(t_rows // tm, n_col),
        in_specs=[
            pl.BlockSpec((tm, D_MODEL), lambda i, j: (i, 0)),
            pl.BlockSpec((1, D_MODEL), lambda i, j: (0, 0)),
            pl.BlockSpec((1, D_MODEL), lambda i, j: (0, 0)),
            pl.BlockSpec((D_MODEL, SSM_WIDTH), lambda i, j: (0, 0)),
            pl.BlockSpec((D_MODEL, PROJ_TN), lambda i, j: (0, j + first_blk)),
        ],
        out_specs=[
            pl.BlockSpec((tm, SSM_WIDTH), lambda i, j: (i, 0)),
            pl.BlockSpec((tm, PROJ_TN), lambda i, j: (i, j)),
        ],
        out_shape=[
            jax.ShapeDtypeStruct((t_rows, SSM_WIDTH), jnp.float32),
            jax.ShapeDtypeStruct((t_rows, cols), jnp.bfloat16),
        ],
        scratch_shapes=[pltpu.VMEM((tm, D_MODEL), jnp.bfloat16)],
        compiler_params=pltpu.CompilerParams(
            dimension_semantics=("parallel", "arbitrary"), vmem_limit_bytes=VMEM_LIMIT),
        name="ln_proj",
    )(x2, g, b, w_u, w_rest)


def _ssm_kernel(um_ref, u_ref, bb_ref, ab_ref, cm_ref, d_ref, wglu_ref, y_ref, state_ref, bu_ref, *, nb):
    half_u = SSM_WIDTH // SSM_HALVES
    half_c = SSM_COLS // SSM_HALVES

    half_cols = lambda h: slice(2 * half_c * h, 2 * half_c * (h + 1))
    groups = [pl.ds(8 * g, 8) for g in range(nb // 8)]

    def expand(u_bf16, h, n_steps):
        bu_ref[pl.ds(0, n_steps * nb), half_cols(h)] = jnp.dot(
            u_bf16[:, half_u * h:half_u * (h + 1)], bb_ref[h], preferred_element_type=jnp.float32)

    def scan_half(h, n_steps):
        for cb in range(half_c // SSM_CB):
            off = cb * SSM_CB
            re_cols = pl.ds(2 * half_c * h + off, SSM_CB)
            im_cols = pl.ds(2 * half_c * h + half_c + off, SSM_CB)
            a_re = ab_ref[0, :, half_c * h + off:half_c * h + off + SSM_CB]
            a_im = ab_ref[1, :, half_c * h + off:half_c * h + off + SSM_CB]
            carry = [(state_ref[grp, re_cols], state_ref[grp, im_cols]) for grp in groups]
            for t in range(n_steps):
                for g, (s_re, s_im) in enumerate(carry):
                    rows = pl.ds(t * nb + 8 * g, 8)
                    carry[g] = (a_re * s_re - a_im * s_im + bu_ref[rows, re_cols],
                                a_re * s_im + a_im * s_re + bu_ref[rows, im_cols])
                    bu_ref[rows, re_cols], bu_ref[rows, im_cols] = carry[g]
            for grp, (s_re, s_im) in zip(groups, carry):
                state_ref[grp, re_cols] = s_re
                state_ref[grp, im_cols] = s_im

    def readout(h):
        return jnp.dot(bu_ref[:, half_cols(h)].astype(jnp.bfloat16), cm_ref[h], preferred_element_type=jnp.float32)

    @pl.when(pl.program_id(0) == 0)
    def _():
        state_ref[...] = jnp.zeros_like(state_ref)
        u_meta = um_ref[...].astype(jnp.bfloat16)
        for h in range(SSM_HALVES):
            expand(u_meta, h, N_META)
            scan_half(h, N_META)

    u = u_ref[...]
    u_bf16 = u.astype(jnp.bfloat16)
    for h in range(SSM_HALVES):
        expand(u_bf16, h, SSM_TC)
    y_halves = []
    for h in range(SSM_HALVES):
        scan_half(h, SSM_TC)
        y_halves.append(readout(h))
    y = jnp.concatenate(y_halves, axis=1)
    y = jax.nn.gelu(y + d_ref[...] * u)
    gate = jnp.dot(y.astype(jnp.bfloat16), wglu_ref[...], preferred_element_type=jnp.float32)
    y_ref[...] = (y * jax.nn.sigmoid(gate)).astype(y_ref.dtype)


def _ssm(u_meta_rows, u_tm, bb, ab, cm, d, wglu, nb):
    rows = u_tm.shape[0]
    blk = SSM_TC * nb
    assert rows % blk == 0 and nb % 8 == 0 and N_META <= SSM_TC
    const = lambda i: (0, 0)
    return pl.pallas_call(
        functools.partial(_ssm_kernel, nb=nb),
        grid=(rows // blk,),
        in_specs=[
            pl.BlockSpec((N_META * nb, SSM_WIDTH), const),
            pl.BlockSpec((blk, SSM_WIDTH), lambda i: (i, 0)),
            pl.BlockSpec((SSM_HALVES, SSM_WIDTH // SSM_HALVES, 2 * SSM_COLS // SSM_HALVES), lambda i: (0, 0, 0)),
            pl.BlockSpec((2, 8, SSM_COLS), lambda i: (0, 0, 0)),
            pl.BlockSpec((SSM_HALVES, 2 * SSM_COLS // SSM_HALVES, SSM_WIDTH // SSM_HALVES), lambda i: (0, 0, 0)),
            pl.BlockSpec((1, SSM_WIDTH), const),
            pl.BlockSpec((SSM_WIDTH, SSM_WIDTH), const),
        ],
        out_specs=pl.BlockSpec((blk, SSM_WIDTH), lambda i: (i, 0)),
        out_shape=jax.ShapeDtypeStruct((rows, SSM_WIDTH), jnp.bfloat16),
        scratch_shapes=[
            pltpu.VMEM((nb, 2 * SSM_COLS), jnp.float32),
            pltpu.VMEM((blk, 2 * SSM_COLS), jnp.float32),
        ],
        compiler_params=pltpu.CompilerParams(
            dimension_semantics=("arbitrary",), vmem_limit_bytes=VMEM_LIMIT),
        name="ssm",
    )(u_meta_rows, u_tm, bb, ab, cm, d, wglu)


def _ssm_params(a_re, a_im, log_dt, b_re, b_im, c_re, c_im):
    f32 = jnp.float32
    lam_re = jnp.minimum(a_re.astype(f32), -1e-4)
    lam_im = a_im.astype(f32)
    dt = jnp.exp(log_dt.astype(f32))[:, None]
    mag = jnp.exp(lam_re * dt)
    ab_re = mag * jnp.cos(lam_im * dt)
    ab_im = mag * jnp.sin(lam_im * dt)
    den = lam_re * lam_re + lam_im * lam_im
    nr = ab_re - 1.0
    ni = ab_im
    z_re = (nr * lam_re + ni * lam_im) / den
    z_im = (ni * lam_re - nr * lam_im) / den
    br32 = b_re.astype(f32)
    bi32 = b_im.astype(f32)
    bb_re = z_re[..., None] * br32 - z_im[..., None] * bi32
    bb_im = z_re[..., None] * bi32 + z_im[..., None] * br32
    gh = SSM_GROUPS // SSM_HALVES
    eye = jnp.eye(gh, dtype=f32)
    split = lambda a: a.reshape((SSM_HALVES, gh) + a.shape[1:])
    exp_re = jnp.einsum('kgpc,gh->kgchp', split(bb_re), eye).reshape(SSM_HALVES, gh * SSM_GROUP, gh * SSM_STATE)
    exp_im = jnp.einsum('kgpc,gh->kgchp', split(bb_im), eye).reshape(SSM_HALVES, gh * SSM_GROUP, gh * SSM_STATE)
    bb = jnp.concatenate([exp_re, exp_im], axis=2)
    ro_re = jnp.einsum('kgcp,gh->kgphc', split(c_re.astype(f32)), eye).reshape(
        SSM_HALVES, gh * SSM_STATE, gh * SSM_GROUP)
    ro_im = jnp.einsum('kgcp,gh->kgphc', split(c_im.astype(f32)), eye).reshape(
        SSM_HALVES, gh * SSM_STATE, gh * SSM_GROUP)
    cm = jnp.concatenate([ro_re, -ro_im], axis=1)
    ab = jnp.stack([ab_re.reshape(SSM_COLS), ab_im.reshape(SSM_COLS)], axis=0)
    ab = jnp.broadcast_to(ab[:, None, :], (2, 8, SSM_COLS))
    return bb.astype(jnp.bfloat16), ab, cm.astype(jnp.bfloat16)


def _attn_kernel(lam_ref, q_ref, k_ref, v_ref, km_ref, vm_ref, g_ref, o_ref, vt_ref, *, scale, seq):
    tq = ATTN_TQ
    f32, bf16 = jnp.float32, jnp.bfloat16
    vt_ref[...] = v_ref[...].astype(f32).T.astype(bf16)
    vm_t = vm_ref[...].astype(f32).T.astype(bf16)
    k_meta = km_ref[...]
    lane = lax.broadcasted_iota(jnp.int32, (tq, ATTN_VDIM), 1)
    key = lax.broadcasted_iota(jnp.int32, (tq, 2 * tq), 0)
    qry = lax.broadcasted_iota(jnp.int32, (tq, 2 * tq), 1) % tq
    diag_visible = qry // CHUNK >= key // CHUNK
    col_max = lambda a: jnp.max(a, axis=0, keepdims=True)
    col_sum = lambda a: jnp.sum(a, axis=0, keepdims=True)

    def scores(i):
        rows = slice(i * tq, (i + 1) * tq)
        q = q_ref[rows, :] * scale
        zero = jnp.zeros_like(q)
        qq = jnp.concatenate([jnp.where(lane < ATTN_HEAD_DIM, q, zero),
                              jnp.where(lane >= ATTN_HEAD_DIM, q, zero)], axis=0)
        scores_t = lambda kb: lax.dot_general(kb, qq, (((1,), (1,)), ((), ())), preferred_element_type=f32)
        s_meta = scores_t(k_meta)
        s_diag = jnp.where(diag_visible, scores_t(k_ref[rows, :]), NEG_INF)
        s_prev = scores_t(k_ref[0:i * tq, :]) if i > 0 else None
        return s_meta, s_diag, s_prev

    def softmax(i, s):
        s_meta, s_diag, s_prev = s
        m = jnp.maximum(col_max(s_meta), col_max(s_diag))
        if i > 0:
            m = jnp.maximum(m, col_max(s_prev))
        p_meta = jnp.exp2(s_meta - m)
        p_diag = jnp.exp2(s_diag - m)
        l = col_sum(p_meta) + col_sum(p_diag)
        p_prev = None
        if i > 0:
            p_prev = jnp.exp2(s_prev - m)
            l = l + col_sum(p_prev)
            p_prev = p_prev.astype(bf16)
        return p_meta.astype(bf16), p_diag.astype(bf16), p_prev, l

    def weighted_values(i, p):
        p_meta, p_diag, p_prev, l = p
        rows = slice(i * tq, (i + 1) * tq)
        acc = (jnp.dot(vm_t, p_meta, preferred_element_type=f32)
               + jnp.dot(vt_ref[:, rows], p_diag, preferred_element_type=f32))
        if i > 0:
            acc = acc + jnp.dot(vt_ref[:, 0:i * tq], p_prev, preferred_element_type=f32)
        o_all = acc / l
        o_t = o_all[:, :tq] - lam_ref[0] * o_all[:, tq:]
        o_t = o_t * lax.rsqrt(jnp.mean(o_t * o_t, axis=0, keepdims=True) + LN_EPS)
        o_ref[rows, :] = (o_t.T * g_ref[...]).astype(o_ref.dtype)

    n_blk = seq // tq
    s_next = scores(0)
    p_last = None
    for i in range(n_blk):
        s_cur = s_next
        s_next = scores(i + 1) if i + 1 < n_blk else None
        if p_last is not None:
            weighted_values(i - 1, p_last)
        p_last = softmax(i, s_cur)
    weighted_values(n_blk - 1, p_last)


def _attention(lam, proj, proj_meta, g_scaled, nb, seq):
    assert seq % ATTN_TQ == 0
    nh = ATTN_HEADS
    first = GATE_COLS // ATTN_VDIM
    head = lambda part: pl.BlockSpec((None, seq, ATTN_VDIM), lambda b, h: (b, 0, first + part * nh + h))
    return pl.pallas_call(
        functools.partial(_attn_kernel, scale=ATTN_HEAD_DIM ** -0.5, seq=seq),
        grid=(nb, nh),
        in_specs=[
            pl.BlockSpec(memory_space=pltpu.SMEM),
            head(0), head(1), head(2),
            pl.BlockSpec((N_META, ATTN_VDIM), lambda b, h: (0, h)),
            pl.BlockSpec((N_META, ATTN_VDIM), lambda b, h: (0, nh + h)),
            pl.BlockSpec((1, ATTN_VDIM), lambda b, h: (0, 0)),
        ],
        out_specs=pl.BlockSpec((None, seq, ATTN_VDIM), lambda b, h: (b, 0, h)),
        out_shape=jax.ShapeDtypeStruct((nb, seq, ATTN_WIDTH), jnp.bfloat16),
        scratch_shapes=[pltpu.VMEM((ATTN_VDIM, seq), jnp.bfloat16)],
        compiler_params=pltpu.CompilerParams(
            dimension_semantics=("parallel", "parallel"), vmem_limit_bytes=VMEM_LIMIT),
        name="attention",
    )(lam, proj, proj, proj, proj_meta, proj_meta, g_scaled)


def _merge_kernel(x_ref, gi_ref, bi_ref, gate_ref, ys_ref, ya_ref, wbs_ref, wba_ref, wo_ref,
                  g1_ref, b1_ref, wr_ref, br_ref, h1_ref, h1p_ref, route_ref, route_t_ref, cnt_ref):
    sub = MERGE_SUB

    @pl.when(pl.program_id(0) == 0)
    def _():
        cnt_ref[...] = jnp.zeros_like(cnt_ref)

    lane_i = lax.broadcasted_iota(jnp.int32, (sub, LANES), 1)
    lane = lane_i.astype(jnp.float32)
    is_group = jnp.logical_and(lane_i >= N_EXPERTS, lane_i < N_EXPERTS + N_EXPERT_GROUPS)
    lane_group = (lane_i // EXPERTS_PER_GROUP).astype(jnp.float32)
    r_i = lax.broadcasted_iota(jnp.int32, (sub, sub), 0)
    c_i = lax.broadcasted_iota(jnp.int32, (sub, sub), 1)
    tri = jnp.where(c_i < r_i, 1.0, 0.0).astype(jnp.bfloat16)

    def first_argmax(vals):
        mx = jnp.max(vals, axis=-1, keepdims=True)
        idx = jnp.min(jnp.where(vals == mx, lane, float(LANES)), axis=-1, keepdims=True)
        return mx, idx

    def mix_stage(rows):
        gates = gate_ref[rows, :].astype(jnp.float32)
        ps = jnp.dot(ys_ref[rows, :], wbs_ref[...], preferred_element_type=jnp.float32)
        pa = jnp.dot(ya_ref[rows, :], wba_ref[...], preferred_element_type=jnp.float32)
        merged = gates[:, :D_MODEL] * ps + gates[:, D_MODEL:] * pa
        return jnp.dot(merged.astype(jnp.bfloat16), wo_ref[...], preferred_element_type=jnp.float32)

    def route_stage(rows, mix, count):
        h = _layer_norm(x_ref[rows, :], gi_ref[...], bi_ref[...])
        h1 = _layer_norm(DEEPNORM_ALPHA * h + mix, g1_ref[...], b1_ref[...])
        h1_ref[rows, :] = h1
        h1p_ref[rows, :] = _pack_halves(h1)

        logits = jnp.dot(h1.astype(jnp.bfloat16), wr_ref[...], preferred_element_type=jnp.float32) + br_ref[...]
        gl = jnp.where(is_group, logits, NEG_INF)
        gmax, glane = first_argmax(gl)
        g_val = 1.0 / jnp.sum(jnp.where(is_group, jnp.exp(gl - gmax), 0.0), axis=-1, keepdims=True)
        g_idx = glane - float(N_EXPERTS)
        in_group = jnp.logical_and(lane_i < N_EXPERTS, lane_group == g_idx)
        el = jnp.where(in_group, logits, NEG_INF)
        m1, i1 = first_argmax(el)
        el2 = jnp.where(lane == i1, NEG_INF, el)
        m2, i2 = first_argmax(el2)
        e2 = jnp.exp(m2 - m1)
        w1 = g_val / (1.0 + e2)
        w2 = g_val * e2 / (1.0 + e2)

        oh = jnp.where(jnp.logical_or(lane == i1, lane == i2), 1.0, 0.0)
        before = jnp.dot(tri, oh.astype(jnp.bfloat16), preferred_element_type=jnp.float32) + count
        rank1 = jnp.sum(jnp.where(lane == i1, before, 0.0), axis=-1, keepdims=True)
        rank2 = jnp.sum(jnp.where(lane == i2, before, 0.0), axis=-1, keepdims=True)

        route = jnp.where(lane_i == 0, i1, 0.0)
        route = jnp.where(lane_i == 1, i2, route)
        route = jnp.where(lane_i == 2, w1, route)
        route = jnp.where(lane_i == 3, w2, route)
        route = jnp.where(lane_i == 4, rank1, route)
        route = jnp.where(lane_i == 5, rank2, route)
        route_ref[rows, :] = route
        route_t_ref[:, rows] = route.T[0:ROUTE_ROWS, :]
        return count + jnp.sum(oh, axis=0, keepdims=True)

    n_chain = x_ref.shape[0] // sub
    rows = [slice(c * sub, (c + 1) * sub) for c in range(n_chain)]
    count = cnt_ref[...]
    mix = mix_stage(rows[0])
    for c in range(n_chain):
        mix_next = mix_stage(rows[c + 1]) if c + 1 < n_chain else None
        count = route_stage(rows[c], mix, count)
        mix = mix_next
    cnt_ref[...] = count


def _merge(x2, gi, bi, gates, ys, ya, wbs, wba, wo, g1, b1, wr, br):
    t_rows = x2.shape[0]
    tm = MERGE_TM
    assert t_rows % tm == 0 and tm % MERGE_SUB == 0
    row = lambda i: (i, 0)
    const = lambda i: (0, 0)
    return pl.pallas_call(
        _merge_kernel,
        grid=(t_rows // tm,),
        in_specs=[
            pl.BlockSpec((tm, D_MODEL), row),
            pl.BlockSpec((1, D_MODEL), const),
            pl.BlockSpec((1, D_MODEL), const),
            pl.BlockSpec((tm, GATE_COLS), row),
            pl.BlockSpec((tm, SSM_WIDTH), row),
            pl.BlockSpec((tm, ATTN_WIDTH), row),
            pl.BlockSpec((SSM_WIDTH, D_MODEL), const, pipeline_mode=pl.Buffered(1)),
            pl.BlockSpec((ATTN_WIDTH, D_MODEL), const, pipeline_mode=pl.Buffered(1)),
            pl.BlockSpec((D_MODEL, D_MODEL), const, pipeline_mode=pl.Buffered(1)),
            pl.BlockSpec((1, D_MODEL), const),
            pl.BlockSpec((1, D_MODEL), const),
            pl.BlockSpec((D_MODEL, LANES), const),
            pl.BlockSpec((1, LANES), const),
        ],
        out_specs=[
            pl.BlockSpec((tm, D_MODEL), row),
            pl.BlockSpec((tm, D_MODEL // 2), row),
            pl.BlockSpec((tm, LANES), row),
            pl.BlockSpec((ROUTE_ROWS, tm), lambda i: (0, i)),
            pl.BlockSpec((1, LANES), const),
        ],
        out_shape=[
            jax.ShapeDtypeStruct((t_rows, D_MODEL), jnp.float32),
            jax.ShapeDtypeStruct((t_rows, D_MODEL // 2), jnp.uint32),
            jax.ShapeDtypeStruct((t_rows, LANES), jnp.float32),
            jax.ShapeDtypeStruct((ROUTE_ROWS, t_rows), jnp.float32),
            jax.ShapeDtypeStruct((1, LANES), jnp.float32),
        ],
        compiler_params=pltpu.CompilerParams(
            dimension_semantics=("arbitrary",), vmem_limit_bytes=VMEM_LIMIT),
        name="merge",
    )(x2, gi, bi, gates, ys, ya, wbs, wba, wo, g1, b1, wr, br)


def _row_map_kernel(lo_ref, hi_ref, dest0_ref, dest1_ref, table_ref, *, t_rows):
    i = pl.program_id(0)
    tm = ROWMAP_TM
    stride = t_rows + DUMP_ROWS

    @pl.when(i == 0)
    def _():
        for e in range(N_EXPERTS + 1):
            def fill(r, carry):
                table_ref[r] = t_rows + (r & (DUMP_ROWS - 1))
                return carry
            lax.fori_loop(lo_ref[e], hi_ref[e], fill, 0)

    base = i * tm
    for a in range(tm):
        table_ref[dest0_ref[0, a]] = base + a
        table_ref[dest1_ref[0, a]] = base + (a + stride)


def _row_map(fill_lo, fill_hi, dest4, n_rows, t_rows):
    tm = ROWMAP_TM
    assert t_rows % tm == 0 and ROW_BLOCK & (ROW_BLOCK - 1) == 0
    grid_spec = pltpu.PrefetchScalarGridSpec(
        num_scalar_prefetch=2,
        grid=(t_rows // tm,),
        in_specs=[pl.BlockSpec((None, None, 1, tm), lambda i, lo, hi, k=k: (k, i, 0, 0), memory_space=pltpu.SMEM)
                  for k in range(TOP_K)],
        out_specs=pl.BlockSpec(memory_space=pltpu.SMEM),
    )
    return pl.pallas_call(
        functools.partial(_row_map_kernel, t_rows=t_rows),
        grid_spec=grid_spec,
        out_shape=jax.ShapeDtypeStruct((n_rows,), jnp.int32),
        compiler_params=pltpu.CompilerParams(dimension_semantics=("arbitrary",)),
        name="row_map",
    )(fill_lo, fill_hi, dest4, dest4)


def _moe_kernel(be_ref, nu_ref, src0_ref, src1_ref, src2_ref, dst0_ref, dst1_ref, h1_hbm, wg_ref, wu_ref,
                wd_ref, ysc_hbm, xbuf0, xbuf1, xbuf2, ybuf0, ybuf1, ybuf2, wgb, wub, wdb, gsem, ssem):
    i = pl.program_id(0)
    n_used = nu_ref[0]
    rb = ROW_BLOCK
    ns = MOE_SLOTS
    xbuf = (xbuf0, xbuf1, xbuf2)
    ybuf = (ybuf0, ybuf1, ybuf2)

    def start_gather(idx_ref, s):
        for r in range(rb):
            pltpu.make_async_copy(h1_hbm.at[pl.ds(idx_ref[0, r], 1)], xbuf[s].at[pl.ds(r, 1)],
                                  gsem.at[s]).start(priority=r % 2)

    def wait_gather(s):
        pltpu.make_async_copy(h1_hbm.at[pl.ds(0, rb)], xbuf[s], gsem.at[s]).wait()

    def start_scatter(idx_ref, s):
        for r in range(rb):
            pltpu.make_async_copy(ybuf[s].at[pl.ds(r, 1)], ysc_hbm.at[pl.ds(idx_ref[0, r], 1)],
                                  ssem.at[s]).start(priority=r % 2)

    def wait_scatter(s):
        pltpu.make_async_copy(ybuf[s], ysc_hbm.at[pl.ds(0, rb)], ssem.at[s]).wait()

    @pl.when(i == 0)
    def _():
        ybuf1[...] = jnp.zeros_like(ybuf1)
        ybuf2[...] = jnp.zeros_like(ybuf2)
        start_gather(src0_ref, 0)
        start_gather(src1_ref, 1)
        start_scatter(dst0_ref, 1)

    expert_changed = jnp.logical_or(i == 0, be_ref[i] != be_ref[jnp.maximum(i - 1, 0)])

    @pl.when(jnp.logical_and(i <= n_used, expert_changed))
    def _():
        wgb[...] = wg_ref[...].astype(jnp.bfloat16)
        wub[...] = wu_ref[...].astype(jnp.bfloat16)
        wdb[...] = wd_ref[...].astype(jnp.bfloat16)

    def step(slot):
        ahead = (slot + 2) % ns
        wait_gather(slot)

        @pl.when(i >= 1)
        def _():
            wait_scatter(slot)

        start_gather(src2_ref, ahead)
        start_scatter(dst1_ref, ahead)
        xb = jnp.concatenate([half.astype(jnp.bfloat16) for half in _unpack_halves(xbuf[slot][...])], axis=1)
        g = jnp.dot(xb, wgb[...], preferred_element_type=jnp.float32)
        u = jnp.dot(xb, wub[...], preferred_element_type=jnp.float32)
        hb = (jax.nn.silu(g) * u).astype(jnp.bfloat16)
        ybuf[slot][...] = _pack_halves(jnp.dot(hb, wdb[...], preferred_element_type=jnp.float32))

        @pl.when(i == n_used)
        def _():
            for s in ((slot + 1) % ns, ahead):
                wait_gather(s)
                wait_scatter(s)

    for s in range(ns):
        pl.when(jnp.logical_and(i <= n_used, i % ns == s))(functools.partial(step, s))


def _moe(block_exp, n_used, src_blocks, dst_blocks, h1, wg, wu, wd, n_blocks):
    t_rows = h1.shape[0]
    rb = ROW_BLOCK
    assert MOE_SLOTS == 3
    wsel = lambda i, be, nu: (be[i], 0, 0)
    smem_blk = lambda off: pl.BlockSpec((None, 1, rb), lambda i, be, nu: (i + off, 0, 0),
                                        memory_space=pltpu.SMEM)
    row_buf = pltpu.VMEM((rb, D_MODEL // 2), jnp.uint32)
    grid_spec = pltpu.PrefetchScalarGridSpec(
        num_scalar_prefetch=2,
        grid=(n_blocks + 1,),
        in_specs=[
            smem_blk(0), smem_blk(1), smem_blk(2), smem_blk(0), smem_blk(1),
            pl.BlockSpec(memory_space=pl.ANY),
            pl.BlockSpec((None, D_MODEL, D_FF_EXPERT), wsel),
            pl.BlockSpec((None, D_MODEL, D_FF_EXPERT), wsel),
            pl.BlockSpec((None, D_FF_EXPERT, D_MODEL), wsel),
        ],
        out_specs=pl.BlockSpec(memory_space=pl.ANY),
        scratch_shapes=[row_buf] * (2 * MOE_SLOTS) + [
            pltpu.VMEM((D_MODEL, D_FF_EXPERT), jnp.bfloat16),
            pltpu.VMEM((D_MODEL, D_FF_EXPERT), jnp.bfloat16),
            pltpu.VMEM((D_FF_EXPERT, D_MODEL), jnp.bfloat16),
            pltpu.SemaphoreType.DMA((MOE_SLOTS,)),
            pltpu.SemaphoreType.DMA((MOE_SLOTS,)),
        ],
    )
    return pl.pallas_call(
        _moe_kernel,
        grid_spec=grid_spec,
        out_shape=jax.ShapeDtypeStruct((TOP_K * t_rows + 2 * rb, D_MODEL // 2), jnp.uint32),
        compiler_params=pltpu.CompilerParams(
            dimension_semantics=("arbitrary",), vmem_limit_bytes=VMEM_LIMIT),
        name="experts",
    )(block_exp, n_used, src_blocks, src_blocks, src_blocks, dst_blocks, dst_blocks, h1, wg, wu, wd)


def _combine_kernel(h1_ref, route_ref, y0_ref, y1_ref, g_ref, b_ref, o_ref):
    route = route_ref[...]
    y0 = _unpack_halves(y0_ref[...])
    y1 = _unpack_halves(y1_ref[...])
    ffn = jnp.concatenate([route[:, 2:3] * a + route[:, 3:4] * b for a, b in zip(y0, y1)], axis=1)
    o_ref[...] = _layer_norm(DEEPNORM_ALPHA * h1_ref[...] + ffn, g_ref[...], b_ref[...])


def _combine(h1, route, ysc, g2, b2):
    t_rows = h1.shape[0]
    tm = COMBINE_TM
    assert t_rows % tm == 0 and ROW_BLOCK % tm == 0
    second = (t_rows + DUMP_ROWS) // tm
    row = lambda i: (i, 0)
    const = lambda i: (0, 0)
    return pl.pallas_call(
        _combine_kernel,
        grid=(t_rows // tm,),
        in_specs=[
            pl.BlockSpec((tm, D_MODEL), row),
            pl.BlockSpec((tm, LANES), row),
            pl.BlockSpec((tm, D_MODEL // 2), row),
            pl.BlockSpec((tm, D_MODEL // 2), lambda i: (i + second, 0)),
            pl.BlockSpec((1, D_MODEL), const),
            pl.BlockSpec((1, D_MODEL), const),
        ],
        out_specs=pl.BlockSpec((tm, D_MODEL), row),
        out_shape=jax.ShapeDtypeStruct((t_rows, D_MODEL), jnp.float32),
        compiler_params=pltpu.CompilerParams(
            dimension_semantics=("parallel",), vmem_limit_bytes=VMEM_LIMIT),
        name="combine",
    )(h1, route, ysc, ysc, g2, b2)


def kernel(x, meta_tokens, ln_in_g, ln_in_b, w_in, ssm_a_re, ssm_a_im, ssm_log_dt, ssm_b_re, ssm_b_im,
           ssm_c_re, ssm_c_im, ssm_d, ssm_w_glu, attn_lambda_q1, attn_lambda_k1, attn_lambda_q2,
           attn_lambda_k2, attn_subln_g, w_br_ssm, w_br_attn, w_o, ln1_g, ln1_b, router_g_w, router_g_b,
           router_e_w, router_e_b, exp_w_gate, exp_w_up, exp_w_down, ln2_g, ln2_b):
    f32, bf16 = jnp.float32, jnp.bfloat16
    nb, seq, d = x.shape
    assert d == D_MODEL and w_in.shape[0] == DEPTH == 1
    t_rows = nb * seq
    l = 0
    lambda_init = 0.8 - 0.6 * math.exp(-0.3 * l)
    row2 = lambda v: v.reshape(1, -1).astype(f32)

    x2 = x.reshape(t_rows, d)
    w_u = w_in[l, :, :SSM_WIDTH].astype(bf16)
    q_end = SSM_WIDTH + ATTN_WIDTH
    w_rest = jnp.concatenate([w_in[l, :, SSM_WIDTH + QKV_COLS:], w_in[l, :, SSM_WIDTH:q_end] * LOG2_E,
                              w_in[l, :, q_end:SSM_WIDTH + QKV_COLS]], axis=1).astype(bf16)
    gi, bi = row2(ln_in_g), row2(ln_in_b)
    u, proj = _ln_proj(x2, gi, bi, w_u, w_rest, min(PROJ_TM, seq), GATE_COLS)
    u_meta, proj_meta = _ln_proj(meta_tokens.astype(f32), gi, bi, w_u, w_rest, N_META, 0,
                                 first_col=GATE_COLS + ATTN_WIDTH)

    bb, ab, cm = _ssm_params(ssm_a_re[l], ssm_a_im[l], ssm_log_dt[l], ssm_b_re[l], ssm_b_im[l],
                             ssm_c_re[l], ssm_c_im[l])
    u_tm = u.reshape(nb, seq, SSM_WIDTH).transpose(1, 0, 2).reshape(seq * nb, SSM_WIDTH)
    u_meta_rows = jnp.repeat(u_meta, nb, axis=0)
    y_ssm_tm = _ssm(u_meta_rows, u_tm, bb, ab, cm, row2(ssm_d[l]), ssm_w_glu[l].astype(bf16), nb)
    y_ssm = y_ssm_tm.reshape(seq, nb, SSM_WIDTH).transpose(1, 0, 2).reshape(t_rows, SSM_WIDTH)

    lam = (jnp.exp(jnp.sum(attn_lambda_q1[l].astype(f32) * attn_lambda_k1[l].astype(f32)))
           - jnp.exp(jnp.sum(attn_lambda_q2[l].astype(f32) * attn_lambda_k2[l].astype(f32)))
           + lambda_init).reshape(1)
    g_scaled = row2(attn_subln_g[l]) * (1.0 - lambda_init)
    y_attn = _attention(lam, proj.reshape(nb, seq, GATE_COLS + QKV_COLS), proj_meta, g_scaled, nb, seq)
    y_attn = y_attn.reshape(t_rows, ATTN_WIDTH)

    w_r = jnp.concatenate([router_e_w[l].astype(f32), router_g_w[l].astype(f32)], axis=1)
    w_r = jnp.pad(w_r, ((0, 0), (0, LANES - w_r.shape[1])))
    b_r = jnp.concatenate([router_e_b[l].astype(f32), router_g_b[l].astype(f32)])
    b_r = jnp.pad(b_r, (0, LANES - b_r.shape[0])).reshape(1, LANES)
    h1, h1_packed, route, route_t, cnt = _merge(x2, gi, bi, proj, y_ssm, y_attn, w_br_ssm[l].astype(bf16),
                                       w_br_attn[l].astype(bf16), w_o[l].astype(bf16), row2(ln1_g[l]),
                                       row2(ln1_b[l]), w_r.astype(bf16), b_r)

    counts = cnt[0, :N_EXPERTS].astype(jnp.int32)
    padded = (counts + ROW_BLOCK - 1) // ROW_BLOCK * ROW_BLOCK
    pad_ends = jnp.cumsum(padded)
    pad_starts = pad_ends - padded
    expert = route_t[0:TOP_K].astype(jnp.int32)
    rank = route_t[4:4 + TOP_K].astype(jnp.int32)
    first_row = jnp.sum(jnp.where(expert[None] == jnp.arange(N_EXPERTS, dtype=jnp.int32)[:, None, None],
                                  pad_starts[:, None, None], 0), axis=0)
    dest = first_row + rank
    n_blocks = -(-(t_rows * TOP_K + N_EXPERTS * (ROW_BLOCK - 1)) // ROW_BLOCK)
    block_start = jnp.arange(n_blocks, dtype=jnp.int32) * ROW_BLOCK
    block_exp = jnp.minimum(jnp.sum(pad_ends[None, :] <= block_start[:, None], axis=1),
                            N_EXPERTS - 1).astype(jnp.int32)
    block_exp = jnp.concatenate([block_exp, block_exp[-1:]])
    n_used = (pad_ends[-1:] // ROW_BLOCK).astype(jnp.int32)

    rb = ROW_BLOCK
    stride = t_rows + DUMP_ROWS
    fill_lo = jnp.concatenate([pad_starts + counts, pad_ends[-1:]]).astype(jnp.int32)
    fill_hi = jnp.concatenate([pad_ends, jnp.full((1,), n_blocks * rb)]).astype(jnp.int32)
    row_map = _row_map(fill_lo, fill_hi, dest.reshape(TOP_K, t_rows // ROWMAP_TM, 1, ROWMAP_TM),
                       n_blocks * rb, t_rows)
    dst_blocks = jnp.concatenate([t_rows + jnp.arange(DUMP_ROWS, dtype=jnp.int32), row_map]).reshape(
        n_blocks + 2, 1, rb)
    src = jnp.minimum(row_map % stride, t_rows - 1)
    src_blocks = jnp.concatenate([src, jnp.zeros((3 * rb,), jnp.int32)]).reshape(n_blocks + 3, 1, rb)

    ysc = _moe(block_exp, n_used, src_blocks, dst_blocks, h1_packed, exp_w_gate[l], exp_w_up[l],
               exp_w_down[l], n_blocks)
    out = _combine(h1, route, ysc, row2(ln2_g[l]), row2(ln2_b[l]))
    return out.reshape(nb, seq, d).astype(x.dtype)
```

```python
import functools
import math

import jax
import jax.numpy as jnp
from jax import lax
from jax.experimental import pallas as pl
from jax.experimental.pallas import tpu as pltpu

D_MODEL = 2048
N_META = 16
CHUNK = 64
ATTN_HEADS = 8
ATTN_HEAD_DIM = 64
ATTN_VDIM = 2 * ATTN_HEAD_DIM
ATTN_WIDTH = ATTN_HEADS * ATTN_VDIM
SSM_WIDTH = D_MODEL // 4
SSM_GROUP = 16
SSM_GROUPS = SSM_WIDTH // SSM_GROUP
SSM_STATE = 64
SSM_COLS = SSM_GROUPS * SSM_STATE
N_EXPERT_GROUPS = 4
EXPERTS_PER_GROUP = 8
N_EXPERTS = N_EXPERT_GROUPS * EXPERTS_PER_GROUP
TOP_K = 2
D_FF_EXPERT = D_MODEL // 4
QKV_COLS = 3 * ATTN_WIDTH
GATE_COLS = 2 * D_MODEL
IN_COLS = SSM_WIDTH + QKV_COLS + GATE_COLS
LN_EPS = 1e-5
DEPTH = 1
DEEPNORM_ALPHA = (2.0 * DEPTH) ** 0.25
LANES = 128
NEG_INF = -1e30
LOG2_E = math.log2(math.e)

PROJ_TN = 1024
PROJ_TM = 1024
SSM_TC = 32
SSM_CB = 512
SSM_HALVES = 2
ATTN_TQ = 256
MERGE_TM = 256
ROW_BLOCK = 256
COMBINE_TM = 256
ROWMAP_TM = 512
MOE_SLOTS = 3
DUMP_ROWS = 2 * ROW_BLOCK
ROUTE_ROWS = 8
VMEM_LIMIT = 56 * 1024 * 1024


def _pack_halves(x):
    n = x.shape[1] // 2
    bits = lambda v: lax.bitcast_convert_type(v.astype(jnp.bfloat16).astype(jnp.float32), jnp.uint32)
    return bits(x[:, :n]) | (bits(x[:, n:]) >> 16)


def _unpack_halves(w):
    hi = lax.bitcast_convert_type(w & jnp.uint32(0xFFFF0000), jnp.float32)
    lo = lax.bitcast_convert_type(w << 16, jnp.float32)
    return hi, lo


def _layer_norm(x, g, b):
    mu = jnp.mean(x, axis=-1, keepdims=True)
    xc = x - mu
    var = jnp.mean(xc * xc, axis=-1, keepdims=True)
    return xc * lax.rsqrt(var + LN_EPS) * g + b


def _ln_proj_kernel(x_ref, g_ref, b_ref, wu_ref, w_ref, u_ref, proj_ref, xn_ref, *, n_gate_blocks):
    j = pl.program_id(1)

    @pl.when(j == 0)
    def _():
        xn = _layer_norm(x_ref[...], g_ref[...], b_ref[...]).astype(jnp.bfloat16)
        xn_ref[...] = xn
        u_ref[...] = jnp.dot(xn, wu_ref[...], preferred_element_type=jnp.float32)

    acc = jnp.dot(xn_ref[...], w_ref[...], preferred_element_type=jnp.float32)
    if n_gate_blocks:
        acc = jnp.where(j < n_gate_blocks, jax.nn.sigmoid(acc), acc)
    proj_ref[...] = acc.astype(jnp.bfloat16)


def _ln_proj(x2, g, b, w_u, w_rest, tm, gate_cols, first_col=0, cols=None):
    t_rows = x2.shape[0]
    cols = w_rest.shape[1] - first_col if cols is None else cols
    n_col = cols // PROJ_TN
    first_blk = first_col // PROJ_TN
    assert cols % PROJ_TN == 0 and gate_cols % PROJ_TN == 0 and first_col % PROJ_TN == 0 and t_rows % tm == 0
    return pl.pallas_call(
        functools.partial(_ln_proj_kernel, n_gate_blocks=gate_cols // PROJ_TN),
        grid=(t_rows // tm, n_col),
        in_specs=[
            pl.BlockSpec((tm, D_MODEL), lambda i, j: (i, 0)),
            pl.BlockSpec((1, D_MODEL), lambda i, j: (0, 0)),
            pl.BlockSpec((1, D_MODEL), lambda i, j: (0, 0)),
            pl.BlockSpec((D_MODEL, SSM_WIDTH), lambda i, j: (0, 0)),
            pl.BlockSpec((D_MODEL, PROJ_TN), lambda i, j: (0, j + first_blk)),
        ],
        out_specs=[
            pl.BlockSpec((tm, SSM_WIDTH), lambda i, j: (i, 0)),
            pl.BlockSpec((tm, PROJ_TN), lambda i, j: (i, j)),
        ],
        out_shape=[
            jax.ShapeDtypeStruct((t_rows, SSM_WIDTH), jnp.float32),
            jax.ShapeDtypeStruct((t_rows, cols), jnp.bfloat16),
        ],
        scratch_shapes=[pltpu.VMEM((tm, D_MODEL), jnp.bfloat16)],
        compiler_params=pltpu.CompilerParams(
            dimension_semantics=("parallel", "arbitrary"), vmem_limit_bytes=VMEM_LIMIT),
        name="ln_proj",
    )(x2, g, b, w_u, w_rest)


def _ssm_kernel(um_ref, u_ref, bb_ref, ab_ref, cm_ref, d_ref, wglu_ref, y_ref, state_ref, bu_ref, *, nb):
    half_u = SSM_WIDTH // SSM_HALVES
    half_c = SSM_COLS // SSM_HALVES

    half_cols = lambda h: slice(2 * half_c * h, 2 * half_c * (h + 1))
    groups = [pl.ds(8 * g, 8) for g in range(nb // 8)]

    def expand(u_bf16, h, n_steps):
        bu_ref[pl.ds(0, n_steps * nb), half_cols(h)] = jnp.dot(
            u_bf16[:, half_u * h:half_u * (h + 1)], bb_ref[h], preferred_element_type=jnp.float32)

    def scan_half(h, n_steps):
        for cb in range(half_c // SSM_CB):
            off = cb * SSM_CB
            re_cols = pl.ds(2 * half_c * h + off, SSM_CB)
            im_cols = pl.ds(2 * half_c * h + half_c + off, SSM_CB)
            a_re = ab_ref[0, :, half_c * h + off:half_c * h + off + SSM_CB]
            a_im = ab_ref[1, :, half_c * h + off:half_c * h + off + SSM_CB]
            carry = [(state_ref[grp, re_cols], state_ref[grp, im_cols]) for grp in groups]
            for t in range(n_steps):
                for g, (s_re, s_im) in enumerate(carry):
                    rows = pl.ds(t * nb + 8 * g, 8)
                    carry[g] = (a_re * s_re - a_im * s_im + bu_ref[rows, re_cols],
                                a_re * s_im + a_im * s_re + bu_ref[rows, im_cols])
                    bu_ref[rows, re_cols], bu_ref[rows, im_cols] = carry[g]
            for grp, (s_re, s_im) in zip(groups, carry):
                state_ref[grp, re_cols] = s_re
                state_ref[grp, im_cols] = s_im

    def readout(h):
        return jnp.dot(bu_ref[:, half_cols(h)].astype(jnp.bfloat16), cm_ref[h], preferred_element_type=jnp.float32)

    @pl.when(pl.program_id(0) == 0)
    def _():
        state_ref[...] = jnp.zeros_like(state_ref)
        u_meta = um_ref[...].astype(jnp.bfloat16)
        for h in range(SSM_HALVES):
            expand(u_meta, h, N_META)
            scan_half(h, N_META)

    u = u_ref[...]
    u_bf16 = u.astype(jnp.bfloat16)
    for h in range(SSM_HALVES):
        expand(u_bf16, h, SSM_TC)
    y_halves = []
    for h in range(SSM_HALVES):
        scan_half(h, SSM_TC)
        y_halves.append(readout(h))
    y = jnp.concatenate(y_halves, axis=1)
    y = jax.nn.gelu(y + d_ref[...] * u)
    gate = jnp.dot(y.astype(jnp.bfloat16), wglu_ref[...], preferred_element_type=jnp.float32)
    y_ref[...] = (y * jax.nn.sigmoid(gate)).astype(y_ref.dtype)


def _ssm(u_meta_rows, u_tm, bb, ab, cm, d, wglu, nb):
    rows = u_tm.shape[0]
    blk = SSM_TC * nb
    assert rows % blk == 0 and nb % 8 == 0 and N_META <= SSM_TC
    const = lambda i: (0, 0)
    return pl.pallas_call(
        functools.partial(_ssm_kernel, nb=nb),
        grid=(rows // blk,),
        in_specs=[
            pl.BlockSpec((N_META * nb, SSM_WIDTH), const),
            pl.BlockSpec((blk, SSM_WIDTH), lambda i: (i, 0)),
            pl.BlockSpec((SSM_HALVES, SSM_WIDTH // SSM_HALVES, 2 * SSM_COLS // SSM_HALVES), lambda i: (0, 0, 0)),
            pl.BlockSpec((2, 8, SSM_COLS), lambda i: (0, 0, 0)),
            pl.BlockSpec((SSM_HALVES, 2 * SSM_COLS // SSM_HALVES, SSM_WIDTH // SSM_HALVES), lambda i: (0, 0, 0)),
            pl.BlockSpec((1, SSM_WIDTH), const),
            pl.BlockSpec((SSM_WIDTH, SSM_WIDTH), const),
        ],
        out_specs=pl.BlockSpec((blk, SSM_WIDTH), lambda i: (i, 0)),
        out_shape=jax.ShapeDtypeStruct((rows, SSM_WIDTH), jnp.bfloat16),
        scratch_shapes=[
            pltpu.VMEM((nb, 2 * SSM_COLS), jnp.float32),
            pltpu.VMEM((blk, 2 * SSM_COLS), jnp.float32),
        ],
        compiler_params=pltpu.CompilerParams(
            dimension_semantics=("arbitrary",), vmem_limit_bytes=VMEM_LIMIT),
        name="ssm",
    )(u_meta_rows, u_tm, bb, ab, cm, d, wglu)


def _ssm_params(a_re, a_im, log_dt, b_re, b_im, c_re, c_im):
    f32 = jnp.float32
    lam_re = jnp.minimum(a_re.astype(f32), -1e-4)
    lam_im = a_im.astype(f32)
    dt = jnp.exp(log_dt.astype(f32))[:, None]
    mag = jnp.exp(lam_re * dt)
    ab_re = mag * jnp.cos(lam_im * dt)
    ab_im = mag * jnp.sin(lam_im * dt)
    den = lam_re * lam_re + lam_im * lam_im
    nr = ab_re - 1.0
    ni = ab_im
    z_re = (nr * lam_re + ni * lam_im) / den
    z_im = (ni * lam_re - nr * lam_im) / den
    br32 = b_re.astype(f32)
    bi32 = b_im.astype(f32)
    bb_re = z_re[..., None] * br32 - z_im[..., None] * bi32
    bb_im = z_re[..., None] * bi32 + z_im[..., None] * br32
    gh = SSM_GROUPS // SSM_HALVES
    eye = jnp.eye(gh, dtype=f32)
    split = lambda a: a.reshape((SSM_HALVES, gh) + a.shape[1:])
    exp_re = jnp.einsum('kgpc,gh->kgchp', split(bb_re), eye).reshape(SSM_HALVES, gh * SSM_GROUP, gh * SSM_STATE)
    exp_im = jnp.einsum('kgpc,gh->kgchp', split(bb_im), eye).reshape(SSM_HALVES, gh * SSM_GROUP, gh * SSM_STATE)
    bb = jnp.concatenate([exp_re, exp_im], axis=2)
    ro_re = jnp.einsum('kgcp,gh->kgphc', split(c_re.astype(f32)), eye).reshape(
        SSM_HALVES, gh * SSM_STATE, gh * SSM_GROUP)
    ro_im = jnp.einsum('kgcp,gh->kgphc', split(c_im.astype(f32)), eye).reshape(
        SSM_HALVES, gh * SSM_STATE, gh * SSM_GROUP)
    cm = jnp.concatenate([ro_re, -ro_im], axis=1)
    ab = jnp.stack([ab_re.reshape(SSM_COLS), ab_im.reshape(SSM_COLS)], axis=0)
    ab = jnp.broadcast_to(ab[:, None, :], (2, 8, SSM_COLS))
    return bb.astype(jnp.bfloat16), ab, cm.astype(jnp.bfloat16)


def _attn_kernel(lam_ref, q_ref, k_ref, v_ref, km_ref, vm_ref, g_ref, o_ref, vt_ref, *, scale, seq):
    tq = ATTN_TQ
    f32, bf16 = jnp.float32, jnp.bfloat16
    vt_ref[...] = v_ref[...].astype(f32).T.astype(bf16)
    vm_t = vm_ref[...].astype(f32).T.astype(bf16)
    k_meta = km_ref[...]
    lane = lax.broadcasted_iota(jnp.int32, (tq, ATTN_VDIM), 1)
    key = lax.broadcasted_iota(jnp.int32, (tq, 2 * tq), 0)
    qry = lax.broadcasted_iota(jnp.int32, (tq, 2 * tq), 1) % tq
    diag_visible = qry // CHUNK >= key // CHUNK
    col_max = lambda a: jnp.max(a, axis=0, keepdims=True)
    col_sum = lambda a: jnp.sum(a, axis=0, keepdims=True)

    def scores(i):
        rows = slice(i * tq, (i + 1) * tq)
        q = q_ref[rows, :] * scale
        zero = jnp.zeros_like(q)
        qq = jnp.concatenate([jnp.where(lane < ATTN_HEAD_DIM, q, zero),
                              jnp.where(lane >= ATTN_HEAD_DIM, q, zero)], axis=0)
        scores_t = lambda kb: lax.dot_general(kb, qq, (((1,), (1,)), ((), ())), preferred_element_type=f32)
        s_meta = scores_t(k_meta)
        s_diag = jnp.where(diag_visible, scores_t(k_ref[rows, :]), NEG_INF)
        s_prev = scores_t(k_ref[0:i * tq, :]) if i > 0 else None
        return s_meta, s_diag, s_prev

    def softmax(i, s):
        s_meta, s_diag, s_prev = s
        m = jnp.maximum(col_max(s_meta), col_max(s_diag))
        if i > 0:
            m = jnp.maximum(m, col_max(s_prev))
        p_meta = jnp.exp2(s_meta - m)
        p_diag = jnp.exp2(s_diag - m)
        l = col_sum(p_meta) + col_sum(p_diag)
        p_prev = None
        if i > 0:
            p_prev = jnp.exp2(s_prev - m)
            l = l + col_sum(p_prev)
            p_prev = p_prev.astype(bf16)
        return p_meta.astype(bf16), p_diag.astype(bf16), p_prev, l

    def weighted_values(i, p):
        p_meta, p_diag, p_prev, l = p
        rows = slice(i * tq, (i + 1) * tq)
        acc = (jnp.dot(vm_t, p_meta, preferred_element_type=f32)
               + jnp.dot(vt_ref[:, rows], p_diag, preferred_element_type=f32))
        if i > 0:
            acc = acc + jnp.dot(vt_ref[:, 0:i * tq], p_prev, preferred_element_type=f32)
        o_all = acc / l
        o_t = o_all[:, :tq] - lam_ref[0] * o_all[:, tq:]
        o_t = o_t * lax.rsqrt(jnp.mean(o_t * o_t, axis=0, keepdims=True) + LN_EPS)
        o_ref[rows, :] = (o_t.T * g_ref[...]).astype(o_ref.dtype)

    n_blk = seq // tq
    s_next = scores(0)
    p_last = None
    for i in range(n_blk):
        s_cur = s_next
        s_next = scores(i + 1) if i + 1 < n_blk else None
        if p_last is not None:
            weighted_values(i - 1, p_last)
        p_last = softmax(i, s_cur)
    weighted_values(n_blk - 1, p_last)


def _attention(lam, proj, proj_meta, g_scaled, nb, seq):
    assert seq % ATTN_TQ == 0
    nh = ATTN_HEADS
    first = GATE_COLS // ATTN_VDIM
    head = lambda part: pl.BlockSpec((None, seq, ATTN_VDIM), lambda b, h: (b, 0, first + part * nh + h))
    return pl.pallas_call(
        functools.partial(_attn_kernel, scale=ATTN_HEAD_DIM ** -0.5, seq=seq),
        grid=(nb, nh),
        in_specs=[
            pl.BlockSpec(memory_space=pltpu.SMEM),
            head(0), head(1), head(2),
            pl.BlockSpec((N_META, ATTN_VDIM), lambda b, h: (0, h)),
            pl.BlockSpec((N_META, ATTN_VDIM), lambda b, h: (0, nh + h)),
            pl.BlockSpec((1, ATTN_VDIM), lambda b, h: (0, 0)),
        ],
        out_specs=pl.BlockSpec((None, seq, ATTN_VDIM), lambda b, h: (b, 0, h)),
        out_shape=jax.ShapeDtypeStruct((nb, seq, ATTN_WIDTH), jnp.bfloat16),
        scratch_shapes=[pltpu.VMEM((ATTN_VDIM, seq), jnp.bfloat16)],
        compiler_params=pltpu.CompilerParams(
            dimension_semantics=("parallel", "parallel"), vmem_limit_bytes=VMEM_LIMIT),
        name="attention",
    )(lam, proj, proj, proj, proj_meta, proj_meta, g_scaled)


def _merge_kernel(x_ref, gi_ref, bi_ref, gate_ref, ys_ref, ya_ref, wbs_ref, wba_ref, wo_ref,
                  g1_ref, b1_ref, wr_ref, br_ref, h1_ref, h1p_ref, route_ref, route_t_ref, cnt_ref, logits_ref):
    i = pl.program_id(0)
    tm = x_ref.shape[0]

    @pl.when(i == 0)
    def _():
        cnt_ref[...] = jnp.zeros_like(cnt_ref)
        logits_ref[...] = jnp.zeros_like(logits_ref)

    h = _layer_norm(x_ref[...], gi_ref[...], bi_ref[...])
    ps = jnp.dot(ys_ref[...], wbs_ref[...], preferred_element_type=jnp.float32)
    pa = jnp.dot(ya_ref[...], wba_ref[...], preferred_element_type=jnp.float32)

    logits = logits_ref[...]
    lane_i = lax.broadcasted_iota(jnp.int32, (tm, LANES), 1)
    lane = lane_i.astype(jnp.float32)
    is_group = jnp.logical_and(lane_i >= N_EXPERTS, lane_i < N_EXPERTS + N_EXPERT_GROUPS)
    lane_group = (lane_i // EXPERTS_PER_GROUP).astype(jnp.float32)

    def first_argmax(vals):
        mx = jnp.max(vals, axis=-1, keepdims=True)
        idx = jnp.min(jnp.where(vals == mx, lane, float(LANES)), axis=-1, keepdims=True)
        return mx, idx

    gl = jnp.where(is_group, logits, NEG_INF)
    gmax, glane = first_argmax(gl)
    g_val = 1.0 / jnp.sum(jnp.where(is_group, jnp.exp(gl - gmax), 0.0), axis=-1, keepdims=True)
    g_idx = glane - float(N_EXPERTS)
    in_group = jnp.logical_and(lane_i < N_EXPERTS, lane_group == g_idx)
    el = jnp.where(in_group, logits, NEG_INF)
    m1, i1 = first_argmax(el)
    el2 = jnp.where(lane == i1, NEG_INF, el)
    m2, i2 = first_argmax(el2)
    e2 = jnp.exp(m2 - m1)
    w1 = g_val / (1.0 + e2)
    w2 = g_val * e2 / (1.0 + e2)

    oh = jnp.where(jnp.logical_or(lane == i1, lane == i2), 1.0, 0.0)
    r_i = lax.broadcasted_iota(jnp.int32, (tm, tm), 0)
    c_i = lax.broadcasted_iota(jnp.int32, (tm, tm), 1)
    tri = jnp.where(c_i < r_i, 1.0, 0.0).astype(jnp.bfloat16)
    count = cnt_ref[...]
    before = jnp.dot(tri, oh.astype(jnp.bfloat16), preferred_element_type=jnp.float32) + count
    rank1 = jnp.sum(jnp.where(lane == i1, before, 0.0), axis=-1, keepdims=True)
    rank2 = jnp.sum(jnp.where(lane == i2, before, 0.0), axis=-1, keepdims=True)
    cnt_ref[...] = count + jnp.where(i >= 1, jnp.sum(oh, axis=0, keepdims=True), 0.0)

    route = jnp.where(lane_i == 0, i1, 0.0)
    route = jnp.where(lane_i == 1, i2, route)
    route = jnp.where(lane_i == 2, w1, route)
    route = jnp.where(lane_i == 3, w2, route)
    route = jnp.where(lane_i == 4, rank1, route)
    route = jnp.where(lane_i == 5, rank2, route)
    route_ref[...] = route
    route_t_ref[...] = route.T[0:ROUTE_ROWS, :]

    gates = gate_ref[...].astype(jnp.float32)
    merged = gates[:, :D_MODEL] * ps + gates[:, D_MODEL:] * pa
    mix = jnp.dot(merged.astype(jnp.bfloat16), wo_ref[...], preferred_element_type=jnp.float32)
    h1 = _layer_norm(DEEPNORM_ALPHA * h + mix, g1_ref[...], b1_ref[...])
    h1_ref[...] = h1
    h1p_ref[...] = _pack_halves(h1)
    logits_ref[...] = jnp.dot(h1.astype(jnp.bfloat16), wr_ref[...], preferred_element_type=jnp.float32) + br_ref[...]


def _merge(x2, gi, bi, gates, ys, ya, wbs, wba, wo, g1, b1, wr, br):
    t_rows = x2.shape[0]
    tm = MERGE_TM
    assert t_rows % tm == 0
    n_tiles = t_rows // tm
    row = lambda i: (jnp.minimum(i, n_tiles - 1), 0)
    routed = lambda i: (jnp.maximum(i - 1, 0), 0)
    const = lambda i: (0, 0)
    return pl.pallas_call(
        _merge_kernel,
        grid=(n_tiles + 1,),
        in_specs=[
            pl.BlockSpec((tm, D_MODEL), row),
            pl.BlockSpec((1, D_MODEL), const),
            pl.BlockSpec((1, D_MODEL), const),
            pl.BlockSpec((tm, GATE_COLS), row),
            pl.BlockSpec((tm, SSM_WIDTH), row),
            pl.BlockSpec((tm, ATTN_WIDTH), row),
            pl.BlockSpec((SSM_WIDTH, D_MODEL), const, pipeline_mode=pl.Buffered(1)),
            pl.BlockSpec((ATTN_WIDTH, D_MODEL), const, pipeline_mode=pl.Buffered(1)),
            pl.BlockSpec((D_MODEL, D_MODEL), const, pipeline_mode=pl.Buffered(1)),
            pl.BlockSpec((1, D_MODEL), const),
            pl.BlockSpec((1, D_MODEL), const),
            pl.BlockSpec((D_MODEL, LANES), const),
            pl.BlockSpec((1, LANES), const),
        ],
        out_specs=[
            pl.BlockSpec((tm, D_MODEL), row),
            pl.BlockSpec((tm, D_MODEL // 2), row),
            pl.BlockSpec((tm, LANES), routed),
            pl.BlockSpec((ROUTE_ROWS, tm), lambda i: (0, jnp.maximum(i - 1, 0))),
            pl.BlockSpec((1, LANES), const),
        ],
        scratch_shapes=[pltpu.VMEM((tm, LANES), jnp.float32)],
        out_shape=[
            jax.ShapeDtypeStruct((t_rows, D_MODEL), jnp.float32),
            jax.ShapeDtypeStruct((t_rows, D_MODEL // 2), jnp.uint32),
            jax.ShapeDtypeStruct((t_rows, LANES), jnp.float32),
            jax.ShapeDtypeStruct((ROUTE_ROWS, t_rows), jnp.float32),
            jax.ShapeDtypeStruct((1, LANES), jnp.float32),
        ],
        compiler_params=pltpu.CompilerParams(
            dimension_semantics=("arbitrary",), vmem_limit_bytes=VMEM_LIMIT),
        name="merge",
    )(x2, gi, bi, gates, ys, ya, wbs, wba, wo, g1, b1, wr, br)


def _row_map_kernel(lo_ref, hi_ref, dest0_ref, dest1_ref, table_ref, *, t_rows):
    i = pl.program_id(0)
    tm = ROWMAP_TM
    stride = t_rows + DUMP_ROWS

    @pl.when(i == 0)
    def _():
        for e in range(N_EXPERTS + 1):
            def fill(r, carry):
                table_ref[r] = t_rows + (r & (DUMP_ROWS - 1))
                return carry
            lax.fori_loop(lo_ref[e], hi_ref[e], fill, 0)

    base = i * tm
    for a in range(tm):
        table_ref[dest0_ref[0, a]] = base + a
        table_ref[dest1_ref[0, a]] = base + (a + stride)


def _row_map(fill_lo, fill_hi, dest4, n_rows, t_rows):
    tm = ROWMAP_TM
    assert t_rows % tm == 0 and ROW_BLOCK & (ROW_BLOCK - 1) == 0
    grid_spec = pltpu.PrefetchScalarGridSpec(
        num_scalar_prefetch=2,
        grid=(t_rows // tm,),
        in_specs=[pl.BlockSpec((None, None, 1, tm), lambda i, lo, hi, k=k: (k, i, 0, 0), memory_space=pltpu.SMEM)
                  for k in range(TOP_K)],
        out_specs=pl.BlockSpec(memory_space=pltpu.SMEM),
    )
    return pl.pallas_call(
        functools.partial(_row_map_kernel, t_rows=t_rows),
        grid_spec=grid_spec,
        out_shape=jax.ShapeDtypeStruct((n_rows,), jnp.int32),
        compiler_params=pltpu.CompilerParams(dimension_semantics=("arbitrary",)),
        name="row_map",
    )(fill_lo, fill_hi, dest4, dest4)


def _moe_kernel(be_ref, nu_ref, src0_ref, src1_ref, src2_ref, dst0_ref, dst1_ref, h1_hbm, wg_ref, wu_ref,
                wd_ref, ysc_hbm, xbuf0, xbuf1, xbuf2, ybuf0, ybuf1, ybuf2, wgb, wub, wdb, gsem, ssem):
    i = pl.program_id(0)
    n_used = nu_ref[0]
    rb = ROW_BLOCK
    ns = MOE_SLOTS
    xbuf = (xbuf0, xbuf1, xbuf2)
    ybuf = (ybuf0, ybuf1, ybuf2)

    def start_gather(idx_ref, s):
        for r in range(rb):
            pltpu.make_async_copy(h1_hbm.at[pl.ds(idx_ref[0, r], 1)], xbuf[s].at[pl.ds(r, 1)],
                                  gsem.at[s]).start(priority=r % 2)

    def wait_gather(s):
        pltpu.make_async_copy(h1_hbm.at[pl.ds(0, rb)], xbuf[s], gsem.at[s]).wait()

    def start_scatter(idx_ref, s):
        for r in range(rb):
            pltpu.make_async_copy(ybuf[s].at[pl.ds(r, 1)], ysc_hbm.at[pl.ds(idx_ref[0, r], 1)],
                                  ssem.at[s]).start(priority=r % 2)

    def wait_scatter(s):
        pltpu.make_async_copy(ybuf[s], ysc_hbm.at[pl.ds(0, rb)], ssem.at[s]).wait()

    @pl.when(i == 0)
    def _():
        ybuf1[...] = jnp.zeros_like(ybuf1)
        ybuf2[...] = jnp.zeros_like(ybuf2)
        start_gather(src0_ref, 0)
        start_gather(src1_ref, 1)
        start_scatter(dst0_ref, 1)

    expert_changed = jnp.logical_or(i == 0, be_ref[i] != be_ref[jnp.maximum(i - 1, 0)])

    @pl.when(jnp.logical_and(i <= n_used, expert_changed))
    def _():
        wgb[...] = wg_ref[...].astype(jnp.bfloat16)
        wub[...] = wu_ref[...].astype(jnp.bfloat16)
        wdb[...] = wd_ref[...].astype(jnp.bfloat16)

    def step(slot):
        ahead = (slot + 2) % ns
        wait_gather(slot)

        @pl.when(i >= 1)
        def _():
            wait_scatter(slot)

        start_gather(src2_ref, ahead)
        start_scatter(dst1_ref, ahead)
        xb = jnp.concatenate([half.astype(jnp.bfloat16) for half in _unpack_halves(xbuf[slot][...])], axis=1)
        g = jnp.dot(xb, wgb[...], preferred_element_type=jnp.float32)
        u = jnp.dot(xb, wub[...], preferred_element_type=jnp.float32)
        hb = (jax.nn.silu(g) * u).astype(jnp.bfloat16)
        ybuf[slot][...] = _pack_halves(jnp.dot(hb, wdb[...], preferred_element_type=jnp.float32))

        @pl.when(i == n_used)
        def _():
            for s in ((slot + 1) % ns, ahead):
                wait_gather(s)
                wait_scatter(s)

    for s in range(ns):
        pl.when(jnp.logical_and(i <= n_used, i % ns == s))(functools.partial(step, s))


def _moe(block_exp, n_used, src_blocks, dst_blocks, h1, wg, wu, wd, n_blocks):
    t_rows = h1.shape[0]
    rb = ROW_BLOCK
    assert MOE_SLOTS == 3
    wsel = lambda i, be, nu: (be[i], 0, 0)
    smem_blk = lambda off: pl.BlockSpec((None, 1, rb), lambda i, be, nu: (i + off, 0, 0),
                                        memory_space=pltpu.SMEM)
    row_buf = pltpu.VMEM((rb, D_MODEL // 2), jnp.uint32)
    grid_spec = pltpu.PrefetchScalarGridSpec(
        num_scalar_prefetch=2,
        grid=(n_blocks + 1,),
        in_specs=[
            smem_blk(0), smem_blk(1), smem_blk(2), smem_blk(0), smem_blk(1),
            pl.BlockSpec(memory_space=pl.ANY),
            pl.BlockSpec((None, D_MODEL, D_FF_EXPERT), wsel),
            pl.BlockSpec((None, D_MODEL, D_FF_EXPERT), wsel),
            pl.BlockSpec((None, D_FF_EXPERT, D_MODEL), wsel),
        ],
        out_specs=pl.BlockSpec(memory_space=pl.ANY),
        scratch_shapes=[row_buf] * (2 * MOE_SLOTS) + [
            pltpu.VMEM((D_MODEL, D_FF_EXPERT), jnp.bfloat16),
            pltpu.VMEM((D_MODEL, D_FF_EXPERT), jnp.bfloat16),
            pltpu.VMEM((D_FF_EXPERT, D_MODEL), jnp.bfloat16),
            pltpu.SemaphoreType.DMA((MOE_SLOTS,)),
            pltpu.SemaphoreType.DMA((MOE_SLOTS,)),
        ],
    )
    return pl.pallas_call(
        _moe_kernel,
        grid_spec=grid_spec,
        out_shape=jax.ShapeDtypeStruct((TOP_K * t_rows + 2 * rb, D_MODEL // 2), jnp.uint32),
        compiler_params=pltpu.CompilerParams(
            dimension_semantics=("arbitrary",), vmem_limit_bytes=VMEM_LIMIT),
        name="experts",
    )(block_exp, n_used, src_blocks, src_blocks, src_blocks, dst_blocks, dst_blocks, h1, wg, wu, wd)


def _combine_kernel(h1_ref, route_ref, y0_ref, y1_ref, g_ref, b_ref, o_ref):
    route = route_ref[...]
    y0 = _unpack_halves(y0_ref[...])
    y1 = _unpack_halves(y1_ref[...])
    ffn = jnp.concatenate([route[:, 2:3] * a + route[:, 3:4] * b for a, b in zip(y0, y1)], axis=1)
    o_ref[...] = _layer_norm(DEEPNORM_ALPHA * h1_ref[...] + ffn, g_ref[...], b_ref[...])


def _combine(h1, route, ysc, g2, b2):
    t_rows = h1.shape[0]
    tm = COMBINE_TM
    assert t_rows % tm == 0 and ROW_BLOCK % tm == 0
    second = (t_rows + DUMP_ROWS) // tm
    row = lambda i: (i, 0)
    const = lambda i: (0, 0)
    return pl.pallas_call(
        _combine_kernel,
        grid=(t_rows // tm,),
        in_specs=[
            pl.BlockSpec((tm, D_MODEL), row),
            pl.BlockSpec((tm, LANES), row),
            pl.BlockSpec((tm, D_MODEL // 2), row),
            pl.BlockSpec((tm, D_MODEL // 2), lambda i: (i + second, 0)),
            pl.BlockSpec((1, D_MODEL), const),
            pl.BlockSpec((1, D_MODEL), const),
        ],
        out_specs=pl.BlockSpec((tm, D_MODEL), row),
        out_shape=jax.ShapeDtypeStruct((t_rows, D_MODEL), jnp.float32),
        compiler_params=pltpu.CompilerParams(
            dimension_semantics=("parallel",), vmem_limit_bytes=VMEM_LIMIT),
        name="combine",
    )(h1, route, ysc, ysc, g2, b2)


def kernel(x, meta_tokens, ln_in_g, ln_in_b, w_in, ssm_a_re, ssm_a_im, ssm_log_dt, ssm_b_re, ssm_b_im,
           ssm_c_re, ssm_c_im, ssm_d, ssm_w_glu, attn_lambda_q1, attn_lambda_k1, attn_lambda_q2,
           attn_lambda_k2, attn_subln_g, w_br_ssm, w_br_attn, w_o, ln1_g, ln1_b, router_g_w, router_g_b,
           router_e_w, router_e_b, exp_w_gate, exp_w_up, exp_w_down, ln2_g, ln2_b):
    f32, bf16 = jnp.float32, jnp.bfloat16
    nb, seq, d = x.shape
    assert d == D_MODEL and w_in.shape[0] == DEPTH == 1
    t_rows = nb * seq
    l = 0
    lambda_init = 0.8 - 0.6 * math.exp(-0.3 * l)
    row2 = lambda v: v.reshape(1, -1).astype(f32)

    x2 = x.reshape(t_rows, d)
    w_u = w_in[l, :, :SSM_WIDTH].astype(bf16)
    q_end = SSM_WIDTH + ATTN_WIDTH
    w_rest = jnp.concatenate([w_in[l, :, SSM_WIDTH + QKV_COLS:], w_in[l, :, SSM_WIDTH:q_end] * LOG2_E,
                              w_in[l, :, q_end:SSM_WIDTH + QKV_COLS]], axis=1).astype(bf16)
    gi, bi = row2(ln_in_g), row2(ln_in_b)
    u, proj = _ln_proj(x2, gi, bi, w_u, w_rest, min(PROJ_TM, seq), GATE_COLS)
    u_meta, proj_meta = _ln_proj(meta_tokens.astype(f32), gi, bi, w_u, w_rest, N_META, 0,
                                 first_col=GATE_COLS + ATTN_WIDTH)

    bb, ab, cm = _ssm_params(ssm_a_re[l], ssm_a_im[l], ssm_log_dt[l], ssm_b_re[l], ssm_b_im[l],
                             ssm_c_re[l], ssm_c_im[l])
    u_tm = u.reshape(nb, seq, SSM_WIDTH).transpose(1, 0, 2).reshape(seq * nb, SSM_WIDTH)
    u_meta_rows = jnp.repeat(u_meta, nb, axis=0)
    y_ssm_tm = _ssm(u_meta_rows, u_tm, bb, ab, cm, row2(ssm_d[l]), ssm_w_glu[l].astype(bf16), nb)
    y_ssm = y_ssm_tm.reshape(seq, nb, SSM_WIDTH).transpose(1, 0, 2).reshape(t_rows, SSM_WIDTH)

    lam = (jnp.exp(jnp.sum(attn_lambda_q1[l].astype(f32) * attn_lambda_k1[l].astype(f32)))
           - jnp.exp(jnp.sum(attn_lambda_q2[l].astype(f32) * attn_lambda_k2[l].astype(f32)))
           + lambda_init).reshape(1)
    g_scaled = row2(attn_subln_g[l]) * (1.0 - lambda_init)
    y_attn = _attention(lam, proj.reshape(nb, seq, GATE_COLS + QKV_COLS), proj_meta, g_scaled, nb, seq)
    y_attn = y_attn.reshape(t_rows, ATTN_WIDTH)

    w_r = jnp.concatenate([router_e_w[l].astype(f32), router_g_w[l].astype(f32)], axis=1)
    w_r = jnp.pad(w_r, ((0, 0), (0, LANES - w_r.shape[1])))
    b_r = jnp.concatenate([router_e_b[l].astype(f32), router_g_b[l].astype(f32)])
    b_r = jnp.pad(b_r, (0, LANES - b_r.shape[0])).reshape(1, LANES)
    h1, h1_packed, route, route_t, cnt = _merge(x2, gi, bi, proj, y_ssm, y_attn, w_br_ssm[l].astype(bf16),
                                       w_br_attn[l].astype(bf16), w_o[l].astype(bf16), row2(ln1_g[l]),
                                       row2(ln1_b[l]), w_r.astype(bf16), b_r)

    counts = cnt[0, :N_EXPERTS].astype(jnp.int32)
    padded = (counts + ROW_BLOCK - 1) // ROW_BLOCK * ROW_BLOCK
    pad_ends = jnp.cumsum(padded)
    pad_starts = pad_ends - padded
    expert = route_t[0:TOP_K].astype(jnp.int32)
    rank = route_t[4:4 + TOP_K].astype(jnp.int32)
    first_row = jnp.sum(jnp.where(expert[None] == jnp.arange(N_EXPERTS, dtype=jnp.int32)[:, None, None],
                                  pad_starts[:, None, None], 0), axis=0)
    dest = first_row + rank
    n_blocks = -(-(t_rows * TOP_K + N_EXPERTS * (ROW_BLOCK - 1)) // ROW_BLOCK)
    block_start = jnp.arange(n_blocks, dtype=jnp.int32) * ROW_BLOCK
    block_exp = jnp.minimum(jnp.sum(pad_ends[None, :] <= block_start[:, None], axis=1),
                            N_EXPERTS - 1).astype(jnp.int32)
    block_exp = jnp.concatenate([block_exp, block_exp[-1:]])
    n_used = (pad_ends[-1:] // ROW_BLOCK).astype(jnp.int32)

    rb = ROW_BLOCK
    stride = t_rows + DUMP_ROWS
    fill_lo = jnp.concatenate([pad_starts + counts, pad_ends[-1:]]).astype(jnp.int32)
    fill_hi = jnp.concatenate([pad_ends, jnp.full((1,), n_blocks * rb)]).astype(jnp.int32)
    row_map = _row_map(fill_lo, fill_hi, dest.reshape(TOP_K, t_rows // ROWMAP_TM, 1, ROWMAP_TM),
                       n_blocks * rb, t_rows)
    dst_blocks = jnp.concatenate([t_rows + jnp.arange(DUMP_ROWS, dtype=jnp.int32), row_map]).reshape(
        n_blocks + 2, 1, rb)
    src = jnp.minimum(row_map % stride, t_rows - 1)
    src_blocks = jnp.concatenate([src, jnp.zeros((3 * rb,), jnp.int32)]).reshape(n_blocks + 3, 1, rb)

    ysc = _moe(block_exp, n_used, src_blocks, dst_blocks, h1_packed, exp_w_gate[l], exp_w_up[l],
               exp_w_down[l], n_blocks)
    out = _combine(h1, route, ysc, row2(ln2_g[l]), row2(ln2_b[l]))
    return out.reshape(nb, seq, d).astype(x.dtype)
```

```python
import functools
import math

import jax
import jax.numpy as jnp
from jax import lax
from jax.experimental import pallas as pl
from jax.experimental.pallas import tpu as pltpu

D_MODEL = 2048
N_META = 16
CHUNK = 64
ATTN_HEADS = 8
ATTN_HEAD_DIM = 64
ATTN_VDIM = 2 * ATTN_HEAD_DIM
ATTN_WIDTH = ATTN_HEADS * ATTN_VDIM
SSM_WIDTH = D_MODEL // 4
SSM_GROUP = 16
SSM_GROUPS = SSM_WIDTH // SSM_GROUP
SSM_STATE = 64
SSM_COLS = SSM_GROUPS * SSM_STATE
N_EXPERT_GROUPS = 4
EXPERTS_PER_GROUP = 8
N_EXPERTS = N_EXPERT_GROUPS * EXPERTS_PER_GROUP
TOP_K = 2
D_FF_EXPERT = D_MODEL // 4
QKV_COLS = 3 * ATTN_WIDTH
GATE_COLS = 2 * D_MODEL
IN_COLS = SSM_WIDTH + QKV_COLS + GATE_COLS
LN_EPS = 1e-5
DEPTH = 1
DEEPNORM_ALPHA = (2.0 * DEPTH) ** 0.25
LANES = 128
NEG_INF = -1e30
LOG2_E = math.log2(math.e)

PROJ_TN = 1024
PROJ_TM = 1024
SSM_TC = 32
SSM_CB = 512
SSM_HALVES = 2
ATTN_TQ = 256
MERGE_TM = 256
ROW_BLOCK = 256
COMBINE_TM = 512
ROWMAP_TM = 1024
MOE_SLOTS = 3
DUMP_ROWS = 2 * ROW_BLOCK
ROUTE_ROWS = 8
VMEM_LIMIT = 56 * 1024 * 1024


def _pack_halves(x):
    n = x.shape[1] // 2
    bits = lambda v: lax.bitcast_convert_type(v.astype(jnp.bfloat16).astype(jnp.float32), jnp.uint32)
    return bits(x[:, :n]) | (bits(x[:, n:]) >> 16)


def _unpack_halves(w):
    hi = lax.bitcast_convert_type(w & jnp.uint32(0xFFFF0000), jnp.float32)
    lo = lax.bitcast_convert_type(w << 16, jnp.float32)
    return hi, lo


def _layer_norm(x, g, b):
    mu = jnp.mean(x, axis=-1, keepdims=True)
    xc = x - mu
    var = jnp.mean(xc * xc, axis=-1, keepdims=True)
    return xc * lax.rsqrt(var + LN_EPS) * g + b


def _ln_proj_kernel(x_ref, g_ref, b_ref, wu_ref, w_ref, u_ref, proj_ref, xn_ref, *, n_gate_blocks):
    j = pl.program_id(1)

    @pl.when(j == 0)
    def _():
        xn = _layer_norm(x_ref[...], g_ref[...], b_ref[...]).astype(jnp.bfloat16)
        xn_ref[...] = xn
        u_ref[...] = jnp.dot(xn, wu_ref[...], preferred_element_type=jnp.float32)

    acc = jnp.dot(xn_ref[...], w_ref[...], preferred_element_type=jnp.float32)
    if n_gate_blocks:
        acc = jnp.where(j < n_gate_blocks, jax.nn.sigmoid(acc), acc)
    proj_ref[...] = acc.astype(jnp.bfloat16)


def _ln_proj(x2, g, b, w_u, w_rest, tm, gate_cols, first_col=0, cols=None):
    t_rows = x2.shape[0]
    cols = w_rest.shape[1] - first_col if cols is None else cols
    n_col = cols // PROJ_TN
    first_blk = first_col // PROJ_TN
    assert cols % PROJ_TN == 0 and gate_cols % PROJ_TN == 0 and first_col % PROJ_TN == 0 and t_rows % tm == 0
    return pl.pallas_call(
        functools.partial(_ln_proj_kernel, n_gate_blocks=gate_cols // PROJ_TN),
        grid=(t_rows // tm, n_col),
        in_specs=[
            pl.BlockSpec((tm, D_MODEL), lambda i, j: (i, 0)),
            pl.BlockSpec((1, D_MODEL), lambda i, j: (0, 0)),
            pl.BlockSpec((1, D_MODEL), lambda i, j: (0, 0)),
            pl.BlockSpec((D_MODEL, SSM_WIDTH), lambda i, j: (0, 0)),
            pl.BlockSpec((D_MODEL, PROJ_TN), lambda i, j: (0, j + first_blk)),
        ],
        out_specs=[
            pl.BlockSpec((tm, SSM_WIDTH), lambda i, j: (i, 0)),
            pl.BlockSpec((tm, PROJ_TN), lambda i, j: (i, j)),
        ],
        out_shape=[
            jax.ShapeDtypeStruct((t_rows, SSM_WIDTH), jnp.float32),
            jax.ShapeDtypeStruct((t_rows, cols), jnp.bfloat16),
        ],
        scratch_shapes=[pltpu.VMEM((tm, D_MODEL), jnp.bfloat16)],
        compiler_params=pltpu.CompilerParams(
            dimension_semantics=("parallel", "arbitrary"), vmem_limit_bytes=VMEM_LIMIT),
        name="ln_proj",
    )(x2, g, b, w_u, w_rest)


def _ssm_kernel(um_ref, u_ref, bb_ref, ab_ref, cm_ref, d_ref, wglu_ref, y_ref, state_ref, bu_ref, *, nb):
    half_u = SSM_WIDTH // SSM_HALVES
    half_c = SSM_COLS // SSM_HALVES

    half_cols = lambda h: slice(2 * half_c * h, 2 * half_c * (h + 1))
    groups = [pl.ds(8 * g, 8) for g in range(nb // 8)]

    def expand(u_bf16, h, n_steps):
        bu_ref[pl.ds(0, n_steps * nb), half_cols(h)] = jnp.dot(
            u_bf16[:, half_u * h:half_u * (h + 1)], bb_ref[h], preferred_element_type=jnp.float32)

    def scan_half(h, n_steps):
        for cb in range(half_c // SSM_CB):
            off = cb * SSM_CB
            re_cols = pl.ds(2 * half_c * h + off, SSM_CB)
            im_cols = pl.ds(2 * half_c * h + half_c + off, SSM_CB)
            a_re = ab_ref[0, :, half_c * h + off:half_c * h + off + SSM_CB]
            a_im = ab_ref[1, :, half_c * h + off:half_c * h + off + SSM_CB]
            carry = [(state_ref[grp, re_cols], state_ref[grp, im_cols]) for grp in groups]
            for t in range(n_steps):
                for g, (s_re, s_im) in enumerate(carry):
                    rows = pl.ds(t * nb + 8 * g, 8)
                    carry[g] = (a_re * s_re - a_im * s_im + bu_ref[rows, re_cols],
                                a_re * s_im + a_im * s_re + bu_ref[rows, im_cols])
                    bu_ref[rows, re_cols], bu_ref[rows, im_cols] = carry[g]
            for grp, (s_re, s_im) in zip(groups, carry):
                state_ref[grp, re_cols] = s_re
                state_ref[grp, im_cols] = s_im

    def readout(h):
        return jnp.dot(bu_ref[:, half_cols(h)].astype(jnp.bfloat16), cm_ref[h], preferred_element_type=jnp.float32)

    @pl.when(pl.program_id(0) == 0)
    def _():
        state_ref[...] = jnp.zeros_like(state_ref)
        u_meta = um_ref[...].astype(jnp.bfloat16)
        for h in range(SSM_HALVES):
            expand(u_meta, h, N_META)
            scan_half(h, N_META)

    u = u_ref[...]
    u_bf16 = u.astype(jnp.bfloat16)
    for h in range(SSM_HALVES):
        expand(u_bf16, h, SSM_TC)
    y_halves = []
    for h in range(SSM_HALVES):
        scan_half(h, SSM_TC)
        y_halves.append(readout(h))
    y = jnp.concatenate(y_halves, axis=1)
    y = jax.nn.gelu(y + d_ref[...] * u)
    gate = jnp.dot(y.astype(jnp.bfloat16), wglu_ref[...], preferred_element_type=jnp.float32)
    y_ref[...] = (y * jax.nn.sigmoid(gate)).astype(y_ref.dtype)


def _ssm(u_meta_rows, u_tm, bb, ab, cm, d, wglu, nb):
    rows = u_tm.shape[0]
    blk = SSM_TC * nb
    assert rows % blk == 0 and nb % 8 == 0 and N_META <= SSM_TC
    const = lambda i: (0, 0)
    return pl.pallas_call(
        functools.partial(_ssm_kernel, nb=nb),
        grid=(rows // blk,),
        in_specs=[
            pl.BlockSpec((N_META * nb, SSM_WIDTH), const),
            pl.BlockSpec((blk, SSM_WIDTH), lambda i: (i, 0)),
            pl.BlockSpec((SSM_HALVES, SSM_WIDTH // SSM_HALVES, 2 * SSM_COLS // SSM_HALVES), lambda i: (0, 0, 0)),
            pl.BlockSpec((2, 8, SSM_COLS), lambda i: (0, 0, 0)),
            pl.BlockSpec((SSM_HALVES, 2 * SSM_COLS // SSM_HALVES, SSM_WIDTH // SSM_HALVES), lambda i: (0, 0, 0)),
            pl.BlockSpec((1, SSM_WIDTH), const),
            pl.BlockSpec((SSM_WIDTH, SSM_WIDTH), const),
        ],
        out_specs=pl.BlockSpec((blk, SSM_WIDTH), lambda i: (i, 0)),
        out_shape=jax.ShapeDtypeStruct((rows, SSM_WIDTH), jnp.bfloat16),
        scratch_shapes=[
            pltpu.VMEM((nb, 2 * SSM_COLS), jnp.float32),
            pltpu.VMEM((blk, 2 * SSM_COLS), jnp.float32),
        ],
        compiler_params=pltpu.CompilerParams(
            dimension_semantics=("arbitrary",), vmem_limit_bytes=VMEM_LIMIT),
        name="ssm",
    )(u_meta_rows, u_tm, bb, ab, cm, d, wglu)


def _ssm_params(a_re, a_im, log_dt, b_re, b_im, c_re, c_im):
    f32 = jnp.float32
    lam_re = jnp.minimum(a_re.astype(f32), -1e-4)
    lam_im = a_im.astype(f32)
    dt = jnp.exp(log_dt.astype(f32))[:, None]
    mag = jnp.exp(lam_re * dt)
    ab_re = mag * jnp.cos(lam_im * dt)
    ab_im = mag * jnp.sin(lam_im * dt)
    den = lam_re * lam_re + lam_im * lam_im
    nr = ab_re - 1.0
    ni = ab_im
    z_re = (nr * lam_re + ni * lam_im) / den
    z_im = (ni * lam_re - nr * lam_im) / den
    br32 = b_re.astype(f32)
    bi32 = b_im.astype(f32)
    bb_re = z_re[..., None] * br32 - z_im[..., None] * bi32
    bb_im = z_re[..., None] * bi32 + z_im[..., None] * br32
    gh = SSM_GROUPS // SSM_HALVES
    eye = jnp.eye(gh, dtype=f32)
    split = lambda a: a.reshape((SSM_HALVES, gh) + a.shape[1:])
    exp_re = jnp.einsum('kgpc,gh->kgchp', split(bb_re), eye).reshape(SSM_HALVES, gh * SSM_GROUP, gh * SSM_STATE)
    exp_im = jnp.einsum('kgpc,gh->kgchp', split(bb_im), eye).reshape(SSM_HALVES, gh * SSM_GROUP, gh * SSM_STATE)
    bb = jnp.concatenate([exp_re, exp_im], axis=2)
    ro_re = jnp.einsum('kgcp,gh->kgphc', split(c_re.astype(f32)), eye).reshape(
        SSM_HALVES, gh * SSM_STATE, gh * SSM_GROUP)
    ro_im = jnp.einsum('kgcp,gh->kgphc', split(c_im.astype(f32)), eye).reshape(
        SSM_HALVES, gh * SSM_STATE, gh * SSM_GROUP)
    cm = jnp.concatenate([ro_re, -ro_im], axis=1)
    ab = jnp.stack([ab_re.reshape(SSM_COLS), ab_im.reshape(SSM_COLS)], axis=0)
    ab = jnp.broadcast_to(ab[:, None, :], (2, 8, SSM_COLS))
    return bb.astype(jnp.bfloat16), ab, cm.astype(jnp.bfloat16)


def _attn_kernel(lam_ref, q_ref, k_ref, v_ref, km_ref, vm_ref, g_ref, o_ref, vt_ref, *, scale, seq):
    tq = ATTN_TQ
    f32, bf16 = jnp.float32, jnp.bfloat16
    vt_ref[...] = v_ref[...].astype(f32).T.astype(bf16)
    vm_t = vm_ref[...].astype(f32).T.astype(bf16)
    k_meta = km_ref[...]
    lane = lax.broadcasted_iota(jnp.int32, (tq, ATTN_VDIM), 1)
    key = lax.broadcasted_iota(jnp.int32, (tq, 2 * tq), 0)
    qry = lax.broadcasted_iota(jnp.int32, (tq, 2 * tq), 1) % tq
    diag_visible = qry // CHUNK >= key // CHUNK
    col_max = lambda a: jnp.max(a, axis=0, keepdims=True)
    col_sum = lambda a: jnp.sum(a, axis=0, keepdims=True)

    def scores(i):
        rows = slice(i * tq, (i + 1) * tq)
        q = q_ref[rows, :] * scale
        zero = jnp.zeros_like(q)
        qq = jnp.concatenate([jnp.where(lane < ATTN_HEAD_DIM, q, zero),
                              jnp.where(lane >= ATTN_HEAD_DIM, q, zero)], axis=0)
        scores_t = lambda kb: lax.dot_general(kb, qq, (((1,), (1,)), ((), ())), preferred_element_type=f32)
        s_meta = scores_t(k_meta)
        s_diag = jnp.where(diag_visible, scores_t(k_ref[rows, :]), NEG_INF)
        s_prev = scores_t(k_ref[0:i * tq, :]) if i > 0 else None
        return s_meta, s_diag, s_prev

    def softmax(i, s):
        s_meta, s_diag, s_prev = s
        m = jnp.maximum(col_max(s_meta), col_max(s_diag))
        if i > 0:
            m = jnp.maximum(m, col_max(s_prev))
        p_meta = jnp.exp2(s_meta - m)
        p_diag = jnp.exp2(s_diag - m)
        l = col_sum(p_meta) + col_sum(p_diag)
        p_prev = None
        if i > 0:
            p_prev = jnp.exp2(s_prev - m)
            l = l + col_sum(p_prev)
            p_prev = p_prev.astype(bf16)
        return p_meta.astype(bf16), p_diag.astype(bf16), p_prev, l

    def weighted_values(i, p):
        p_meta, p_diag, p_prev, l = p
        rows = slice(i * tq, (i + 1) * tq)
        acc = (jnp.dot(vm_t, p_meta, preferred_element_type=f32)
               + jnp.dot(vt_ref[:, rows], p_diag, preferred_element_type=f32))
        if i > 0:
            acc = acc + jnp.dot(vt_ref[:, 0:i * tq], p_prev, preferred_element_type=f32)
        o_all = acc / l
        o_t = o_all[:, :tq] - lam_ref[0] * o_all[:, tq:]
        o_t = o_t * lax.rsqrt(jnp.mean(o_t * o_t, axis=0, keepdims=True) + LN_EPS)
        o_ref[rows, :] = (o_t.T * g_ref[...]).astype(o_ref.dtype)

    n_blk = seq // tq
    s_next = scores(0)
    p_last = None
    for i in range(n_blk):
        s_cur = s_next
        s_next = scores(i + 1) if i + 1 < n_blk else None
        if p_last is not None:
            weighted_values(i - 1, p_last)
        p_last = softmax(i, s_cur)
    weighted_values(n_blk - 1, p_last)


def _attention(lam, proj, proj_meta, g_scaled, nb, seq):
    assert seq % ATTN_TQ == 0
    nh = ATTN_HEADS
    first = GATE_COLS // ATTN_VDIM
    head = lambda part: pl.BlockSpec((None, seq, ATTN_VDIM), lambda b, h: (b, 0, first + part * nh + h))
    return pl.pallas_call(
        functools.partial(_attn_kernel, scale=ATTN_HEAD_DIM ** -0.5, seq=seq),
        grid=(nb, nh),
        in_specs=[
            pl.BlockSpec(memory_space=pltpu.SMEM),
            head(0), head(1), head(2),
            pl.BlockSpec((N_META, ATTN_VDIM), lambda b, h: (0, h)),
            pl.BlockSpec((N_META, ATTN_VDIM), lambda b, h: (0, nh + h)),
            pl.BlockSpec((1, ATTN_VDIM), lambda b, h: (0, 0)),
        ],
        out_specs=pl.BlockSpec((None, seq, ATTN_VDIM), lambda b, h: (b, 0, h)),
        out_shape=jax.ShapeDtypeStruct((nb, seq, ATTN_WIDTH), jnp.bfloat16),
        scratch_shapes=[pltpu.VMEM((ATTN_VDIM, seq), jnp.bfloat16)],
        compiler_params=pltpu.CompilerParams(
            dimension_semantics=("parallel", "parallel"), vmem_limit_bytes=VMEM_LIMIT),
        name="attention",
    )(lam, proj, proj, proj, proj_meta, proj_meta, g_scaled)


def _merge_kernel(x_ref, gi_ref, bi_ref, gate_ref, ys_ref, ya_ref, wbs_ref, wba_ref, wo_ref,
                  g1_ref, b1_ref, wr_ref, br_ref, h1_ref, h1p_ref, route_ref, route_t_ref, cnt_ref, logits_ref):
    i = pl.program_id(0)
    tm = x_ref.shape[0]

    @pl.when(i == 0)
    def _():
        cnt_ref[...] = jnp.zeros_like(cnt_ref)
        logits_ref[...] = jnp.zeros_like(logits_ref)

    h = _layer_norm(x_ref[...], gi_ref[...], bi_ref[...])
    ps = jnp.dot(ys_ref[...], wbs_ref[...], preferred_element_type=jnp.float32)
    pa = jnp.dot(ya_ref[...], wba_ref[...], preferred_element_type=jnp.float32)

    logits = logits_ref[...]
    lane_i = lax.broadcasted_iota(jnp.int32, (tm, LANES), 1)
    lane = lane_i.astype(jnp.float32)
    is_group = jnp.logical_and(lane_i >= N_EXPERTS, lane_i < N_EXPERTS + N_EXPERT_GROUPS)
    lane_group = (lane_i // EXPERTS_PER_GROUP).astype(jnp.float32)

    def first_argmax(vals):
        mx = jnp.max(vals, axis=-1, keepdims=True)
        idx = jnp.min(jnp.where(vals == mx, lane, float(LANES)), axis=-1, keepdims=True)
        return mx, idx

    gl = jnp.where(is_group, logits, NEG_INF)
    gmax, glane = first_argmax(gl)
    g_val = 1.0 / jnp.sum(jnp.where(is_group, jnp.exp(gl - gmax), 0.0), axis=-1, keepdims=True)
    g_idx = glane - float(N_EXPERTS)
    in_group = jnp.logical_and(lane_i < N_EXPERTS, lane_group == g_idx)
    el = jnp.where(in_group, logits, NEG_INF)
    m1, i1 = first_argmax(el)
    el2 = jnp.where(lane == i1, NEG_INF, el)
    m2, i2 = first_argmax(el2)
    e2 = jnp.exp(m2 - m1)
    w1 = g_val / (1.0 + e2)
    w2 = g_val * e2 / (1.0 + e2)

    oh = jnp.where(jnp.logical_or(lane == i1, lane == i2), 1.0, 0.0)
    r_i = lax.broadcasted_iota(jnp.int32, (tm, tm), 0)
    c_i = lax.broadcasted_iota(jnp.int32, (tm, tm), 1)
    tri = jnp.where(c_i < r_i, 1.0, 0.0).astype(jnp.bfloat16)
    count = cnt_ref[...]
    before = jnp.dot(tri, oh.astype(jnp.bfloat16), preferred_element_type=jnp.float32) + count
    rank1 = jnp.sum(jnp.where(lane == i1, before, 0.0), axis=-1, keepdims=True)
    rank2 = jnp.sum(jnp.where(lane == i2, before, 0.0), axis=-1, keepdims=True)
    cnt_ref[...] = count + jnp.where(i >= 1, jnp.sum(oh, axis=0, keepdims=True), 0.0)

    route = jnp.where(lane_i == 0, i1, 0.0)
    route = jnp.where(lane_i == 1, i2, route)
    route = jnp.where(lane_i == 2, w1, route)
    route = jnp.where(lane_i == 3, w2, route)
    route = jnp.where(lane_i == 4, rank1, route)
    route = jnp.where(lane_i == 5, rank2, route)
    route_ref[...] = route
    route_t_ref[...] = route.T[0:ROUTE_ROWS, :]

    gates = gate_ref[...].astype(jnp.float32)
    merged = gates[:, :D_MODEL] * ps + gates[:, D_MODEL:] * pa
    mix = jnp.dot(merged.astype(jnp.bfloat16), wo_ref[...], preferred_element_type=jnp.float32)
    h1 = _layer_norm(DEEPNORM_ALPHA * h + mix, g1_ref[...], b1_ref[...])
    h1_ref[...] = h1
    h1p_ref[...] = _pack_halves(h1)
    logits_ref[...] = jnp.dot(h1.astype(jnp.bfloat16), wr_ref[...], preferred_element_type=jnp.float32) + br_ref[...]


def _merge(x2, gi, bi, gates, ys, ya, wbs, wba, wo, g1, b1, wr, br):
    t_rows = x2.shape[0]
    tm = MERGE_TM
    assert t_rows % tm == 0
    n_tiles = t_rows // tm
    row = lambda i: (jnp.minimum(i, n_tiles - 1), 0)
    routed = lambda i: (jnp.maximum(i - 1, 0), 0)
    const = lambda i: (0, 0)
    return pl.pallas_call(
        _merge_kernel,
        grid=(n_tiles + 1,),
        in_specs=[
            pl.BlockSpec((tm, D_MODEL), row),
            pl.BlockSpec((1, D_MODEL), const),
            pl.BlockSpec((1, D_MODEL), const),
            pl.BlockSpec((tm, GATE_COLS), row),
            pl.BlockSpec((tm, SSM_WIDTH), row),
            pl.BlockSpec((tm, ATTN_WIDTH), row),
            pl.BlockSpec((SSM_WIDTH, D_MODEL), const, pipeline_mode=pl.Buffered(1)),
            pl.BlockSpec((ATTN_WIDTH, D_MODEL), const, pipeline_mode=pl.Buffered(1)),
            pl.BlockSpec((D_MODEL, D_MODEL), const, pipeline_mode=pl.Buffered(1)),
            pl.BlockSpec((1, D_MODEL), const),
            pl.BlockSpec((1, D_MODEL), const),
            pl.BlockSpec((D_MODEL, LANES), const),
            pl.BlockSpec((1, LANES), const),
        ],
        out_specs=[
            pl.BlockSpec((tm, D_MODEL), row),
            pl.BlockSpec((tm, D_MODEL // 2), row),
            pl.BlockSpec((tm, LANES), routed),
            pl.BlockSpec((ROUTE_ROWS, tm), lambda i: (0, jnp.maximum(i - 1, 0))),
            pl.BlockSpec((1, LANES), const),
        ],
        scratch_shapes=[pltpu.VMEM((tm, LANES), jnp.float32)],
        out_shape=[
            jax.ShapeDtypeStruct((t_rows, D_MODEL), jnp.float32),
            jax.ShapeDtypeStruct((t_rows, D_MODEL // 2), jnp.uint32),
            jax.ShapeDtypeStruct((t_rows, LANES), jnp.float32),
            jax.ShapeDtypeStruct((ROUTE_ROWS, t_rows), jnp.float32),
            jax.ShapeDtypeStruct((1, LANES), jnp.float32),
        ],
        compiler_params=pltpu.CompilerParams(
            dimension_semantics=("arbitrary",), vmem_limit_bytes=VMEM_LIMIT),
        name="merge",
    )(x2, gi, bi, gates, ys, ya, wbs, wba, wo, g1, b1, wr, br)


def _row_map_kernel(lo_ref, hi_ref, dest0_ref, dest1_ref, table_ref, *, t_rows):
    i = pl.program_id(0)
    tm = ROWMAP_TM
    stride = t_rows + DUMP_ROWS

    @pl.when(i == 0)
    def _():
        for e in range(N_EXPERTS + 1):
            def fill(r, carry):
                table_ref[r] = t_rows + (r & (DUMP_ROWS - 1))
                return carry
            lax.fori_loop(lo_ref[e], hi_ref[e], fill, 0)

    base = i * tm
    for a in range(tm):
        table_ref[dest0_ref[0, a]] = base + a
        table_ref[dest1_ref[0, a]] = base + (a + stride)


def _row_map(fill_lo, fill_hi, dest4, n_rows, t_rows):
    tm = ROWMAP_TM
    assert t_rows % tm == 0 and ROW_BLOCK & (ROW_BLOCK - 1) == 0
    grid_spec = pltpu.PrefetchScalarGridSpec(
        num_scalar_prefetch=2,
        grid=(t_rows // tm,),
        in_specs=[pl.BlockSpec((None, None, 1, tm), lambda i, lo, hi, k=k: (k, i, 0, 0), memory_space=pltpu.SMEM)
                  for k in range(TOP_K)],
        out_specs=pl.BlockSpec(memory_space=pltpu.SMEM),
    )
    return pl.pallas_call(
        functools.partial(_row_map_kernel, t_rows=t_rows),
        grid_spec=grid_spec,
        out_shape=jax.ShapeDtypeStruct((n_rows,), jnp.int32),
        compiler_params=pltpu.CompilerParams(dimension_semantics=("arbitrary",)),
        name="row_map",
    )(fill_lo, fill_hi, dest4, dest4)


def _moe_kernel(be_ref, nu_ref, src0_ref, src1_ref, src2_ref, dst0_ref, dst1_ref, h1_hbm, wg_ref, wu_ref,
                wd_ref, ysc_hbm, xbuf0, xbuf1, xbuf2, ybuf0, ybuf1, ybuf2, wgb, wub, wdb, gsem, ssem):
    i = pl.program_id(0)
    n_used = nu_ref[0]
    rb = ROW_BLOCK
    ns = MOE_SLOTS
    xbuf = (xbuf0, xbuf1, xbuf2)
    ybuf = (ybuf0, ybuf1, ybuf2)

    def start_gather(idx_ref, s):
        for r in range(rb):
            pltpu.make_async_copy(h1_hbm.at[pl.ds(idx_ref[0, r], 1)], xbuf[s].at[pl.ds(r, 1)],
                                  gsem.at[s]).start(priority=r % 2)

    def wait_gather(s):
        pltpu.make_async_copy(h1_hbm.at[pl.ds(0, rb)], xbuf[s], gsem.at[s]).wait()

    def start_scatter(idx_ref, s):
        for r in range(rb):
            pltpu.make_async_copy(ybuf[s].at[pl.ds(r, 1)], ysc_hbm.at[pl.ds(idx_ref[0, r], 1)],
                                  ssem.at[s]).start(priority=r % 2)

    def wait_scatter(s):
        pltpu.make_async_copy(ybuf[s], ysc_hbm.at[pl.ds(0, rb)], ssem.at[s]).wait()

    @pl.when(i == 0)
    def _():
        ybuf1[...] = jnp.zeros_like(ybuf1)
        ybuf2[...] = jnp.zeros_like(ybuf2)
        start_gather(src0_ref, 0)
        start_gather(src1_ref, 1)
        start_scatter(dst0_ref, 1)

    expert_changed = jnp.logical_or(i == 0, be_ref[i] != be_ref[jnp.maximum(i - 1, 0)])

    @pl.when(jnp.logical_and(i <= n_used, expert_changed))
    def _():
        wgb[...] = wg_ref[...].astype(jnp.bfloat16)
        wub[...] = wu_ref[...].astype(jnp.bfloat16)
        wdb[...] = wd_ref[...].astype(jnp.bfloat16)

    def step(slot):
        ahead = (slot + 2) % ns
        wait_gather(slot)

        @pl.when(i >= 1)
        def _():
            wait_scatter(slot)

        start_gather(src2_ref, ahead)
        start_scatter(dst1_ref, ahead)
        xb = jnp.concatenate([half.astype(jnp.bfloat16) for half in _unpack_halves(xbuf[slot][...])], axis=1)
        g = jnp.dot(xb, wgb[...], preferred_element_type=jnp.float32)
        u = jnp.dot(xb, wub[...], preferred_element_type=jnp.float32)
        hb = (jax.nn.silu(g) * u).astype(jnp.bfloat16)
        ybuf[slot][...] = _pack_halves(jnp.dot(hb, wdb[...], preferred_element_type=jnp.float32))

        @pl.when(i == n_used)
        def _():
            for s in ((slot + 1) % ns, ahead):
                wait_gather(s)
                wait_scatter(s)

    for s in range(ns):
        pl.when(jnp.logical_and(i <= n_used, i % ns == s))(functools.partial(step, s))


def _moe(block_exp, n_used, src_blocks, dst_blocks, h1, wg, wu, wd, n_blocks):
    t_rows = h1.shape[0]
    rb = ROW_BLOCK
    assert MOE_SLOTS == 3
    wsel = lambda i, be, nu: (be[i], 0, 0)
    smem_blk = lambda off: pl.BlockSpec((None, 1, rb), lambda i, be, nu: (i + off, 0, 0),
                                        memory_space=pltpu.SMEM)
    row_buf = pltpu.VMEM((rb, D_MODEL // 2), jnp.uint32)
    grid_spec = pltpu.PrefetchScalarGridSpec(
        num_scalar_prefetch=2,
        grid=(n_blocks + 1,),
        in_specs=[
            smem_blk(0), smem_blk(1), smem_blk(2), smem_blk(0), smem_blk(1),
            pl.BlockSpec(memory_space=pl.ANY),
            pl.BlockSpec((None, D_MODEL, D_FF_EXPERT), wsel),
            pl.BlockSpec((None, D_MODEL, D_FF_EXPERT), wsel),
            pl.BlockSpec((None, D_FF_EXPERT, D_MODEL), wsel),
        ],
        out_specs=pl.BlockSpec(memory_space=pl.ANY),
        scratch_shapes=[row_buf] * (2 * MOE_SLOTS) + [
            pltpu.VMEM((D_MODEL, D_FF_EXPERT), jnp.bfloat16),
            pltpu.VMEM((D_MODEL, D_FF_EXPERT), jnp.bfloat16),
            pltpu.VMEM((D_FF_EXPERT, D_MODEL), jnp.bfloat16),
            pltpu.SemaphoreType.DMA((MOE_SLOTS,)),
            pltpu.SemaphoreType.DMA((MOE_SLOTS,)),
        ],
    )
    return pl.pallas_call(
        _moe_kernel,
        grid_spec=grid_spec,
        out_shape=jax.ShapeDtypeStruct((TOP_K * t_rows + 2 * rb, D_MODEL // 2), jnp.uint32),
        compiler_params=pltpu.CompilerParams(
            dimension_semantics=("arbitrary",), vmem_limit_bytes=VMEM_LIMIT),
        name="experts",
    )(block_exp, n_used, src_blocks, src_blocks, src_blocks, dst_blocks, dst_blocks, h1, wg, wu, wd)


def _combine_kernel(h1_ref, route_ref, y0_ref, y1_ref, g_ref, b_ref, o_ref):
    route = route_ref[...]
    y0 = _unpack_halves(y0_ref[...])
    y1 = _unpack_halves(y1_ref[...])
    ffn = jnp.concatenate([route[:, 2:3] * a + route[:, 3:4] * b for a, b in zip(y0, y1)], axis=1)
    o_ref[...] = _layer_norm(DEEPNORM_ALPHA * h1_ref[...] + ffn, g_ref[...], b_ref[...])


def _combine(h1, route, ysc, g2, b2):
    t_rows = h1.shape[0]
    tm = COMBINE_TM
    assert t_rows % tm == 0 and DUMP_ROWS % tm == 0
    second = (t_rows + DUMP_ROWS) // tm
    row = lambda i: (i, 0)
    const = lambda i: (0, 0)
    return pl.pallas_call(
        _combine_kernel,
        grid=(t_rows // tm,),
        in_specs=[
            pl.BlockSpec((tm, D_MODEL), row),
            pl.BlockSpec((tm, LANES), row),
            pl.BlockSpec((tm, D_MODEL // 2), row),
            pl.BlockSpec((tm, D_MODEL // 2), lambda i: (i + second, 0)),
            pl.BlockSpec((1, D_MODEL), const),
            pl.BlockSpec((1, D_MODEL), const),
        ],
        out_specs=pl.BlockSpec((tm, D_MODEL), row),
        out_shape=jax.ShapeDtypeStruct((t_rows, D_MODEL), jnp.float32),
        compiler_params=pltpu.CompilerParams(
            dimension_semantics=("parallel",), vmem_limit_bytes=VMEM_LIMIT),
        name="combine",
    )(h1, route, ysc, ysc, g2, b2)


def kernel(x, meta_tokens, ln_in_g, ln_in_b, w_in, ssm_a_re, ssm_a_im, ssm_log_dt, ssm_b_re, ssm_b_im,
           ssm_c_re, ssm_c_im, ssm_d, ssm_w_glu, attn_lambda_q1, attn_lambda_k1, attn_lambda_q2,
           attn_lambda_k2, attn_subln_g, w_br_ssm, w_br_attn, w_o, ln1_g, ln1_b, router_g_w, router_g_b,
           router_e_w, router_e_b, exp_w_gate, exp_w_up, exp_w_down, ln2_g, ln2_b):
    f32, bf16 = jnp.float32, jnp.bfloat16
    nb, seq, d = x.shape
    assert d == D_MODEL and w_in.shape[0] == DEPTH == 1
    t_rows = nb * seq
    l = 0
    lambda_init = 0.8 - 0.6 * math.exp(-0.3 * l)
    row2 = lambda v: v.reshape(1, -1).astype(f32)

    x2 = x.reshape(t_rows, d)
    w_u = w_in[l, :, :SSM_WIDTH].astype(bf16)
    q_end = SSM_WIDTH + ATTN_WIDTH
    w_rest = jnp.concatenate([w_in[l, :, SSM_WIDTH + QKV_COLS:], w_in[l, :, SSM_WIDTH:q_end] * LOG2_E,
                              w_in[l, :, q_end:SSM_WIDTH + QKV_COLS]], axis=1).astype(bf16)
    gi, bi = row2(ln_in_g), row2(ln_in_b)
    u, proj = _ln_proj(x2, gi, bi, w_u, w_rest, min(PROJ_TM, seq), GATE_COLS)
    u_meta, proj_meta = _ln_proj(meta_tokens.astype(f32), gi, bi, w_u, w_rest, N_META, 0,
                                 first_col=GATE_COLS + ATTN_WIDTH)

    bb, ab, cm = _ssm_params(ssm_a_re[l], ssm_a_im[l], ssm_log_dt[l], ssm_b_re[l], ssm_b_im[l],
                             ssm_c_re[l], ssm_c_im[l])
    u_tm = u.reshape(nb, seq, SSM_WIDTH).transpose(1, 0, 2).reshape(seq * nb, SSM_WIDTH)
    u_meta_rows = jnp.repeat(u_meta, nb, axis=0)
    y_ssm_tm = _ssm(u_meta_rows, u_tm, bb, ab, cm, row2(ssm_d[l]), ssm_w_glu[l].astype(bf16), nb)
    y_ssm = y_ssm_tm.reshape(seq, nb, SSM_WIDTH).transpose(1, 0, 2).reshape(t_rows, SSM_WIDTH)

    lam = (jnp.exp(jnp.sum(attn_lambda_q1[l].astype(f32) * attn_lambda_k1[l].astype(f32)))
           - jnp.exp(jnp.sum(attn_lambda_q2[l].astype(f32) * attn_lambda_k2[l].astype(f32)))
           + lambda_init).reshape(1)
    g_scaled = row2(attn_subln_g[l]) * (1.0 - lambda_init)
    y_attn = _attention(lam, proj.reshape(nb, seq, GATE_COLS + QKV_COLS), proj_meta, g_scaled, nb, seq)
    y_attn = y_attn.reshape(t_rows, ATTN_WIDTH)

    w_r = jnp.concatenate([router_e_w[l].astype(f32), router_g_w[l].astype(f32)], axis=1)
    w_r = jnp.pad(w_r, ((0, 0), (0, LANES - w_r.shape[1])))
    b_r = jnp.concatenate([router_e_b[l].astype(f32), router_g_b[l].astype(f32)])
    b_r = jnp.pad(b_r, (0, LANES - b_r.shape[0])).reshape(1, LANES)
    h1, h1_packed, route, route_t, cnt = _merge(x2, gi, bi, proj, y_ssm, y_attn, w_br_ssm[l].astype(bf16),
                                       w_br_attn[l].astype(bf16), w_o[l].astype(bf16), row2(ln1_g[l]),
                                       row2(ln1_b[l]), w_r.astype(bf16), b_r)

    counts = cnt[0, :N_EXPERTS].astype(jnp.int32)
    padded = (counts + ROW_BLOCK - 1) // ROW_BLOCK * ROW_BLOCK
    pad_ends = jnp.cumsum(padded)
    pad_starts = pad_ends - padded
    expert = route_t[0:TOP_K].astype(jnp.int32)
    rank = route_t[4:4 + TOP_K].astype(jnp.int32)
    first_row = jnp.sum(jnp.where(expert[None] == jnp.arange(N_EXPERTS, dtype=jnp.int32)[:, None, None],
                                  pad_starts[:, None, None], 0), axis=0)
    dest = first_row + rank
    n_blocks = -(-(t_rows * TOP_K + N_EXPERTS * (ROW_BLOCK - 1)) // ROW_BLOCK)
    block_start = jnp.arange(n_blocks, dtype=jnp.int32) * ROW_BLOCK
    block_exp = jnp.minimum(jnp.sum(pad_ends[None, :] <= block_start[:, None], axis=1),
                            N_EXPERTS - 1).astype(jnp.int32)
    block_exp = jnp.concatenate([block_exp, block_exp[-1:]])
    n_used = (pad_ends[-1:] // ROW_BLOCK).astype(jnp.int32)

    rb = ROW_BLOCK
    stride = t_rows + DUMP_ROWS
    fill_lo = jnp.concatenate([pad_starts + counts, pad_ends[-1:]]).astype(jnp.int32)
    fill_hi = jnp.concatenate([pad_ends, jnp.full((1,), n_blocks * rb)]).astype(jnp.int32)
    row_map = _row_map(fill_lo, fill_hi, dest.reshape(TOP_K, t_rows // ROWMAP_TM, 1, ROWMAP_TM),
                       n_blocks * rb, t_rows)
    dst_blocks = jnp.concatenate([t_rows + jnp.arange(DUMP_ROWS, dtype=jnp.int32), row_map]).reshape(
        n_blocks + 2, 1, rb)
    src = jnp.minimum(row_map % stride, t_rows - 1)
    src_blocks = jnp.concatenate([src, jnp.zeros((3 * rb,), jnp.int32)]).reshape(n_blocks + 3, 1, rb)

    ysc = _moe(block_exp, n_used, src_blocks, dst_blocks, h1_packed, exp_w_gate[l], exp_w_up[l],
               exp_w_down[l], n_blocks)
    out = _combine(h1, route, ysc, row2(ln2_g[l]), row2(ln2_b[l]))
    return out.reshape(nb, seq, d).astype(x.dtype)
```

```python
import functools
import math

import jax
import jax.numpy as jnp
from jax import lax
from jax.experimental import pallas as pl
from jax.experimental.pallas import tpu as pltpu

D_MODEL = 2048
N_META = 16
CHUNK = 64
ATTN_HEADS = 8
ATTN_HEAD_DIM = 64
ATTN_VDIM = 2 * ATTN_HEAD_DIM
ATTN_WIDTH = ATTN_HEADS * ATTN_VDIM
SSM_WIDTH = D_MODEL // 4
SSM_GROUP = 16
SSM_GROUPS = SSM_WIDTH // SSM_GROUP
SSM_STATE = 64
SSM_COLS = SSM_GROUPS * SSM_STATE
N_EXPERT_GROUPS = 4
EXPERTS_PER_GROUP = 8
N_EXPERTS = N_EXPERT_GROUPS * EXPERTS_PER_GROUP
TOP_K = 2
D_FF_EXPERT = D_MODEL // 4
QKV_COLS = 3 * ATTN_WIDTH
GATE_COLS = 2 * D_MODEL
LN_EPS = 1e-5
DEPTH = 1
DEEPNORM_ALPHA = (2.0 * DEPTH) ** 0.25
LANES = 128
SUBLANES = 8
NEG_INF = -1e30
LOG2_E = math.log2(math.e)

PROJ_TN = 1024
PROJ_TM = 1024
SSM_TC = 32
SSM_CB = 512
SSM_HALVES = 2
ATTN_TQ = 256
MERGE_TM = 256
ROW_BLOCK = 256
COMBINE_TM = 512
ROWMAP_TM = 1024
MOE_SLOTS = 3
DUMP_ROWS = 2 * ROW_BLOCK
ROUTE_ROWS = 8
VMEM_LIMIT = 56 * 1024 * 1024


def _pack_halves(x):
    n = x.shape[1] // 2
    bits = lambda v: lax.bitcast_convert_type(v.astype(jnp.bfloat16).astype(jnp.float32), jnp.uint32)
    return bits(x[:, :n]) | (bits(x[:, n:]) >> 16)


def _unpack_halves(w):
    hi = lax.bitcast_convert_type(w & jnp.uint32(0xFFFF0000), jnp.float32)
    lo = lax.bitcast_convert_type(w << 16, jnp.float32)
    return hi, lo


def _layer_norm(x, g, b):
    mu = jnp.mean(x, axis=-1, keepdims=True)
    xc = x - mu
    var = jnp.mean(xc * xc, axis=-1, keepdims=True)
    return xc * lax.rsqrt(var + LN_EPS) * g + b


def _ln_proj_kernel(x_ref, g_ref, b_ref, wu_ref, w_ref, u_ref, proj_ref, xn_ref, *, n_gate_blocks):
    j = pl.program_id(1)

    @pl.when(j == 0)
    def _():
        xn = _layer_norm(x_ref[...], g_ref[...], b_ref[...]).astype(jnp.bfloat16)
        xn_ref[...] = xn
        u_ref[...] = jnp.dot(xn, wu_ref[...], preferred_element_type=jnp.float32)

    acc = jnp.dot(xn_ref[...], w_ref[...], preferred_element_type=jnp.float32)
    if n_gate_blocks:
        acc = jnp.where(j < n_gate_blocks, jax.nn.sigmoid(acc), acc)
    proj_ref[...] = acc.astype(jnp.bfloat16)


def _ln_proj(x2, g, b, w_u, w_rest, tm, gate_cols, first_col=0, cols=None):
    t_rows = x2.shape[0]
    cols = w_rest.shape[1] - first_col if cols is None else cols
    n_col = cols // PROJ_TN
    first_blk = first_col // PROJ_TN
    assert cols % PROJ_TN == 0 and gate_cols % PROJ_TN == 0 and first_col % PROJ_TN == 0 and t_rows % tm == 0
    return pl.pallas_call(
        functools.partial(_ln_proj_kernel, n_gate_blocks=gate_cols // PROJ_TN),
        grid=(t_rows // tm, n_col),
        in_specs=[
            pl.BlockSpec((tm, D_MODEL), lambda i, j: (i, 0)),
            pl.BlockSpec((1, D_MODEL), lambda i, j: (0, 0)),
            pl.BlockSpec((1, D_MODEL), lambda i, j: (0, 0)),
            pl.BlockSpec((D_MODEL, SSM_WIDTH), lambda i, j: (0, 0)),
            pl.BlockSpec((D_MODEL, PROJ_TN), lambda i, j: (0, j + first_blk)),
        ],
        out_specs=[
            pl.BlockSpec((tm, SSM_WIDTH), lambda i, j: (i, 0)),
            pl.BlockSpec((tm, PROJ_TN), lambda i, j: (i, j)),
        ],
        out_shape=[
            jax.ShapeDtypeStruct((t_rows, SSM_WIDTH), jnp.float32),
            jax.ShapeDtypeStruct((t_rows, cols), jnp.bfloat16),
        ],
        scratch_shapes=[pltpu.VMEM((tm, D_MODEL), jnp.bfloat16)],
        compiler_params=pltpu.CompilerParams(
            dimension_semantics=("parallel", "arbitrary"), vmem_limit_bytes=VMEM_LIMIT),
        name="ln_proj",
    )(x2, g, b, w_u, w_rest)


def _ssm_kernel(um_ref, u_ref, bb_ref, ab_ref, cm_ref, d_ref, wglu_ref, y_ref, state_ref, bu_ref, *, nb):
    half_u = SSM_WIDTH // SSM_HALVES
    half_c = SSM_COLS // SSM_HALVES

    half_cols = lambda h: slice(2 * half_c * h, 2 * half_c * (h + 1))
    groups = [pl.ds(SUBLANES * g, SUBLANES) for g in range(nb // SUBLANES)]

    def expand(u_bf16, h, n_steps):
        bu_ref[pl.ds(0, n_steps * nb), half_cols(h)] = jnp.dot(
            u_bf16[:, half_u * h:half_u * (h + 1)], bb_ref[h], preferred_element_type=jnp.float32)

    def scan_half(h, n_steps):
        for cb in range(half_c // SSM_CB):
            off = cb * SSM_CB
            re_cols = pl.ds(2 * half_c * h + off, SSM_CB)
            im_cols = pl.ds(2 * half_c * h + half_c + off, SSM_CB)
            a_re = ab_ref[0, :, half_c * h + off:half_c * h + off + SSM_CB]
            a_im = ab_ref[1, :, half_c * h + off:half_c * h + off + SSM_CB]
            carry = [(state_ref[grp, re_cols], state_ref[grp, im_cols]) for grp in groups]
            for t in range(n_steps):
                for g, (s_re, s_im) in enumerate(carry):
                    rows = pl.ds(t * nb + SUBLANES * g, SUBLANES)
                    carry[g] = (a_re * s_re - a_im * s_im + bu_ref[rows, re_cols],
                                a_re * s_im + a_im * s_re + bu_ref[rows, im_cols])
                    bu_ref[rows, re_cols], bu_ref[rows, im_cols] = carry[g]
            for grp, (s_re, s_im) in zip(groups, carry):
                state_ref[grp, re_cols] = s_re
                state_ref[grp, im_cols] = s_im

    def readout(h):
        return jnp.dot(bu_ref[:, half_cols(h)].astype(jnp.bfloat16), cm_ref[h], preferred_element_type=jnp.float32)

    @pl.when(pl.program_id(0) == 0)
    def _():
        state_ref[...] = jnp.zeros_like(state_ref)
        u_meta = um_ref[...].astype(jnp.bfloat16)
        for h in range(SSM_HALVES):
            expand(u_meta, h, N_META)
            scan_half(h, N_META)

    u = u_ref[...]
    u_bf16 = u.astype(jnp.bfloat16)
    for h in range(SSM_HALVES):
        expand(u_bf16, h, SSM_TC)
    y_halves = []
    for h in range(SSM_HALVES):
        scan_half(h, SSM_TC)
        y_halves.append(readout(h))
    y = jnp.concatenate(y_halves, axis=1)
    y = jax.nn.gelu(y + d_ref[...] * u)
    gate = jnp.dot(y.astype(jnp.bfloat16), wglu_ref[...], preferred_element_type=jnp.float32)
    y_ref[...] = (y * jax.nn.sigmoid(gate)).astype(y_ref.dtype)


def _ssm(u_meta_rows, u_tm, bb, ab, cm, d, wglu, nb):
    rows = u_tm.shape[0]
    blk = SSM_TC * nb
    assert rows % blk == 0 and nb % SUBLANES == 0 and N_META <= SSM_TC
    const = lambda i: (0, 0)
    return pl.pallas_call(
        functools.partial(_ssm_kernel, nb=nb),
        grid=(rows // blk,),
        in_specs=[
            pl.BlockSpec((N_META * nb, SSM_WIDTH), const),
            pl.BlockSpec((blk, SSM_WIDTH), lambda i: (i, 0)),
            pl.BlockSpec((SSM_HALVES, SSM_WIDTH // SSM_HALVES, 2 * SSM_COLS // SSM_HALVES), lambda i: (0, 0, 0)),
            pl.BlockSpec((2, SUBLANES, SSM_COLS), lambda i: (0, 0, 0)),
            pl.BlockSpec((SSM_HALVES, 2 * SSM_COLS // SSM_HALVES, SSM_WIDTH // SSM_HALVES), lambda i: (0, 0, 0)),
            pl.BlockSpec((1, SSM_WIDTH), const),
            pl.BlockSpec((SSM_WIDTH, SSM_WIDTH), const),
        ],
        out_specs=pl.BlockSpec((blk, SSM_WIDTH), lambda i: (i, 0)),
        out_shape=jax.ShapeDtypeStruct((rows, SSM_WIDTH), jnp.bfloat16),
        scratch_shapes=[
            pltpu.VMEM((nb, 2 * SSM_COLS), jnp.float32),
            pltpu.VMEM((blk, 2 * SSM_COLS), jnp.float32),
        ],
        compiler_params=pltpu.CompilerParams(
            dimension_semantics=("arbitrary",), vmem_limit_bytes=VMEM_LIMIT),
        name="ssm",
    )(u_meta_rows, u_tm, bb, ab, cm, d, wglu)


def _ssm_params(a_re, a_im, log_dt, b_re, b_im, c_re, c_im):
    f32 = jnp.float32
    lam_re = jnp.minimum(a_re.astype(f32), -1e-4)
    lam_im = a_im.astype(f32)
    dt = jnp.exp(log_dt.astype(f32))[:, None]
    mag = jnp.exp(lam_re * dt)
    ab_re = mag * jnp.cos(lam_im * dt)
    ab_im = mag * jnp.sin(lam_im * dt)
    den = lam_re * lam_re + lam_im * lam_im
    nr = ab_re - 1.0
    ni = ab_im
    z_re = (nr * lam_re + ni * lam_im) / den
    z_im = (ni * lam_re - nr * lam_im) / den
    br32 = b_re.astype(f32)
    bi32 = b_im.astype(f32)
    bb_re = z_re[..., None] * br32 - z_im[..., None] * bi32
    bb_im = z_re[..., None] * bi32 + z_im[..., None] * br32
    gh = SSM_GROUPS // SSM_HALVES
    eye = jnp.eye(gh, dtype=f32)
    split = lambda a: a.reshape((SSM_HALVES, gh) + a.shape[1:])
    exp_re = jnp.einsum('kgpc,gh->kgchp', split(bb_re), eye).reshape(SSM_HALVES, gh * SSM_GROUP, gh * SSM_STATE)
    exp_im = jnp.einsum('kgpc,gh->kgchp', split(bb_im), eye).reshape(SSM_HALVES, gh * SSM_GROUP, gh * SSM_STATE)
    bb = jnp.concatenate([exp_re, exp_im], axis=2)
    ro_re = jnp.einsum('kgcp,gh->kgphc', split(c_re.astype(f32)), eye).reshape(
        SSM_HALVES, gh * SSM_STATE, gh * SSM_GROUP)
    ro_im = jnp.einsum('kgcp,gh->kgphc', split(c_im.astype(f32)), eye).reshape(
        SSM_HALVES, gh * SSM_STATE, gh * SSM_GROUP)
    cm = jnp.concatenate([ro_re, -ro_im], axis=1)
    ab = jnp.stack([ab_re.reshape(SSM_COLS), ab_im.reshape(SSM_COLS)], axis=0)
    ab = jnp.broadcast_to(ab[:, None, :], (2, SUBLANES, SSM_COLS))
    return bb.astype(jnp.bfloat16), ab, cm.astype(jnp.bfloat16)


def _attn_kernel(lam_ref, q_ref, k_ref, v_ref, km_ref, vm_ref, g_ref, o_ref, vt_ref, *, scale, seq):
    tq = ATTN_TQ
    f32, bf16 = jnp.float32, jnp.bfloat16
    vt_ref[...] = v_ref[...].astype(f32).T.astype(bf16)
    vm_t = vm_ref[...].astype(f32).T.astype(bf16)
    k_meta = km_ref[...]
    lane = lax.broadcasted_iota(jnp.int32, (tq, ATTN_VDIM), 1)
    key = lax.broadcasted_iota(jnp.int32, (tq, 2 * tq), 0)
    qry = lax.broadcasted_iota(jnp.int32, (tq, 2 * tq), 1) % tq
    diag_visible = qry // CHUNK >= key // CHUNK
    col_max = lambda a: jnp.max(a, axis=0, keepdims=True)
    col_sum = lambda a: jnp.sum(a, axis=0, keepdims=True)

    def scores(i):
        rows = slice(i * tq, (i + 1) * tq)
        q = q_ref[rows, :] * scale
        zero = jnp.zeros_like(q)
        qq = jnp.concatenate([jnp.where(lane < ATTN_HEAD_DIM, q, zero),
                              jnp.where(lane >= ATTN_HEAD_DIM, q, zero)], axis=0)
        scores_t = lambda kb: lax.dot_general(kb, qq, (((1,), (1,)), ((), ())), preferred_element_type=f32)
        s_meta = scores_t(k_meta)
        s_diag = jnp.where(diag_visible, scores_t(k_ref[rows, :]), NEG_INF)
        s_prev = scores_t(k_ref[0:i * tq, :]) if i > 0 else None
        return s_meta, s_diag, s_prev

    def softmax(i, s):
        s_meta, s_diag, s_prev = s
        m = jnp.maximum(col_max(s_meta), col_max(s_diag))
        if i > 0:
            m = jnp.maximum(m, col_max(s_prev))
        p_meta = jnp.exp2(s_meta - m)
        p_diag = jnp.exp2(s_diag - m)
        l = col_sum(p_meta) + col_sum(p_diag)
        p_prev = None
        if i > 0:
            p_prev = jnp.exp2(s_prev - m)
            l = l + col_sum(p_prev)
            p_prev = p_prev.astype(bf16)
        return p_meta.astype(bf16), p_diag.astype(bf16), p_prev, l

    def weighted_values(i, p):
        p_meta, p_diag, p_prev, l = p
        rows = slice(i * tq, (i + 1) * tq)
        acc = (jnp.dot(vm_t, p_meta, preferred_element_type=f32)
               + jnp.dot(vt_ref[:, rows], p_diag, preferred_element_type=f32))
        if i > 0:
            acc = acc + jnp.dot(vt_ref[:, 0:i * tq], p_prev, preferred_element_type=f32)
        o_all = acc / l
        o_t = o_all[:, :tq] - lam_ref[0] * o_all[:, tq:]
        o_t = o_t * lax.rsqrt(jnp.mean(o_t * o_t, axis=0, keepdims=True) + LN_EPS)
        o_ref[rows, :] = (o_t.T * g_ref[...]).astype(o_ref.dtype)

    n_blk = seq // tq
    s_next = scores(0)
    p_last = None
    for i in range(n_blk):
        s_cur = s_next
        s_next = scores(i + 1) if i + 1 < n_blk else None
        if p_last is not None:
            weighted_values(i - 1, p_last)
        p_last = softmax(i, s_cur)
    weighted_values(n_blk - 1, p_last)


def _attention(lam, proj, proj_meta, g_scaled, nb, seq):
    assert seq % ATTN_TQ == 0
    nh = ATTN_HEADS
    first = GATE_COLS // ATTN_VDIM
    head = lambda part: pl.BlockSpec((None, seq, ATTN_VDIM), lambda b, h: (b, 0, first + part * nh + h))
    return pl.pallas_call(
        functools.partial(_attn_kernel, scale=ATTN_HEAD_DIM ** -0.5, seq=seq),
        grid=(nb, nh),
        in_specs=[
            pl.BlockSpec(memory_space=pltpu.SMEM),
            head(0), head(1), head(2),
            pl.BlockSpec((N_META, ATTN_VDIM), lambda b, h: (0, h)),
            pl.BlockSpec((N_META, ATTN_VDIM), lambda b, h: (0, nh + h)),
            pl.BlockSpec((1, ATTN_VDIM), lambda b, h: (0, 0)),
        ],
        out_specs=pl.BlockSpec((None, seq, ATTN_VDIM), lambda b, h: (b, 0, h)),
        out_shape=jax.ShapeDtypeStruct((nb, seq, ATTN_WIDTH), jnp.bfloat16),
        scratch_shapes=[pltpu.VMEM((ATTN_VDIM, seq), jnp.bfloat16)],
        compiler_params=pltpu.CompilerParams(
            dimension_semantics=("parallel", "parallel"), vmem_limit_bytes=VMEM_LIMIT),
        name="attention",
    )(lam, proj, proj, proj, proj_meta, proj_meta, g_scaled)


def _merge_kernel(x_ref, gi_ref, bi_ref, gate_ref, ys_ref, ya_ref, wbs_ref, wba_ref, wo_ref,
                  g1_ref, b1_ref, wr_ref, br_ref, h1_ref, h1p_ref, route_ref, route_t_ref, cnt_ref, logits_ref):
    i = pl.program_id(0)
    tm = x_ref.shape[0]

    @pl.when(i == 0)
    def _():
        cnt_ref[...] = jnp.zeros_like(cnt_ref)
        logits_ref[...] = jnp.zeros_like(logits_ref)

    h = _layer_norm(x_ref[...], gi_ref[...], bi_ref[...])
    ps = jnp.dot(ys_ref[...], wbs_ref[...], preferred_element_type=jnp.float32)
    pa = jnp.dot(ya_ref[...], wba_ref[...], preferred_element_type=jnp.float32)

    logits = logits_ref[...]
    lane_i = lax.broadcasted_iota(jnp.int32, (tm, LANES), 1)
    lane = lane_i.astype(jnp.float32)
    is_group = jnp.logical_and(lane_i >= N_EXPERTS, lane_i < N_EXPERTS + N_EXPERT_GROUPS)
    lane_group = (lane_i // EXPERTS_PER_GROUP).astype(jnp.float32)

    def first_argmax(vals):
        mx = jnp.max(vals, axis=-1, keepdims=True)
        idx = jnp.min(jnp.where(vals == mx, lane, float(LANES)), axis=-1, keepdims=True)
        return mx, idx

    gl = jnp.where(is_group, logits, NEG_INF)
    gmax, glane = first_argmax(gl)
    g_val = 1.0 / jnp.sum(jnp.where(is_group, jnp.exp(gl - gmax), 0.0), axis=-1, keepdims=True)
    g_idx = glane - float(N_EXPERTS)
    in_group = jnp.logical_and(lane_i < N_EXPERTS, lane_group == g_idx)
    el = jnp.where(in_group, logits, NEG_INF)
    m1, i1 = first_argmax(el)
    el2 = jnp.where(lane == i1, NEG_INF, el)
    m2, i2 = first_argmax(el2)
    e2 = jnp.exp(m2 - m1)
    w1 = g_val / (1.0 + e2)
    w2 = g_val * e2 / (1.0 + e2)

    oh = jnp.where(jnp.logical_or(lane == i1, lane == i2), 1.0, 0.0)
    r_i = lax.broadcasted_iota(jnp.int32, (tm, tm), 0)
    c_i = lax.broadcasted_iota(jnp.int32, (tm, tm), 1)
    tri = jnp.where(c_i < r_i, 1.0, 0.0).astype(jnp.bfloat16)
    count = cnt_ref[...]
    before = jnp.dot(tri, oh.astype(jnp.bfloat16), preferred_element_type=jnp.float32) + count
    rank1 = jnp.sum(jnp.where(lane == i1, before, 0.0), axis=-1, keepdims=True)
    rank2 = jnp.sum(jnp.where(lane == i2, before, 0.0), axis=-1, keepdims=True)
    cnt_ref[...] = count + jnp.where(i >= 1, jnp.sum(oh, axis=0, keepdims=True), 0.0)

    route = jnp.where(lane_i == 0, i1, 0.0)
    route = jnp.where(lane_i == 1, i2, route)
    route = jnp.where(lane_i == 2, w1, route)
    route = jnp.where(lane_i == 3, w2, route)
    route = jnp.where(lane_i == 4, rank1, route)
    route = jnp.where(lane_i == 5, rank2, route)
    route_ref[...] = route
    route_t_ref[...] = route.T[0:ROUTE_ROWS, :]

    gates = gate_ref[...].astype(jnp.float32)
    merged = gates[:, :D_MODEL] * ps + gates[:, D_MODEL:] * pa
    mix = jnp.dot(merged.astype(jnp.bfloat16), wo_ref[...], preferred_element_type=jnp.float32)
    h1 = _layer_norm(DEEPNORM_ALPHA * h + mix, g1_ref[...], b1_ref[...])
    h1_ref[...] = h1
    h1p_ref[...] = _pack_halves(h1)
    logits_ref[...] = jnp.dot(h1.astype(jnp.bfloat16), wr_ref[...], preferred_element_type=jnp.float32) + br_ref[...]


def _merge(x2, gi, bi, gates, ys, ya, wbs, wba, wo, g1, b1, wr, br):
    t_rows = x2.shape[0]
    tm = MERGE_TM
    assert t_rows % tm == 0
    n_tiles = t_rows // tm
    row = lambda i: (jnp.minimum(i, n_tiles - 1), 0)
    routed = lambda i: (jnp.maximum(i - 1, 0), 0)
    const = lambda i: (0, 0)
    return pl.pallas_call(
        _merge_kernel,
        grid=(n_tiles + 1,),
        in_specs=[
            pl.BlockSpec((tm, D_MODEL), row),
            pl.BlockSpec((1, D_MODEL), const),
            pl.BlockSpec((1, D_MODEL), const),
            pl.BlockSpec((tm, GATE_COLS), row),
            pl.BlockSpec((tm, SSM_WIDTH), row),
            pl.BlockSpec((tm, ATTN_WIDTH), row),
            pl.BlockSpec((SSM_WIDTH, D_MODEL), const, pipeline_mode=pl.Buffered(1)),
            pl.BlockSpec((ATTN_WIDTH, D_MODEL), const, pipeline_mode=pl.Buffered(1)),
            pl.BlockSpec((D_MODEL, D_MODEL), const, pipeline_mode=pl.Buffered(1)),
            pl.BlockSpec((1, D_MODEL), const),
            pl.BlockSpec((1, D_MODEL), const),
            pl.BlockSpec((D_MODEL, LANES), const),
            pl.BlockSpec((1, LANES), const),
        ],
        out_specs=[
            pl.BlockSpec((tm, D_MODEL), row),
            pl.BlockSpec((tm, D_MODEL // 2), row),
            pl.BlockSpec((tm, LANES), routed),
            pl.BlockSpec((ROUTE_ROWS, tm), lambda i: (0, jnp.maximum(i - 1, 0))),
            pl.BlockSpec((1, LANES), const),
        ],
        scratch_shapes=[pltpu.VMEM((tm, LANES), jnp.float32)],
        out_shape=[
            jax.ShapeDtypeStruct((t_rows, D_MODEL), jnp.float32),
            jax.ShapeDtypeStruct((t_rows, D_MODEL // 2), jnp.uint32),
            jax.ShapeDtypeStruct((t_rows, LANES), jnp.float32),
            jax.ShapeDtypeStruct((ROUTE_ROWS, t_rows), jnp.float32),
            jax.ShapeDtypeStruct((1, LANES), jnp.float32),
        ],
        compiler_params=pltpu.CompilerParams(
            dimension_semantics=("arbitrary",), vmem_limit_bytes=VMEM_LIMIT),
        name="merge",
    )(x2, gi, bi, gates, ys, ya, wbs, wba, wo, g1, b1, wr, br)


def _row_map_kernel(lo_ref, hi_ref, dest0_ref, dest1_ref, table_ref, *, t_rows):
    i = pl.program_id(0)
    tm = ROWMAP_TM
    stride = t_rows + DUMP_ROWS

    @pl.when(i == 0)
    def _():
        for e in range(N_EXPERTS + 1):
            def fill(r, carry):
                table_ref[r] = t_rows + (r & (DUMP_ROWS - 1))
                return carry
            lax.fori_loop(lo_ref[e], hi_ref[e], fill, 0)

    base = i * tm
    for a in range(tm):
        table_ref[dest0_ref[0, a]] = base + a
        table_ref[dest1_ref[0, a]] = base + (a + stride)


def _row_map(fill_lo, fill_hi, dest4, n_rows, t_rows):
    tm = ROWMAP_TM
    assert t_rows % tm == 0 and ROW_BLOCK & (ROW_BLOCK - 1) == 0
    grid_spec = pltpu.PrefetchScalarGridSpec(
        num_scalar_prefetch=2,
        grid=(t_rows // tm,),
        in_specs=[pl.BlockSpec((None, None, 1, tm), lambda i, lo, hi, k=k: (k, i, 0, 0), memory_space=pltpu.SMEM)
                  for k in range(TOP_K)],
        out_specs=pl.BlockSpec(memory_space=pltpu.SMEM),
    )
    return pl.pallas_call(
        functools.partial(_row_map_kernel, t_rows=t_rows),
        grid_spec=grid_spec,
        out_shape=jax.ShapeDtypeStruct((n_rows,), jnp.int32),
        compiler_params=pltpu.CompilerParams(dimension_semantics=("arbitrary",)),
        name="row_map",
    )(fill_lo, fill_hi, dest4, dest4)


def _moe_kernel(be_ref, nu_ref, src0_ref, src1_ref, src2_ref, dst0_ref, dst1_ref, h1_hbm, wg_ref, wu_ref,
                wd_ref, ysc_hbm, xbuf0, xbuf1, xbuf2, ybuf0, ybuf1, ybuf2, wgb, wub, wdb, gsem, ssem):
    i = pl.program_id(0)
    n_used = nu_ref[0]
    rb = ROW_BLOCK
    ns = MOE_SLOTS
    xbuf = (xbuf0, xbuf1, xbuf2)
    ybuf = (ybuf0, ybuf1, ybuf2)

    def start_gather(idx_ref, s):
        for r in range(rb):
            pltpu.make_async_copy(h1_hbm.at[pl.ds(idx_ref[0, r], 1)], xbuf[s].at[pl.ds(r, 1)],
                                  gsem.at[s]).start(priority=r % 2)

    def wait_gather(s):
        pltpu.make_async_copy(h1_hbm.at[pl.ds(0, rb)], xbuf[s], gsem.at[s]).wait()

    def start_scatter(idx_ref, s):
        for r in range(rb):
            pltpu.make_async_copy(ybuf[s].at[pl.ds(r, 1)], ysc_hbm.at[pl.ds(idx_ref[0, r], 1)],
                                  ssem.at[s]).start(priority=r % 2)

    def wait_scatter(s):
        pltpu.make_async_copy(ybuf[s], ysc_hbm.at[pl.ds(0, rb)], ssem.at[s]).wait()

    @pl.when(i == 0)
    def _():
        ybuf1[...] = jnp.zeros_like(ybuf1)
        ybuf2[...] = jnp.zeros_like(ybuf2)
        start_gather(src0_ref, 0)
        start_gather(src1_ref, 1)
        start_scatter(dst0_ref, 1)

    expert_changed = jnp.logical_or(i == 0, be_ref[i] != be_ref[jnp.maximum(i - 1, 0)])

    @pl.when(jnp.logical_and(i <= n_used, expert_changed))
    def _():
        wgb[...] = wg_ref[...].astype(jnp.bfloat16)
        wub[...] = wu_ref[...].astype(jnp.bfloat16)
        wdb[...] = wd_ref[...].astype(jnp.bfloat16)

    def step(slot):
        ahead = (slot + 2) % ns
        wait_gather(slot)

        @pl.when(i >= 1)
        def _():
            wait_scatter(slot)

        start_gather(src2_ref, ahead)
        start_scatter(dst1_ref, ahead)
        xb = jnp.concatenate([half.astype(jnp.bfloat16) for half in _unpack_halves(xbuf[slot][...])], axis=1)
        g = jnp.dot(xb, wgb[...], preferred_element_type=jnp.float32)
        u = jnp.dot(xb, wub[...], preferred_element_type=jnp.float32)
        hb = (jax.nn.silu(g) * u).astype(jnp.bfloat16)
        ybuf[slot][...] = _pack_halves(jnp.dot(hb, wdb[...], preferred_element_type=jnp.float32))

        @pl.when(i == n_used)
        def _():
            for s in ((slot + 1) % ns, ahead):
                wait_gather(s)
                wait_scatter(s)

    for s in range(ns):
        pl.when(jnp.logical_and(i <= n_used, i % ns == s))(functools.partial(step, s))


def _moe(block_exp, n_used, src_blocks, dst_blocks, h1, wg, wu, wd, n_blocks):
    t_rows = h1.shape[0]
    rb = ROW_BLOCK
    assert MOE_SLOTS == 3
    wsel = lambda i, be, nu: (be[i], 0, 0)
    smem_blk = lambda off: pl.BlockSpec((None, 1, rb), lambda i, be, nu: (i + off, 0, 0),
                                        memory_space=pltpu.SMEM)
    row_buf = pltpu.VMEM((rb, D_MODEL // 2), jnp.uint32)
    grid_spec = pltpu.PrefetchScalarGridSpec(
        num_scalar_prefetch=2,
        grid=(n_blocks + 1,),
        in_specs=[
            smem_blk(0), smem_blk(1), smem_blk(2), smem_blk(0), smem_blk(1),
            pl.BlockSpec(memory_space=pl.ANY),
            pl.BlockSpec((None, D_MODEL, D_FF_EXPERT), wsel),
            pl.BlockSpec((None, D_MODEL, D_FF_EXPERT), wsel),
            pl.BlockSpec((None, D_FF_EXPERT, D_MODEL), wsel),
        ],
        out_specs=pl.BlockSpec(memory_space=pl.ANY),
        scratch_shapes=[row_buf] * (2 * MOE_SLOTS) + [
            pltpu.VMEM((D_MODEL, D_FF_EXPERT), jnp.bfloat16),
            pltpu.VMEM((D_MODEL, D_FF_EXPERT), jnp.bfloat16),
            pltpu.VMEM((D_FF_EXPERT, D_MODEL), jnp.bfloat16),
            pltpu.SemaphoreType.DMA((MOE_SLOTS,)),
            pltpu.SemaphoreType.DMA((MOE_SLOTS,)),
        ],
    )
    return pl.pallas_call(
        _moe_kernel,
        grid_spec=grid_spec,
        out_shape=jax.ShapeDtypeStruct((TOP_K * t_rows + 2 * rb, D_MODEL // 2), jnp.uint32),
        compiler_params=pltpu.CompilerParams(
            dimension_semantics=("arbitrary",), vmem_limit_bytes=VMEM_LIMIT),
        name="experts",
    )(block_exp, n_used, src_blocks, src_blocks, src_blocks, dst_blocks, dst_blocks, h1, wg, wu, wd)


def _combine_kernel(h1_ref, route_ref, y0_ref, y1_ref, g_ref, b_ref, o_ref):
    route = route_ref[...]
    y0 = _unpack_halves(y0_ref[...])
    y1 = _unpack_halves(y1_ref[...])
    ffn = jnp.concatenate([route[:, 2:3] * a + route[:, 3:4] * b for a, b in zip(y0, y1)], axis=1)
    o_ref[...] = _layer_norm(DEEPNORM_ALPHA * h1_ref[...] + ffn, g_ref[...], b_ref[...])


def _combine(h1, route, ysc, g2, b2):
    t_rows = h1.shape[0]
    tm = COMBINE_TM
    assert t_rows % tm == 0 and DUMP_ROWS % tm == 0
    second = (t_rows + DUMP_ROWS) // tm
    row = lambda i: (i, 0)
    const = lambda i: (0, 0)
    return pl.pallas_call(
        _combine_kernel,
        grid=(t_rows // tm,),
        in_specs=[
            pl.BlockSpec((tm, D_MODEL), row),
            pl.BlockSpec((tm, LANES), row),
            pl.BlockSpec((tm, D_MODEL // 2), row),
            pl.BlockSpec((tm, D_MODEL // 2), lambda i: (i + second, 0)),
            pl.BlockSpec((1, D_MODEL), const),
            pl.BlockSpec((1, D_MODEL), const),
        ],
        out_specs=pl.BlockSpec((tm, D_MODEL), row),
        out_shape=jax.ShapeDtypeStruct((t_rows, D_MODEL), jnp.float32),
        compiler_params=pltpu.CompilerParams(
            dimension_semantics=("parallel",), vmem_limit_bytes=VMEM_LIMIT),
        name="combine",
    )(h1, route, ysc, ysc, g2, b2)


def kernel(x, meta_tokens, ln_in_g, ln_in_b, w_in, ssm_a_re, ssm_a_im, ssm_log_dt, ssm_b_re, ssm_b_im,
           ssm_c_re, ssm_c_im, ssm_d, ssm_w_glu, attn_lambda_q1, attn_lambda_k1, attn_lambda_q2,
           attn_lambda_k2, attn_subln_g, w_br_ssm, w_br_attn, w_o, ln1_g, ln1_b, router_g_w, router_g_b,
           router_e_w, router_e_b, exp_w_gate, exp_w_up, exp_w_down, ln2_g, ln2_b):
    f32, bf16 = jnp.float32, jnp.bfloat16
    nb, seq, d = x.shape
    assert d == D_MODEL and w_in.shape[0] == DEPTH == 1
    t_rows = nb * seq
    l = 0
    lambda_init = 0.8 - 0.6 * math.exp(-0.3 * l)
    row2 = lambda v: v.reshape(1, -1).astype(f32)

    x2 = x.reshape(t_rows, d)
    w_u = w_in[l, :, :SSM_WIDTH].astype(bf16)
    q_end = SSM_WIDTH + ATTN_WIDTH
    w_rest = jnp.concatenate([w_in[l, :, SSM_WIDTH + QKV_COLS:], w_in[l, :, SSM_WIDTH:q_end] * LOG2_E,
                              w_in[l, :, q_end:SSM_WIDTH + QKV_COLS]], axis=1).astype(bf16)
    gi, bi = row2(ln_in_g), row2(ln_in_b)
    u, proj = _ln_proj(x2, gi, bi, w_u, w_rest, min(PROJ_TM, seq), GATE_COLS)
    u_meta, proj_meta = _ln_proj(meta_tokens.astype(f32), gi, bi, w_u, w_rest, N_META, 0,
                                 first_col=GATE_COLS + ATTN_WIDTH)

    bb, ab, cm = _ssm_params(ssm_a_re[l], ssm_a_im[l], ssm_log_dt[l], ssm_b_re[l], ssm_b_im[l],
                             ssm_c_re[l], ssm_c_im[l])
    u_tm = u.reshape(nb, seq, SSM_WIDTH).transpose(1, 0, 2).reshape(seq * nb, SSM_WIDTH)
    u_meta_rows = jnp.repeat(u_meta, nb, axis=0)
    y_ssm_tm = _ssm(u_meta_rows, u_tm, bb, ab, cm, row2(ssm_d[l]), ssm_w_glu[l].astype(bf16), nb)
    y_ssm = y_ssm_tm.reshape(seq, nb, SSM_WIDTH).transpose(1, 0, 2).reshape(t_rows, SSM_WIDTH)

    lam = (jnp.exp(jnp.sum(attn_lambda_q1[l].astype(f32) * attn_lambda_k1[l].astype(f32)))
           - jnp.exp(jnp.sum(attn_lambda_q2[l].astype(f32) * attn_lambda_k2[l].astype(f32)))
           + lambda_init).reshape(1)
    g_scaled = row2(attn_subln_g[l]) * (1.0 - lambda_init)
    y_attn = _attention(lam, proj.reshape(nb, seq, GATE_COLS + QKV_COLS), proj_meta, g_scaled, nb, seq)
    y_attn = y_attn.reshape(t_rows, ATTN_WIDTH)

    w_r = jnp.concatenate([router_e_w[l].astype(f32), router_g_w[l].astype(f32)], axis=1)
    w_r = jnp.pad(w_r, ((0, 0), (0, LANES - w_r.shape[1])))
    b_r = jnp.concatenate([router_e_b[l].astype(f32), router_g_b[l].astype(f32)])
    b_r = jnp.pad(b_r, (0, LANES - b_r.shape[0])).reshape(1, LANES)
    h1, h1_packed, route, route_t, cnt = _merge(x2, gi, bi, proj, y_ssm, y_attn, w_br_ssm[l].astype(bf16),
                                       w_br_attn[l].astype(bf16), w_o[l].astype(bf16), row2(ln1_g[l]),
                                       row2(ln1_b[l]), w_r.astype(bf16), b_r)

    counts = cnt[0, :N_EXPERTS].astype(jnp.int32)
    padded = (counts + ROW_BLOCK - 1) // ROW_BLOCK * ROW_BLOCK
    pad_ends = jnp.cumsum(padded)
    pad_starts = pad_ends - padded
    expert = route_t[0:TOP_K].astype(jnp.int32)
    rank = route_t[4:4 + TOP_K].astype(jnp.int32)
    first_row = jnp.sum(jnp.where(expert[None] == jnp.arange(N_EXPERTS, dtype=jnp.int32)[:, None, None],
                                  pad_starts[:, None, None], 0), axis=0)
    dest = first_row + rank
    n_blocks = -(-(t_rows * TOP_K + N_EXPERTS * (ROW_BLOCK - 1)) // ROW_BLOCK)
    block_start = jnp.arange(n_blocks, dtype=jnp.int32) * ROW_BLOCK
    block_exp = jnp.minimum(jnp.sum(pad_ends[None, :] <= block_start[:, None], axis=1),
                            N_EXPERTS - 1).astype(jnp.int32)
    block_exp = jnp.concatenate([block_exp, block_exp[-1:]])
    n_used = (pad_ends[-1:] // ROW_BLOCK).astype(jnp.int32)

    rb = ROW_BLOCK
    stride = t_rows + DUMP_ROWS
    fill_lo = jnp.concatenate([pad_starts + counts, pad_ends[-1:]]).astype(jnp.int32)
    fill_hi = jnp.concatenate([pad_ends, jnp.full((1,), n_blocks * rb)]).astype(jnp.int32)
    row_map = _row_map(fill_lo, fill_hi, dest.reshape(TOP_K, t_rows // ROWMAP_TM, 1, ROWMAP_TM),
                       n_blocks * rb, t_rows)
    dst_blocks = jnp.concatenate([t_rows + jnp.arange(DUMP_ROWS, dtype=jnp.int32), row_map]).reshape(
        n_blocks + 2, 1, rb)
    src = jnp.minimum(row_map % stride, t_rows - 1)
    src_blocks = jnp.concatenate([src, jnp.zeros((3 * rb,), jnp.int32)]).reshape(n_blocks + 3, 1, rb)

    ysc = _moe(block_exp, n_used, src_blocks, dst_blocks, h1_packed, exp_w_gate[l], exp_w_up[l],
               exp_w_down[l], n_blocks)
    out = _combine(h1, route, ysc, row2(ln2_g[l]), row2(ln2_b[l]))
    return out.reshape(nb, seq, d).astype(x.dtype)
```

```python
import functools
import math

import jax
import jax.numpy as jnp
from jax import lax
from jax.experimental import pallas as pl
from jax.experimental.pallas import tpu as pltpu

D_MODEL = 2048
N_META = 16
CHUNK = 64
ATTN_HEADS = 8
ATTN_HEAD_DIM = 64
ATTN_VDIM = 2 * ATTN_HEAD_DIM
ATTN_WIDTH = ATTN_HEADS * ATTN_VDIM
SSM_WIDTH = D_MODEL // 4
SSM_GROUP = 16
SSM_GROUPS = SSM_WIDTH // SSM_GROUP
SSM_STATE = 64
SSM_COLS = SSM_GROUPS * SSM_STATE
N_EXPERT_GROUPS = 4
EXPERTS_PER_GROUP = 8
N_EXPERTS = N_EXPERT_GROUPS * EXPERTS_PER_GROUP
TOP_K = 2
D_FF_EXPERT = D_MODEL // 4
QKV_COLS = 3 * ATTN_WIDTH
GATE_COLS = 2 * D_MODEL
LN_EPS = 1e-5
DEPTH = 1
DEEPNORM_ALPHA = (2.0 * DEPTH) ** 0.25
LANES = 128
SUBLANES = 8
NEG_INF = -1e30
LOG2_E = math.log2(math.e)

PROJ_TN = 1024
PROJ_TM = 1024
SSM_TC = 32
SSM_CB = 512
SSM_HALVES = 2
ATTN_TQ = 256
MERGE_TM = 256
ROW_BLOCK = 256
COMBINE_TM = 512
ROWMAP_TM = 1024
MOE_SLOTS = 3
DUMP_ROWS = 2 * ROW_BLOCK
ROUTE_ROWS = 8
VMEM_LIMIT = 56 * 1024 * 1024


def _pack_halves(x):
    n = x.shape[1] // 2
    bits = lambda v: lax.bitcast_convert_type(v.astype(jnp.bfloat16).astype(jnp.float32), jnp.uint32)
    return bits(x[:, :n]) | (bits(x[:, n:]) >> 16)


def _unpack_halves(w):
    hi = lax.bitcast_convert_type(w & jnp.uint32(0xFFFF0000), jnp.float32)
    lo = lax.bitcast_convert_type(w << 16, jnp.float32)
    return hi, lo


def _layer_norm(x, g, b):
    mu = jnp.mean(x, axis=-1, keepdims=True)
    xc = x - mu
    var = jnp.mean(xc * xc, axis=-1, keepdims=True)
    return xc * lax.rsqrt(var + LN_EPS) * g + b


def _ln_proj_kernel(x_ref, g_ref, b_ref, wu_ref, w_ref, u_ref, proj_ref, xn_ref, *, n_gate_blocks):
    j = pl.program_id(1)

    @pl.when(j == 0)
    def _():
        xn = _layer_norm(x_ref[...], g_ref[...], b_ref[...]).astype(jnp.bfloat16)
        xn_ref[...] = xn
        u_ref[...] = jnp.dot(xn, wu_ref[...], preferred_element_type=jnp.float32)

    acc = jnp.dot(xn_ref[...], w_ref[...], preferred_element_type=jnp.float32)
    if n_gate_blocks:
        acc = jnp.where(j < n_gate_blocks, jax.nn.sigmoid(acc), acc)
    proj_ref[...] = acc.astype(jnp.bfloat16)


def _ln_proj(x2, g, b, w_u, w_rest, tm, gate_cols, first_col=0, cols=None):
    t_rows = x2.shape[0]
    cols = w_rest.shape[1] - first_col if cols is None else cols
    n_col = cols // PROJ_TN
    first_blk = first_col // PROJ_TN
    assert cols % PROJ_TN == 0 and gate_cols % PROJ_TN == 0 and first_col % PROJ_TN == 0 and t_rows % tm == 0
    return pl.pallas_call(
        functools.partial(_ln_proj_kernel, n_gate_blocks=gate_cols // PROJ_TN),
        grid=(t_rows // tm, n_col),
        in_specs=[
            pl.BlockSpec((tm, D_MODEL), lambda i, j: (i, 0)),
            pl.BlockSpec((1, D_MODEL), lambda i, j: (0, 0)),
            pl.BlockSpec((1, D_MODEL), lambda i, j: (0, 0)),
            pl.BlockSpec((D_MODEL, SSM_WIDTH), lambda i, j: (0, 0)),
            pl.BlockSpec((D_MODEL, PROJ_TN), lambda i, j: (0, j + first_blk)),
        ],
        out_specs=[
            pl.BlockSpec((tm, SSM_WIDTH), lambda i, j: (i, 0)),
            pl.BlockSpec((tm, PROJ_TN), lambda i, j: (i, j)),
        ],
        out_shape=[
            jax.ShapeDtypeStruct((t_rows, SSM_WIDTH), jnp.float32),
            jax.ShapeDtypeStruct((t_rows, cols), jnp.bfloat16),
        ],
        scratch_shapes=[pltpu.VMEM((tm, D_MODEL), jnp.bfloat16)],
        compiler_params=pltpu.CompilerParams(
            dimension_semantics=("parallel", "arbitrary"), vmem_limit_bytes=VMEM_LIMIT),
        name="ln_proj",
    )(x2, g, b, w_u, w_rest)


def _ssm_kernel(um_ref, u_ref, bb_ref, ab_ref, cm_ref, d_ref, wglu_ref, y_ref, state_ref, bu_ref, *, nb):
    half_u = SSM_WIDTH // SSM_HALVES
    half_c = SSM_COLS // SSM_HALVES

    half_cols = lambda h: slice(2 * half_c * h, 2 * half_c * (h + 1))
    groups = [pl.ds(SUBLANES * g, SUBLANES) for g in range(nb // SUBLANES)]

    def expand(u_bf16, h, n_steps):
        bu_ref[pl.ds(0, n_steps * nb), half_cols(h)] = jnp.dot(
            u_bf16[:, half_u * h:half_u * (h + 1)], bb_ref[h], preferred_element_type=jnp.float32)

    def scan_half(h, n_steps):
        for cb in range(half_c // SSM_CB):
            off = cb * SSM_CB
            re_cols = pl.ds(2 * half_c * h + off, SSM_CB)
            im_cols = pl.ds(2 * half_c * h + half_c + off, SSM_CB)
            a_re = ab_ref[0, :, half_c * h + off:half_c * h + off + SSM_CB]
            a_im = ab_ref[1, :, half_c * h + off:half_c * h + off + SSM_CB]
            carry = [(state_ref[grp, re_cols], state_ref[grp, im_cols]) for grp in groups]
            for t in range(n_steps):
                for g, (s_re, s_im) in enumerate(carry):
                    rows = pl.ds(t * nb + SUBLANES * g, SUBLANES)
                    carry[g] = (a_re * s_re - a_im * s_im + bu_ref[rows, re_cols],
                                a_re * s_im + a_im * s_re + bu_ref[rows, im_cols])
                    bu_ref[rows, re_cols], bu_ref[rows, im_cols] = carry[g]
            for grp, (s_re, s_im) in zip(groups, carry):
                state_ref[grp, re_cols] = s_re
                state_ref[grp, im_cols] = s_im

    def readout(h):
        return jnp.dot(bu_ref[:, half_cols(h)].astype(jnp.bfloat16), cm_ref[h], preferred_element_type=jnp.float32)

    @pl.when(pl.program_id(0) == 0)
    def _():
        state_ref[...] = jnp.zeros_like(state_ref)
        u_meta = um_ref[...].astype(jnp.bfloat16)
        for h in range(SSM_HALVES):
            expand(u_meta, h, N_META)
            scan_half(h, N_META)

    u = u_ref[...]
    u_bf16 = u.astype(jnp.bfloat16)
    for h in range(SSM_HALVES):
        expand(u_bf16, h, SSM_TC)
    y_halves = []
    for h in range(SSM_HALVES):
        scan_half(h, SSM_TC)
        y_halves.append(readout(h))
    y = jnp.concatenate(y_halves, axis=1)
    y = jax.nn.gelu(y + d_ref[...] * u)
    gate = jnp.dot(y.astype(jnp.bfloat16), wglu_ref[...], preferred_element_type=jnp.float32)
    y_ref[...] = (y * jax.nn.sigmoid(gate)).astype(y_ref.dtype)


def _ssm(u_meta_rows, u_tm, bb, ab, cm, d, wglu, nb):
    rows = u_tm.shape[0]
    blk = SSM_TC * nb
    assert rows % blk == 0 and nb % SUBLANES == 0 and N_META <= SSM_TC
    const = lambda i: (0, 0)
    return pl.pallas_call(
        functools.partial(_ssm_kernel, nb=nb),
        grid=(rows // blk,),
        in_specs=[
            pl.BlockSpec((N_META * nb, SSM_WIDTH), const),
            pl.BlockSpec((blk, SSM_WIDTH), lambda i: (i, 0)),
            pl.BlockSpec((SSM_HALVES, SSM_WIDTH // SSM_HALVES, 2 * SSM_COLS // SSM_HALVES), lambda i: (0, 0, 0)),
            pl.BlockSpec((2, SUBLANES, SSM_COLS), lambda i: (0, 0, 0)),
            pl.BlockSpec((SSM_HALVES, 2 * SSM_COLS // SSM_HALVES, SSM_WIDTH // SSM_HALVES), lambda i: (0, 0, 0)),
            pl.BlockSpec((1, SSM_WIDTH), const),
            pl.BlockSpec((SSM_WIDTH, SSM_WIDTH), const),
        ],
        out_specs=pl.BlockSpec((blk, SSM_WIDTH), lambda i: (i, 0)),
        out_shape=jax.ShapeDtypeStruct((rows, SSM_WIDTH), jnp.bfloat16),
        scratch_shapes=[
            pltpu.VMEM((nb, 2 * SSM_COLS), jnp.float32),
            pltpu.VMEM((blk, 2 * SSM_COLS), jnp.float32),
        ],
        compiler_params=pltpu.CompilerParams(
            dimension_semantics=("arbitrary",), vmem_limit_bytes=VMEM_LIMIT),
        name="ssm",
    )(u_meta_rows, u_tm, bb, ab, cm, d, wglu)


def _ssm_params(a_re, a_im, log_dt, b_re, b_im, c_re, c_im):
    f32 = jnp.float32
    lam_re = jnp.minimum(a_re.astype(f32), -1e-4)
    lam_im = a_im.astype(f32)
    dt = jnp.exp(log_dt.astype(f32))[:, None]
    mag = jnp.exp(lam_re * dt)
    ab_re = mag * jnp.cos(lam_im * dt)
    ab_im = mag * jnp.sin(lam_im * dt)
    den = lam_re * lam_re + lam_im * lam_im
    nr = ab_re - 1.0
    ni = ab_im
    z_re = (nr * lam_re + ni * lam_im) / den
    z_im = (ni * lam_re - nr * lam_im) / den
    br32 = b_re.astype(f32)
    bi32 = b_im.astype(f32)
    bb_re = z_re[..., None] * br32 - z_im[..., None] * bi32
    bb_im = z_re[..., None] * bi32 + z_im[..., None] * br32
    gh = SSM_GROUPS // SSM_HALVES
    eye = jnp.eye(gh, dtype=f32)
    split = lambda a: a.reshape((SSM_HALVES, gh) + a.shape[1:])
    exp_re = jnp.einsum('kgpc,gh->kgchp', split(bb_re), eye).reshape(SSM_HALVES, gh * SSM_GROUP, gh * SSM_STATE)
    exp_im = jnp.einsum('kgpc,gh->kgchp', split(bb_im), eye).reshape(SSM_HALVES, gh * SSM_GROUP, gh * SSM_STATE)
    bb = jnp.concatenate([exp_re, exp_im], axis=2)
    ro_re = jnp.einsum('kgcp,gh->kgphc', split(c_re.astype(f32)), eye).reshape(
        SSM_HALVES, gh * SSM_STATE, gh * SSM_GROUP)
    ro_im = jnp.einsum('kgcp,gh->kgphc', split(c_im.astype(f32)), eye).reshape(
        SSM_HALVES, gh * SSM_STATE, gh * SSM_GROUP)
    cm = jnp.concatenate([ro_re, -ro_im], axis=1)
    ab = jnp.stack([ab_re.reshape(SSM_COLS), ab_im.reshape(SSM_COLS)], axis=0)
    ab = jnp.broadcast_to(ab[:, None, :], (2, SUBLANES, SSM_COLS))
    return bb.astype(jnp.bfloat16), ab, cm.astype(jnp.bfloat16)


def _attn_kernel(lam_ref, q_ref, k_ref, v_ref, km_ref, vm_ref, g_ref, o_ref, vt_ref, *, scale, seq):
    tq = ATTN_TQ
    f32, bf16 = jnp.float32, jnp.bfloat16
    vt_ref[...] = v_ref[...].astype(f32).T.astype(bf16)
    vm_t = vm_ref[...].astype(f32).T.astype(bf16)
    k_meta = km_ref[...]
    lane = lax.broadcasted_iota(jnp.int32, (tq, ATTN_VDIM), 1)
    key = lax.broadcasted_iota(jnp.int32, (tq, 2 * tq), 0)
    qry = lax.broadcasted_iota(jnp.int32, (tq, 2 * tq), 1) % tq
    diag_visible = qry // CHUNK >= key // CHUNK
    col_max = lambda a: jnp.max(a, axis=0, keepdims=True)
    col_sum = lambda a: jnp.sum(a, axis=0, keepdims=True)

    def scores(i):
        rows = slice(i * tq, (i + 1) * tq)
        q = q_ref[rows, :] * scale
        zero = jnp.zeros_like(q)
        qq = jnp.concatenate([jnp.where(lane < ATTN_HEAD_DIM, q, zero),
                              jnp.where(lane >= ATTN_HEAD_DIM, q, zero)], axis=0)
        scores_t = lambda kb: lax.dot_general(kb, qq, (((1,), (1,)), ((), ())), preferred_element_type=f32)
        s_meta = scores_t(k_meta)
        s_diag = jnp.where(diag_visible, scores_t(k_ref[rows, :]), NEG_INF)
        s_prev = scores_t(k_ref[0:i * tq, :]) if i > 0 else None
        return s_meta, s_diag, s_prev

    def softmax(i, s):
        s_meta, s_diag, s_prev = s
        m = jnp.maximum(col_max(s_meta), col_max(s_diag))
        if i > 0:
            m = jnp.maximum(m, col_max(s_prev))
        p_meta = jnp.exp2(s_meta - m)
        p_diag = jnp.exp2(s_diag - m)
        l = col_sum(p_meta) + col_sum(p_diag)
        p_prev = None
        if i > 0:
            p_prev = jnp.exp2(s_prev - m)
            l = l + col_sum(p_prev)
            p_prev = p_prev.astype(bf16)
        return p_meta.astype(bf16), p_diag.astype(bf16), p_prev, l

    def weighted_values(i, p):
        p_meta, p_diag, p_prev, l = p
        rows = slice(i * tq, (i + 1) * tq)
        acc = (jnp.dot(vm_t, p_meta, preferred_element_type=f32)
               + jnp.dot(vt_ref[:, rows], p_diag, preferred_element_type=f32))
        if i > 0:
            acc = acc + jnp.dot(vt_ref[:, 0:i * tq], p_prev, preferred_element_type=f32)
        o_all = acc / l
        o_t = o_all[:, :tq] - lam_ref[0] * o_all[:, tq:]
        o_t = o_t * lax.rsqrt(jnp.mean(o_t * o_t, axis=0, keepdims=True) + LN_EPS)
        o_ref[rows, :] = (o_t.T * g_ref[...]).astype(o_ref.dtype)

    n_blk = seq // tq
    s_next = scores(0)
    p_last = None
    for i in range(n_blk):
        s_cur = s_next
        s_next = scores(i + 1) if i + 1 < n_blk else None
        if p_last is not None:
            weighted_values(i - 1, p_last)
        p_last = softmax(i, s_cur)
    weighted_values(n_blk - 1, p_last)


def _attention(lam, proj, proj_meta, g_scaled, nb, seq):
    assert seq % ATTN_TQ == 0
    nh = ATTN_HEADS
    first = GATE_COLS // ATTN_VDIM
    head = lambda part: pl.BlockSpec((None, seq, ATTN_VDIM), lambda b, h: (b, 0, first + part * nh + h))
    return pl.pallas_call(
        functools.partial(_attn_kernel, scale=ATTN_HEAD_DIM ** -0.5, seq=seq),
        grid=(nb, nh),
        in_specs=[
            pl.BlockSpec(memory_space=pltpu.SMEM),
            head(0), head(1), head(2),
            pl.BlockSpec((N_META, ATTN_VDIM), lambda b, h: (0, h)),
            pl.BlockSpec((N_META, ATTN_VDIM), lambda b, h: (0, nh + h)),
            pl.BlockSpec((1, ATTN_VDIM), lambda b, h: (0, 0)),
        ],
        out_specs=pl.BlockSpec((None, seq, ATTN_VDIM), lambda b, h: (b, 0, h)),
        out_shape=jax.ShapeDtypeStruct((nb, seq, ATTN_WIDTH), jnp.bfloat16),
        scratch_shapes=[pltpu.VMEM((ATTN_VDIM, seq), jnp.bfloat16)],
        compiler_params=pltpu.CompilerParams(
            dimension_semantics=("parallel", "parallel"), vmem_limit_bytes=VMEM_LIMIT),
        name="attention",
    )(lam, proj, proj, proj, proj_meta, proj_meta, g_scaled)


def _merge_kernel(x_ref, gi_ref, bi_ref, gate_ref, ys_ref, ya_ref, wbs_ref, wba_ref, wo_ref,
                  g1_ref, b1_ref, wr_ref, br_ref, h1_ref, h1p_ref, route_ref, route_t_ref, cnt_ref, logits_ref):
    i = pl.program_id(0)
    tm = x_ref.shape[0]

    @pl.when(i == 0)
    def _():
        cnt_ref[...] = jnp.zeros_like(cnt_ref)
        logits_ref[...] = jnp.zeros_like(logits_ref)

    h = _layer_norm(x_ref[...], gi_ref[...], bi_ref[...])
    ps = jnp.dot(ys_ref[...], wbs_ref[...], preferred_element_type=jnp.float32)
    pa = jnp.dot(ya_ref[...], wba_ref[...], preferred_element_type=jnp.float32)

    logits = logits_ref[...]
    lane_i = lax.broadcasted_iota(jnp.int32, (tm, LANES), 1)
    lane = lane_i.astype(jnp.float32)
    is_group = jnp.logical_and(lane_i >= N_EXPERTS, lane_i < N_EXPERTS + N_EXPERT_GROUPS)
    lane_group = (lane_i // EXPERTS_PER_GROUP).astype(jnp.float32)

    def first_argmax(vals):
        mx = jnp.max(vals, axis=-1, keepdims=True)
        idx = jnp.min(jnp.where(vals == mx, lane, float(LANES)), axis=-1, keepdims=True)
        return mx, idx

    gl = jnp.where(is_group, logits, NEG_INF)
    gmax, glane = first_argmax(gl)
    g_val = 1.0 / jnp.sum(jnp.where(is_group, jnp.exp(gl - gmax), 0.0), axis=-1, keepdims=True)
    g_idx = glane - float(N_EXPERTS)
    in_group = jnp.logical_and(lane_i < N_EXPERTS, lane_group == g_idx)
    el = jnp.where(in_group, logits, NEG_INF)
    m1, i1 = first_argmax(el)
    el2 = jnp.where(lane == i1, NEG_INF, el)
    m2, i2 = first_argmax(el2)
    e2 = jnp.exp(m2 - m1)
    w1 = g_val / (1.0 + e2)
    w2 = g_val * e2 / (1.0 + e2)

    oh = jnp.where(jnp.logical_or(lane == i1, lane == i2), 1.0, 0.0)
    r_i = lax.broadcasted_iota(jnp.int32, (tm, tm), 0)
    c_i = lax.broadcasted_iota(jnp.int32, (tm, tm), 1)
    tri = jnp.where(c_i < r_i, 1.0, 0.0).astype(jnp.bfloat16)
    count = cnt_ref[...]
    before = jnp.dot(tri, oh.astype(jnp.bfloat16), preferred_element_type=jnp.float32) + count
    rank1 = jnp.sum(jnp.where(lane == i1, before, 0.0), axis=-1, keepdims=True)
    rank2 = jnp.sum(jnp.where(lane == i2, before, 0.0), axis=-1, keepdims=True)
    cnt_ref[...] = count + jnp.where(i >= 1, jnp.sum(oh, axis=0, keepdims=True), 0.0)

    route = jnp.where(lane_i == 0, i1, 0.0)
    route = jnp.where(lane_i == 1, i2, route)
    route = jnp.where(lane_i == 2, w1, route)
    route = jnp.where(lane_i == 3, w2, route)
    route = jnp.where(lane_i == 4, rank1, route)
    route = jnp.where(lane_i == 5, rank2, route)
    route_ref[...] = route
    route_t_ref[...] = route.T[0:ROUTE_ROWS, :]

    gates = gate_ref[...].astype(jnp.float32)
    merged = gates[:, :D_MODEL] * ps + gates[:, D_MODEL:] * pa
    mix = jnp.dot(merged.astype(jnp.bfloat16), wo_ref[...], preferred_element_type=jnp.float32)
    h1 = _layer_norm(DEEPNORM_ALPHA * h + mix, g1_ref[...], b1_ref[...])
    h1_ref[...] = h1
    h1p_ref[...] = _pack_halves(h1)
    logits_ref[...] = jnp.dot(h1.astype(jnp.bfloat16), wr_ref[...], preferred_element_type=jnp.float32) + br_ref[...]


def _merge(x2, gi, bi, gates, ys, ya, wbs, wba, wo, g1, b1, wr, br):
    t_rows = x2.shape[0]
    tm = MERGE_TM
    assert t_rows % tm == 0
    n_tiles = t_rows // tm
    row = lambda i: (jnp.minimum(i, n_tiles - 1), 0)
    routed = lambda i: (jnp.maximum(i - 1, 0), 0)
    const = lambda i: (0, 0)
    return pl.pallas_call(
        _merge_kernel,
        grid=(n_tiles + 1,),
        in_specs=[
            pl.BlockSpec((tm, D_MODEL), row),
            pl.BlockSpec((1, D_MODEL), const),
            pl.BlockSpec((1, D_MODEL), const),
            pl.BlockSpec((tm, GATE_COLS), row),
            pl.BlockSpec((tm, SSM_WIDTH), row),
            pl.BlockSpec((tm, ATTN_WIDTH), row),
            pl.BlockSpec((SSM_WIDTH, D_MODEL), const, pipeline_mode=pl.Buffered(1)),
            pl.BlockSpec((ATTN_WIDTH, D_MODEL), const, pipeline_mode=pl.Buffered(1)),
            pl.BlockSpec((D_MODEL, D_MODEL), const, pipeline_mode=pl.Buffered(1)),
            pl.BlockSpec((1, D_MODEL), const),
            pl.BlockSpec((1, D_MODEL), const),
            pl.BlockSpec((D_MODEL, LANES), const),
            pl.BlockSpec((1, LANES), const),
        ],
        out_specs=[
            pl.BlockSpec((tm, D_MODEL), row),
            pl.BlockSpec((tm, D_MODEL // 2), row),
            pl.BlockSpec((tm, LANES), routed),
            pl.BlockSpec((ROUTE_ROWS, tm), lambda i: (0, jnp.maximum(i - 1, 0))),
            pl.BlockSpec((1, LANES), const),
        ],
        scratch_shapes=[pltpu.VMEM((tm, LANES), jnp.float32)],
        out_shape=[
            jax.ShapeDtypeStruct((t_rows, D_MODEL), jnp.float32),
            jax.ShapeDtypeStruct((t_rows, D_MODEL // 2), jnp.uint32),
            jax.ShapeDtypeStruct((t_rows, LANES), jnp.float32),
            jax.ShapeDtypeStruct((ROUTE_ROWS, t_rows), jnp.float32),
            jax.ShapeDtypeStruct((1, LANES), jnp.float32),
        ],
        compiler_params=pltpu.CompilerParams(
            dimension_semantics=("arbitrary",), vmem_limit_bytes=VMEM_LIMIT),
        name="merge",
    )(x2, gi, bi, gates, ys, ya, wbs, wba, wo, g1, b1, wr, br)


def _row_map_kernel(lo_ref, hi_ref, dest0_ref, dest1_ref, table_ref, *, t_rows):
    i = pl.program_id(0)
    tm = ROWMAP_TM
    stride = t_rows + DUMP_ROWS

    @pl.when(i == 0)
    def _():
        for e in range(N_EXPERTS + 1):
            def fill(r, carry):
                table_ref[r] = t_rows + (r & (DUMP_ROWS - 1))
                return carry
            lax.fori_loop(lo_ref[e], hi_ref[e], fill, 0)

    base = i * tm
    for a in range(tm):
        table_ref[dest0_ref[0, a]] = base + a
        table_ref[dest1_ref[0, a]] = base + (a + stride)


def _row_map(fill_lo, fill_hi, dest4, n_rows, t_rows):
    tm = ROWMAP_TM
    assert t_rows % tm == 0 and ROW_BLOCK & (ROW_BLOCK - 1) == 0
    grid_spec = pltpu.PrefetchScalarGridSpec(
        num_scalar_prefetch=2,
        grid=(t_rows // tm,),
        in_specs=[pl.BlockSpec((None, None, 1, tm), lambda i, lo, hi, k=k: (k, i, 0, 0), memory_space=pltpu.SMEM)
                  for k in range(TOP_K)],
        out_specs=pl.BlockSpec(memory_space=pltpu.SMEM),
    )
    return pl.pallas_call(
        functools.partial(_row_map_kernel, t_rows=t_rows),
        grid_spec=grid_spec,
        out_shape=jax.ShapeDtypeStruct((n_rows,), jnp.int32),
        compiler_params=pltpu.CompilerParams(dimension_semantics=("arbitrary",)),
        name="row_map",
    )(fill_lo, fill_hi, dest4, dest4)


def _moe_kernel(be_ref, nu_ref, src0_ref, src1_ref, src2_ref, dst0_ref, dst1_ref, h1_hbm, wg_ref, wu_ref,
                wd_ref, ysc_hbm, xbuf0, xbuf1, xbuf2, ybuf0, ybuf1, ybuf2, wgb, wub, wdb, gsem, ssem):
    i = pl.program_id(0)
    n_used = nu_ref[0]
    rb = ROW_BLOCK
    ns = MOE_SLOTS
    xbuf = (xbuf0, xbuf1, xbuf2)
    ybuf = (ybuf0, ybuf1, ybuf2)

    def start_gather(idx_ref, s):
        for r in range(rb):
            pltpu.make_async_copy(h1_hbm.at[pl.ds(idx_ref[0, r], 1)], xbuf[s].at[pl.ds(r, 1)],
                                  gsem.at[s]).start(priority=1)

    def wait_gather(s):
        pltpu.make_async_copy(h1_hbm.at[pl.ds(0, rb)], xbuf[s], gsem.at[s]).wait()

    def start_scatter(idx_ref, s):
        for r in range(rb):
            pltpu.make_async_copy(ybuf[s].at[pl.ds(r, 1)], ysc_hbm.at[pl.ds(idx_ref[0, r], 1)],
                                  ssem.at[s]).start(priority=r % 2)

    def wait_scatter(s):
        pltpu.make_async_copy(ybuf[s], ysc_hbm.at[pl.ds(0, rb)], ssem.at[s]).wait()

    @pl.when(i == 0)
    def _():
        ybuf1[...] = jnp.zeros_like(ybuf1)
        ybuf2[...] = jnp.zeros_like(ybuf2)
        start_gather(src0_ref, 0)
        start_gather(src1_ref, 1)
        start_scatter(dst0_ref, 1)

    expert_changed = jnp.logical_or(i == 0, be_ref[i] != be_ref[jnp.maximum(i - 1, 0)])

    @pl.when(jnp.logical_and(i <= n_used, expert_changed))
    def _():
        wgb[...] = wg_ref[...].astype(jnp.bfloat16)
        wub[...] = wu_ref[...].astype(jnp.bfloat16)
        wdb[...] = wd_ref[...].astype(jnp.bfloat16)

    def step(slot):
        ahead = (slot + 2) % ns
        wait_gather(slot)

        @pl.when(i >= 1)
        def _():
            wait_scatter(slot)

        start_gather(src2_ref, ahead)
        start_scatter(dst1_ref, ahead)
        xb = jnp.concatenate([half.astype(jnp.bfloat16) for half in _unpack_halves(xbuf[slot][...])], axis=1)
        g = jnp.dot(xb, wgb[...], preferred_element_type=jnp.float32)
        u = jnp.dot(xb, wub[...], preferred_element_type=jnp.float32)
        hb = (jax.nn.silu(g) * u).astype(jnp.bfloat16)
        ybuf[slot][...] = _pack_halves(jnp.dot(hb, wdb[...], preferred_element_type=jnp.float32))

        @pl.when(i == n_used)
        def _():
            for s in ((slot + 1) % ns, ahead):
                wait_gather(s)
                wait_scatter(s)

    for s in range(ns):
        pl.when(jnp.logical_and(i <= n_used, i % ns == s))(functools.partial(step, s))


def _moe(block_exp, n_used, src_blocks, dst_blocks, h1, wg, wu, wd, n_blocks):
    t_rows = h1.shape[0]
    rb = ROW_BLOCK
    assert MOE_SLOTS == 3
    wsel = lambda i, be, nu: (be[i], 0, 0)
    smem_blk = lambda off: pl.BlockSpec((None, 1, rb), lambda i, be, nu: (i + off, 0, 0),
                                        memory_space=pltpu.SMEM)
    row_buf = pltpu.VMEM((rb, D_MODEL // 2), jnp.uint32)
    grid_spec = pltpu.PrefetchScalarGridSpec(
        num_scalar_prefetch=2,
        grid=(n_blocks + 1,),
        in_specs=[
            smem_blk(0), smem_blk(1), smem_blk(2), smem_blk(0), smem_blk(1),
            pl.BlockSpec(memory_space=pl.ANY),
            pl.BlockSpec((None, D_MODEL, D_FF_EXPERT), wsel),
            pl.BlockSpec((None, D_MODEL, D_FF_EXPERT), wsel),
            pl.BlockSpec((None, D_FF_EXPERT, D_MODEL), wsel),
        ],
        out_specs=pl.BlockSpec(memory_space=pl.ANY),
        scratch_shapes=[row_buf] * (2 * MOE_SLOTS) + [
            pltpu.VMEM((D_MODEL, D_FF_EXPERT), jnp.bfloat16),
            pltpu.VMEM((D_MODEL, D_FF_EXPERT), jnp.bfloat16),
            pltpu.VMEM((D_FF_EXPERT, D_MODEL), jnp.bfloat16),
            pltpu.SemaphoreType.DMA((MOE_SLOTS,)),
            pltpu.SemaphoreType.DMA((MOE_SLOTS,)),
        ],
    )
    return pl.pallas_call(
        _moe_kernel,
        grid_spec=grid_spec,
        out_shape=jax.ShapeDtypeStruct((TOP_K * t_rows + 2 * rb, D_MODEL // 2), jnp.uint32),
        compiler_params=pltpu.CompilerParams(
            dimension_semantics=("arbitrary",), vmem_limit_bytes=VMEM_LIMIT),
        name="experts",
    )(block_exp, n_used, src_blocks, src_blocks, src_blocks, dst_blocks, dst_blocks, h1, wg, wu, wd)


def _combine_kernel(h1_ref, route_ref, y0_ref, y1_ref, g_ref, b_ref, o_ref):
    route = route_ref[...]
    y0 = _unpack_halves(y0_ref[...])
    y1 = _unpack_halves(y1_ref[...])
    ffn = jnp.concatenate([route[:, 2:3] * a + route[:, 3:4] * b for a, b in zip(y0, y1)], axis=1)
    o_ref[...] = _layer_norm(DEEPNORM_ALPHA * h1_ref[...] + ffn, g_ref[...], b_ref[...])


def _combine(h1, route, ysc, g2, b2):
    t_rows = h1.shape[0]
    tm = COMBINE_TM
    assert t_rows % tm == 0 and DUMP_ROWS % tm == 0
    second = (t_rows + DUMP_ROWS) // tm
    row = lambda i: (i, 0)
    const = lambda i: (0, 0)
    return pl.pallas_call(
        _combine_kernel,
        grid=(t_rows // tm,),
        in_specs=[
            pl.BlockSpec((tm, D_MODEL), row),
            pl.BlockSpec((tm, LANES), row),
            pl.BlockSpec((tm, D_MODEL // 2), row),
            pl.BlockSpec((tm, D_MODEL // 2), lambda i: (i + second, 0)),
            pl.BlockSpec((1, D_MODEL), const),
            pl.BlockSpec((1, D_MODEL), const),
        ],
        out_specs=pl.BlockSpec((tm, D_MODEL), row),
        out_shape=jax.ShapeDtypeStruct((t_rows, D_MODEL), jnp.float32),
        compiler_params=pltpu.CompilerParams(
            dimension_semantics=("parallel",), vmem_limit_bytes=VMEM_LIMIT),
        name="combine",
    )(h1, route, ysc, ysc, g2, b2)


def kernel(x, meta_tokens, ln_in_g, ln_in_b, w_in, ssm_a_re, ssm_a_im, ssm_log_dt, ssm_b_re, ssm_b_im,
           ssm_c_re, ssm_c_im, ssm_d, ssm_w_glu, attn_lambda_q1, attn_lambda_k1, attn_lambda_q2,
           attn_lambda_k2, attn_subln_g, w_br_ssm, w_br_attn, w_o, ln1_g, ln1_b, router_g_w, router_g_b,
           router_e_w, router_e_b, exp_w_gate, exp_w_up, exp_w_down, ln2_g, ln2_b):
    f32, bf16 = jnp.float32, jnp.bfloat16
    nb, seq, d = x.shape
    assert d == D_MODEL and w_in.shape[0] == DEPTH == 1
    t_rows = nb * seq
    l = 0
    lambda_init = 0.8 - 0.6 * math.exp(-0.3 * l)
    row2 = lambda v: v.reshape(1, -1).astype(f32)

    x2 = x.reshape(t_rows, d)
    w_u = w_in[l, :, :SSM_WIDTH].astype(bf16)
    q_end = SSM_WIDTH + ATTN_WIDTH
    w_rest = jnp.concatenate([w_in[l, :, SSM_WIDTH + QKV_COLS:], w_in[l, :, SSM_WIDTH:q_end] * LOG2_E,
                              w_in[l, :, q_end:SSM_WIDTH + QKV_COLS]], axis=1).astype(bf16)
    gi, bi = row2(ln_in_g), row2(ln_in_b)
    u, proj = _ln_proj(x2, gi, bi, w_u, w_rest, min(PROJ_TM, seq), GATE_COLS)
    u_meta, proj_meta = _ln_proj(meta_tokens.astype(f32), gi, bi, w_u, w_rest, N_META, 0,
                                 first_col=GATE_COLS + ATTN_WIDTH)

    bb, ab, cm = _ssm_params(ssm_a_re[l], ssm_a_im[l], ssm_log_dt[l], ssm_b_re[l], ssm_b_im[l],
                             ssm_c_re[l], ssm_c_im[l])
    u_tm = u.reshape(nb, seq, SSM_WIDTH).transpose(1, 0, 2).reshape(seq * nb, SSM_WIDTH)
    u_meta_rows = jnp.repeat(u_meta, nb, axis=0)
    y_ssm_tm = _ssm(u_meta_rows, u_tm, bb, ab, cm, row2(ssm_d[l]), ssm_w_glu[l].astype(bf16), nb)
    y_ssm = y_ssm_tm.reshape(seq, nb, SSM_WIDTH).transpose(1, 0, 2).reshape(t_rows, SSM_WIDTH)

    lam = (jnp.exp(jnp.sum(attn_lambda_q1[l].astype(f32) * attn_lambda_k1[l].astype(f32)))
           - jnp.exp(jnp.sum(attn_lambda_q2[l].astype(f32) * attn_lambda_k2[l].astype(f32)))
           + lambda_init).reshape(1)
    g_scaled = row2(attn_subln_g[l]) * (1.0 - lambda_init)
    y_attn = _attention(lam, proj.reshape(nb, seq, GATE_COLS + QKV_COLS), proj_meta, g_scaled, nb, seq)
    y_attn = y_attn.reshape(t_rows, ATTN_WIDTH)

    w_r = jnp.concatenate([router_e_w[l].astype(f32), router_g_w[l].astype(f32)], axis=1)
    w_r = jnp.pad(w_r, ((0, 0), (0, LANES - w_r.shape[1])))
    b_r = jnp.concatenate([router_e_b[l].astype(f32), router_g_b[l].astype(f32)])
    b_r = jnp.pad(b_r, (0, LANES - b_r.shape[0])).reshape(1, LANES)
    h1, h1_packed, route, route_t, cnt = _merge(x2, gi, bi, proj, y_ssm, y_attn, w_br_ssm[l].astype(bf16),
                                       w_br_attn[l].astype(bf16), w_o[l].astype(bf16), row2(ln1_g[l]),
                                       row2(ln1_b[l]), w_r.astype(bf16), b_r)

    counts = cnt[0, :N_EXPERTS].astype(jnp.int32)
    padded = (counts + ROW_BLOCK - 1) // ROW_BLOCK * ROW_BLOCK
    pad_ends = jnp.cumsum(padded)
    pad_starts = pad_ends - padded
    expert = route_t[0:TOP_K].astype(jnp.int32)
    rank = route_t[4:4 + TOP_K].astype(jnp.int32)
    first_row = jnp.sum(jnp.where(expert[None] == jnp.arange(N_EXPERTS, dtype=jnp.int32)[:, None, None],
                                  pad_starts[:, None, None], 0), axis=0)
    dest = first_row + rank
    n_blocks = -(-(t_rows * TOP_K + N_EXPERTS * (ROW_BLOCK - 1)) // ROW_BLOCK)
    block_start = jnp.arange(n_blocks, dtype=jnp.int32) * ROW_BLOCK
    block_exp = jnp.minimum(jnp.sum(pad_ends[None, :] <= block_start[:, None], axis=1),
                            N_EXPERTS - 1).astype(jnp.int32)
    block_exp = jnp.concatenate([block_exp, block_exp[-1:]])
    n_used = (pad_ends[-1:] // ROW_BLOCK).astype(jnp.int32)

    rb = ROW_BLOCK
    stride = t_rows + DUMP_ROWS
    fill_lo = jnp.concatenate([pad_starts + counts, pad_ends[-1:]]).astype(jnp.int32)
    fill_hi = jnp.concatenate([pad_ends, jnp.full((1,), n_blocks * rb)]).astype(jnp.int32)
    row_map = _row_map(fill_lo, fill_hi, dest.reshape(TOP_K, t_rows // ROWMAP_TM, 1, ROWMAP_TM),
                       n_blocks * rb, t_rows)
    dst_blocks = jnp.concatenate([t_rows + jnp.arange(DUMP_ROWS, dtype=jnp.int32), row_map]).reshape(
        n_blocks + 2, 1, rb)
    src = jnp.minimum(row_map % stride, t_rows - 1)
    src_blocks = jnp.concatenate([src, jnp.zeros((3 * rb,), jnp.int32)]).reshape(n_blocks + 3, 1, rb)

    ysc = _moe(block_exp, n_used, src_blocks, dst_blocks, h1_packed, exp_w_gate[l], exp_w_up[l],
               exp_w_down[l], n_blocks)
    out = _combine(h1, route, ysc, row2(ln2_g[l]), row2(ln2_b[l]))
    return out.reshape(nb, seq, d).astype(x.dtype)
```
